```python
import math
import jax, jax.numpy as jnp
from jax import lax
import numpy as np

D_MODEL = 2048
BATCH = 4
SEQ = 4096
DEPTH = 4

CHUNK = 64
Q_BLOCK = 128
N_MIXERS = 4
EPS = 1e-6
NEG_INF = -1e30

A_WIDTH = D_MODEL
A_HEADS = 16
A_HEAD_DIM = A_WIDTH // (2 * A_HEADS)
T5_BUCKETS = 32
T5_MAX_DIST = 128

B_HEADS = 4
B_DK = D_MODEL // 2 // B_HEADS
B_DV = D_MODEL // B_HEADS
B_GATE_RANK = 16
B_GATE_TAU = 16.0

C_WIDTH = D_MODEL
C_BLOCKS = 8
C_BLOCK_DIM = C_WIDTH // C_BLOCKS
C_CONV = 4
C_C = 8.0

D_HEADS = 16
D_Q_RANK = 512
D_KV_RANK = 512
D_NOPE = 128
D_ROPE = 64
D_V = 128
ROPE_THETA = 10000.0


def _n_layers_of(m):
    return (DEPTH - m + N_MIXERS - 1) // N_MIXERS


N_A = _n_layers_of(0)
N_B = _n_layers_of(1)
N_C = _n_layers_of(2)
N_D = _n_layers_of(3)

kernel_name = "hybrid_chunk_causal_interleaved_trunk"


def _rmsnorm(x, g):
    xf = x.astype(jnp.float32)
    y = xf * lax.rsqrt(jnp.mean(xf * xf, axis=-1, keepdims=True) + EPS)
    return (y * g.astype(jnp.float32)).astype(x.dtype)


def _chunk_mask(qpos, kpos):
    return (kpos[None, :] // CHUNK) <= (qpos[:, None] // CHUNK)


def _t5_bucket(rel):
    nb = T5_BUCKETS // 2
    max_exact = nb // 2
    ret = jnp.where(rel > 0, nb, 0)
    n = jnp.abs(rel)
    nf = jnp.maximum(n, 1).astype(jnp.float32)
    large = max_exact + (jnp.log(nf / max_exact) / math.log(T5_MAX_DIST / max_exact)
                         * (nb - max_exact)).astype(jnp.int32)
    large = jnp.minimum(large, nb - 1)
    return ret + jnp.where(n < max_exact, n, large)


def _rope(x, pos):
    half = x.shape[-1] // 2
    inv = ROPE_THETA ** (-jnp.arange(half, dtype=jnp.float32) / half)
    ang = pos[:, None] * inv[None, :]
    cos = jnp.cos(ang)[None, :, None, :]
    sin = jnp.sin(ang)[None, :, None, :]
    xf = x.astype(jnp.float32)
    x1, x2 = xf[..., :half], xf[..., half:]
    return jnp.concatenate([x1 * cos - x2 * sin, x2 * cos + x1 * sin], axis=-1).astype(x.dtype)


def _diff_attention(h, w_in, qk_g, lam_vecs, subln_g, w_out, rel_bias, layer_idx):
    B, S, _ = h.shape
    H, d = A_HEADS, A_HEAD_DIM
    q, k, v, g = jnp.split(h @ w_in, 4, axis=-1)
    q = _rmsnorm(q.reshape(B, S, H, 2, d), qk_g[0]) * (d ** -0.5)
    k = _rmsnorm(k.reshape(B, S, H, 2, d), qk_g[1])
    v = v.reshape(B, S, H, 2 * d)
    lf = lam_vecs.astype(jnp.float32)
    lam_init = 0.8 - 0.6 * math.exp(-0.3 * layer_idx)
    lam = jnp.exp(jnp.sum(lf[0] * lf[1])) - jnp.exp(jnp.sum(lf[2] * lf[3])) + lam_init
    kpos = jnp.arange(S)
    nq = S // Q_BLOCK
    qb = q.reshape(B, nq, Q_BLOCK, H, 2, d).transpose(1, 0, 2, 3, 4, 5)

    def block(args):
        qblk, j = args
        qpos = j * Q_BLOCK + jnp.arange(Q_BLOCK)
        bias = rel_bias[_t5_bucket(kpos[None, :] - qpos[:, None])]
        s = (jnp.einsum('bqhtd,bkhtd->bthqk', qblk, k).astype(jnp.float32)
             + jnp.transpose(bias, (2, 0, 1)).astype(jnp.float32))
        s = jnp.where(_chunk_mask(qpos, kpos), s, NEG_INF)
        p = jax.nn.softmax(s, axis=-1)
        attn = p[:, 0] - lam * p[:, 1]
        return jnp.einsum('bhqk,bkhe->bqhe', attn.astype(v.dtype), v)

    o = lax.map(block, (qb, jnp.arange(nq)))
    o = o.transpose(1, 0, 2, 3, 4).reshape(B, S, H, 2 * d)
    o = _rmsnorm(o, subln_g) * (1.0 - lam_init)
    y = o.reshape(B, S, A_WIDTH) * jax.nn.silu(g)
    return y @ w_out


def _gla(h, w_in, w_gate, gate_bias, out_g, w_out):
    B, S, _ = h.shape
    H, dk, dv = B_HEADS, B_DK, B_DV
    nc = S // CHUNK
    q, k, v, g, lr = jnp.split(
        h @ w_in, [H * dk, 2 * H * dk, 2 * H * dk + H * dv, 2 * H * dk + 2 * H * dv], axis=-1)
    log_alpha = jax.nn.log_sigmoid((lr @ w_gate + gate_bias).astype(jnp.float32)) / B_GATE_TAU

    def chunks(t, e):
        return t.astype(jnp.float32).reshape(B, nc, CHUNK, H, e).transpose(1, 0, 2, 3, 4)

    qc = chunks(q, dk) * (dk ** -0.5)
    kc = chunks(k, dk)
    vc = chunks(v, dv)
    cum = jnp.cumsum(chunks(log_alpha, dk), axis=2)
    total = cum[:, :, -1]
    kc = kc * jnp.exp(total[:, :, None] - cum)

    def step(state, xs):
        qi, ki, vi, ti = xs
        state = jnp.exp(ti)[..., None] * state + jnp.einsum('bchk,bchv->bhkv', ki, vi)
        return state, jnp.einsum('bchk,bhkv->bchv', qi, state)

    s0 = jnp.zeros((B, H, dk, dv), jnp.float32)
    _, o = lax.scan(step, s0, (qc, kc, vc, total))
    o = o.transpose(1, 0, 2, 3, 4).reshape(B, S, H, dv)
    o = _rmsnorm(o, out_g).reshape(B, S, H * dv).astype(h.dtype)
    return (o * jax.nn.silu(g)) @ w_out


def _rglru(h, w_in, conv_w, conv_b, w_rg, b_rg, w_ig, b_ig, lam, w_out):
    B, S, _ = h.shape
    u, g = jnp.split(h @ w_in, 2, axis=-1)
    up = jnp.pad(u, ((0, 0), (C_CONV - 1, 0), (0, 0)))
    xc = conv_b + up[:, 0:S] * conv_w[0]
    for t in range(1, C_CONV):
        xc = xc + up[:, t:t + S] * conv_w[t]
    xb = xc.reshape(B, S, C_BLOCKS, C_BLOCK_DIM)
    r = jax.nn.sigmoid((jnp.einsum('bsnd,nde->bsne', xb, w_rg).reshape(B, S, C_WIDTH)
                        + b_rg).astype(jnp.float32))
    i = jax.nn.sigmoid((jnp.einsum('bsnd,nde->bsne', xb, w_ig).reshape(B, S, C_WIDTH)
                        + b_ig).astype(jnp.float32))
    log_a = -C_C * r * jax.nn.softplus(-lam.astype(jnp.float32))
    a = jnp.exp(log_a)
    xin = jnp.sqrt(-jnp.expm1(2.0 * log_a)) * (i * xc.astype(jnp.float32))

    def combine(left, right):
        a_l, b_l = left
        a_r, b_r = right
        return a_l * a_r, a_r * b_l + b_r

    _, hs = lax.associative_scan(combine, (a, xin), axis=1)
    y = hs.astype(h.dtype) * jax.nn.silu(g)
    return y @ w_out


def _chunk_causal_attention(q, k, v):
    B, S, H, dq = q.shape
    dv = v.shape[-1]
    nq = S // Q_BLOCK
    kpos = jnp.arange(S)
    qb = q.reshape(B, nq, Q_BLOCK, H, dq).transpose(1, 0, 2, 3, 4)

    def block(args):
        qblk, j = args
        qpos = j * Q_BLOCK + jnp.arange(Q_BLOCK)
        s = jnp.einsum('bqhd,bkhd->bhqk', qblk, k).astype(jnp.float32)
        s = jnp.where(_chunk_mask(qpos, kpos), s, NEG_INF)
        p = jax.nn.softmax(s, axis=-1)
        return jnp.einsum('bhqk,bkhe->bqhe', p.astype(v.dtype), v)

    o = lax.map(block, (qb, jnp.arange(nq)))
    return o.transpose(1, 0, 2, 3, 4).reshape(B, S, H, dv)


def _mla(h, w_in, q_lat_g, kv_lat_g, w_uq, w_ukv, qk_g, w_out):
    B, S, _ = h.shape
    H = D_HEADS
    dqk = D_NOPE + D_ROPE
    cq, ckv, k_pe, g = jnp.split(
        h @ w_in, [D_Q_RANK, D_Q_RANK + D_KV_RANK, D_Q_RANK + D_KV_RANK + D_ROPE], axis=-1)
    q = (_rmsnorm(cq, q_lat_g) @ w_uq).reshape(B, S, H, dqk)
    kv = (_rmsnorm(ckv, kv_lat_g) @ w_ukv).reshape(B, S, H, D_NOPE + D_V)
    k_nope, v = kv[..., :D_NOPE], kv[..., D_NOPE:]
    pos = jnp.arange(S, dtype=jnp.float32)
    q_nope = _rmsnorm(q[..., :D_NOPE], qk_g[0, :D_NOPE])
    q_pe = _rope(_rmsnorm(q[..., D_NOPE:], qk_g[0, D_NOPE:]), pos)
    k_nope = _rmsnorm(k_nope, qk_g[1, :D_NOPE])
    k_pe = _rope(_rmsnorm(k_pe, qk_g[1, D_NOPE:])[:, :, None, :], pos)
    q = jnp.concatenate([q_nope, q_pe], axis=-1) * (dqk ** -0.5)
    k = jnp.concatenate([k_nope, jnp.broadcast_to(k_pe, (B, S, H, D_ROPE))], axis=-1)
    o = _chunk_causal_attention(q, k, v)
    y = o.reshape(B, S, H * D_V) * jax.nn.silu(g)
    return y @ w_out


def setup_inputs(seed: int = 0) -> dict:
    key = jax.random.key(seed)
    ks = jax.random.split(key, 29)
    f32 = jnp.float32

    def nrm(k, shape, scale):
        return jax.random.normal(k, shape, f32) * scale

    def gain(k, shape):
        return 1.0 + 0.02 * jax.random.normal(k, shape, f32)

    a_in = 4 * A_WIDTH
    b_in = 2 * B_HEADS * B_DK + 2 * B_HEADS * B_DV + B_GATE_RANK
    c_in = 2 * C_WIDTH
    d_in = D_Q_RANK + D_KV_RANK + D_ROPE + D_HEADS * D_V
    u = jax.random.uniform(ks[20], (N_C, C_WIDTH), f32, minval=0.9, maxval=0.999)
    a0 = u ** (1.0 / C_C)
    c_lambda = jnp.log(a0) - jnp.log1p(-a0)
    return {
        "x": nrm(ks[0], (BATCH, SEQ, D_MODEL), 1.0),
        "norm_g": gain(ks[1], (DEPTH, D_MODEL)),
        "rel_bias": nrm(ks[2], (T5_BUCKETS, A_HEADS), 0.2),
        "a_w_in": nrm(ks[3], (N_A, D_MODEL, a_in), D_MODEL ** -0.5),
        "a_qk_g": gain(ks[4], (N_A, 2, A_HEAD_DIM)),
        "a_lambda": nrm(ks[5], (N_A, 4, A_HEAD_DIM), 0.1),
        "a_subln_g": gain(ks[6], (N_A, 2 * A_HEAD_DIM)),
        "a_w_out": nrm(ks[7], (N_A, A_WIDTH, D_MODEL), A_WIDTH ** -0.5),
        "b_w_in": nrm(ks[8], (N_B, D_MODEL, b_in), D_MODEL ** -0.5),
        "b_w_gate": nrm(ks[9], (N_B, B_GATE_RANK, B_HEADS * B_DK), B_GATE_RANK ** -0.5),
        "b_gate_bias": nrm(ks[10], (N_B, B_HEADS * B_DK), 0.1),
        "b_out_g": gain(ks[11], (N_B, B_DV)),
        "b_w_out": nrm(ks[12], (N_B, B_HEADS * B_DV, D_MODEL), (B_HEADS * B_DV) ** -0.5),
        "c_w_in": nrm(ks[13], (N_C, D_MODEL, c_in), D_MODEL ** -0.5),
        "c_conv_w": nrm(ks[14], (N_C, C_CONV, C_WIDTH), C_CONV ** -0.5),
        "c_conv_b": nrm(ks[15], (N_C, C_WIDTH), 0.02),
        "c_w_rgate": nrm(ks[16], (N_C, C_BLOCKS, C_BLOCK_DIM, C_BLOCK_DIM), C_BLOCK_DIM ** -0.5),
        "c_b_rgate": nrm(ks[17], (N_C, C_WIDTH), 0.02),
        "c_w_igate": nrm(ks[18], (N_C, C_BLOCKS, C_BLOCK_DIM, C_BLOCK_DIM), C_BLOCK_DIM ** -0.5),
        "c_b_igate": nrm(ks[19], (N_C, C_WIDTH), 0.02),
        "c_lambda": c_lambda,
        "c_w_out": nrm(ks[21], (N_C, C_WIDTH, D_MODEL), C_WIDTH ** -0.5),
        "d_w_in": nrm(ks[22], (N_D, D_MODEL, d_in), D_MODEL ** -0.5),
        "d_q_lat_g": gain(ks[23], (N_D, D_Q_RANK)),
        "d_kv_lat_g": gain(ks[24], (N_D, D_KV_RANK)),
        "d_w_uq": nrm(ks[25], (N_D, D_Q_RANK, D_HEADS * (D_NOPE + D_ROPE)), D_Q_RANK ** -0.5),
        "d_w_ukv": nrm(ks[26], (N_D, D_KV_RANK, D_HEADS * (D_NOPE + D_V)), D_KV_RANK ** -0.5),
        "d_qk_g": gain(ks[27], (N_D, 2, D_NOPE + D_ROPE)),
        "d_w_out": nrm(ks[28], (N_D, D_HEADS * D_V, D_MODEL), (D_HEADS * D_V) ** -0.5),
    }


def reference(x, norm_g, rel_bias,
              a_w_in, a_qk_g, a_lambda, a_subln_g, a_w_out,
              b_w_in, b_w_gate, b_gate_bias, b_out_g, b_w_out,
              c_w_in, c_conv_w, c_conv_b, c_w_rgate, c_b_rgate, c_w_igate, c_b_igate,
              c_lambda, c_w_out,
              d_w_in, d_q_lat_g, d_kv_lat_g, d_w_uq, d_w_ukv, d_qk_g, d_w_out):
    for i in range(DEPTH):
        m, j = i % N_MIXERS, i // N_MIXERS
        h = _rmsnorm(x, norm_g[i])
        if m == 0:
            y = _diff_attention(h, a_w_in[j], a_qk_g[j], a_lambda[j], a_subln_g[j],
                                a_w_out[j], rel_bias, i)
        elif m == 1:
            y = _gla(h, b_w_in[j], b_w_gate[j], b_gate_bias[j], b_out_g[j], b_w_out[j])
        elif m == 2:
            y = _rglru(h, c_w_in[j], c_conv_w[j], c_conv_b[j], c_w_rgate[j], c_b_rgate[j],
                       c_w_igate[j], c_b_igate[j], c_lambda[j], c_w_out[j])
        else:
            y = _mla(h, d_w_in[j], d_q_lat_g[j], d_kv_lat_g[j], d_w_uq[j], d_w_ukv[j],
                     d_qk_g[j], d_w_out[j])
        x = x + y.astype(x.dtype)
    return x
```

```python
import functools
import math

import jax
import jax.numpy as jnp
from jax import lax
from jax.experimental import pallas as pl
from jax.experimental.pallas import tpu as pltpu

F32 = jnp.float32
BF16 = jnp.bfloat16

EPS = 1e-6
NEG_INF = -1e30
CHUNK = 64
LANES = 128
V7X_VMEM_BYTES = 64 * 1024 * 1024
VMEM_LIMIT = V7X_VMEM_BYTES - 8 * 1024 * 1024

A_HEADS = 16
A_HEAD_DIM = 64
T5_BUCKETS = 32
B_HEADS = 4
B_GATE_TAU = 16.0
C_BLOCKS = 8
C_CONV = 4
C_C = 8.0
D_HEADS = 16
D_NOPE = 128
D_ROPE = 64
D_V = 128
ROPE_THETA = 10000.0

ATTN_TQ = 256


def _cparams(*semantics):
    return pltpu.CompilerParams(dimension_semantics=semantics,
                                vmem_limit_bytes=VMEM_LIMIT)


def _silu(g):
    return g * jax.nn.sigmoid(g)


def _softplus(x):
    return jnp.maximum(x, 0.0) + jnp.log(1.0 + jnp.exp(-jnp.abs(x)))


_EXPM1_SERIES_BOUND = 0.35
_EXPM1_SERIES_TERMS = 9


def _expm1(x):
    poly = jnp.full_like(x, 1.0 / math.factorial(_EXPM1_SERIES_TERMS))
    for n in range(_EXPM1_SERIES_TERMS - 1, 0, -1):
        poly = poly * x + 1.0 / math.factorial(n)
    return jnp.where(jnp.abs(x) < _EXPM1_SERIES_BOUND, poly * x, jnp.exp(x) - 1.0)


def _nt_dot(a, b):
    return lax.dot_general(a, b, (((1,), (1,)), ((), ())),
                           preferred_element_type=F32)


def _norm_matmul_kernel(x_ref, g_ref, w_ref, o_ref, h_ref):
    @pl.when(pl.program_id(1) == 0)
    def _():
        x = x_ref[...].astype(F32)
        ms = jnp.mean(x * x, axis=-1, keepdims=True)
        h_ref[...] = (x * lax.rsqrt(ms + EPS) * g_ref[...]).astype(BF16)

    o_ref[...] = jnp.dot(h_ref[...], w_ref[...],
                         preferred_element_type=F32).astype(o_ref.dtype)


def _norm_matmul(x, gain, w, *, out_dtype, tm, tn, col_block=0):
    m = x.shape[0]
    k, n = w.shape
    return pl.pallas_call(
        _norm_matmul_kernel,
        grid=(m // tm, n // tn),
        in_specs=[pl.BlockSpec((tm, k), lambda i, j: (i, col_block)),
                  pl.BlockSpec((1, k), lambda i, j: (0, 0)),
                  pl.BlockSpec((k, tn), lambda i, j: (0, j))],
        out_specs=pl.BlockSpec((tm, tn), lambda i, j: (i, j)),
        out_shape=jax.ShapeDtypeStruct((m, n), out_dtype),
        scratch_shapes=[pltpu.VMEM((tm, k), BF16)],
        compiler_params=_cparams("parallel", "arbitrary"),
        name="norm_matmul",
    )(x, gain.reshape(1, k).astype(F32), w)


def _matmul_residual_kernel(y_ref, w_ref, x_ref, o_ref):
    o_ref[...] = x_ref[...] + jnp.dot(y_ref[...], w_ref[...],
                                      preferred_element_type=F32)


def _matmul_residual(y, w, x, *, tm, tn):
    m, k = y.shape
    n = w.shape[1]
    return pl.pallas_call(
        _matmul_residual_kernel,
        grid=(m // tm, n // tn),
        in_specs=[pl.BlockSpec((tm, k), lambda i, j: (i, 0)),
                  pl.BlockSpec((k, tn), lambda i, j: (0, j)),
                  pl.BlockSpec((tm, tn), lambda i, j: (i, j))],
        out_specs=pl.BlockSpec((tm, tn), lambda i, j: (i, j)),
        out_shape=jax.ShapeDtypeStruct((m, n), F32),
        compiler_params=_cparams("parallel", "arbitrary"),
        name="matmul_residual",
    )(y, w, x)


def _half_rmsnorm(x, gain, lo_mask):
    x2 = x * x
    lo = jnp.sum(jnp.where(lo_mask, x2, 0.0), axis=-1, keepdims=True)
    hi = jnp.sum(jnp.where(lo_mask, 0.0, x2), axis=-1, keepdims=True)
    ms = jnp.where(lo_mask, lo, hi) * (1.0 / 64)
    return x * lax.rsqrt(ms + EPS) * gain


def _chunk_mask_add(tq, variant):
    r = lax.broadcasted_iota(jnp.int32, (tq, 2 * tq), 0)
    c = lax.broadcasted_iota(jnp.int32, (tq, 2 * tq), 1)
    key_chunk = lax.shift_right_arithmetic(c - tq * variant, 6)
    qry_chunk = lax.shift_right_arithmetic(r, 6)
    return jnp.where(key_chunk <= qry_chunk, 0.0, NEG_INF).astype(F32)


def _softmax_first(s):
    m = jnp.max(s, axis=-1, keepdims=True)
    p = jnp.exp(s - m)
    return p, m, jnp.sum(p, axis=-1, keepdims=True)


def _softmax_next(s, m, l):
    m_new = jnp.maximum(m, jnp.max(s, axis=-1, keepdims=True))
    alpha = jnp.exp(m - m_new)
    p = jnp.exp(s - m_new)
    return p, m_new, alpha * l + jnp.sum(p, axis=-1, keepdims=True), alpha


_T5_LARGE_THRESHOLDS = (12, 16, 23, 32, 46, 64, 91)


def _t5_tiles_kernel(rb_ref, o_ref, *, tq):
    h = pl.program_id(0)
    variant = pl.program_id(1)
    r = lax.broadcasted_iota(jnp.int32, (tq, 2 * tq), 0)
    c = lax.broadcasted_iota(jnp.int32, (tq, 2 * tq), 1)
    key_off = c - tq * variant
    rel = key_off - r
    n = jnp.abs(rel)
    large = jnp.full_like(n, 8)
    for t in _T5_LARGE_THRESHOLDS:
        large = large + (n >= t).astype(jnp.int32)
    bucket = jnp.where(rel > 0, T5_BUCKETS // 2, 0) + jnp.where(n < 8, n, large)
    far = rb_ref[T5_BUCKETS // 2 - 1, h]
    bias = jnp.zeros((tq, 2 * tq), F32)
    for b in range(T5_BUCKETS):
        bias = jnp.where(bucket == b, rb_ref[b, h] - far, bias)
    visible = (lax.shift_right_arithmetic(key_off, 6)
               <= lax.shift_right_arithmetic(r, 6))
    o_ref[0, 0] = jnp.where(visible, bias, NEG_INF)


def _t5_tiles(rel_bias, tq):
    nb, nh = rel_bias.shape
    return pl.pallas_call(
        functools.partial(_t5_tiles_kernel, tq=tq),
        grid=(nh, 2),
        in_specs=[pl.BlockSpec(memory_space=pltpu.SMEM)],
        out_specs=pl.BlockSpec((1, 1, tq, 2 * tq), lambda h, v: (h, v, 0, 0)),
        out_shape=jax.ShapeDtypeStruct((nh, 2, tq, 2 * tq), F32),
        compiler_params=_cparams("parallel", "arbitrary"),
        name="t5_tiles",
    )(rel_bias.astype(F32))


def _diff_attn_kernel(lamv_ref, q_ref, k_ref, v_ref, g_ref, bias_ref, qkg_ref,
                      sub_ref, o_ref, kn_ref, acc_ref, *, tq, lam_init):
    i = pl.program_id(2)
    seq = k_ref.shape[1]
    lane = lax.broadcasted_iota(jnp.int32, (1, LANES), 1)
    lo_mask = lane < A_HEAD_DIM

    @pl.when(i == 0)
    def _():
        def body(c, carry):
            rows = pl.ds(pl.multiple_of(c * 512, 512), 512)
            kn_ref[rows, :] = _half_rmsnorm(
                k_ref[0, rows, :].astype(F32), qkg_ref[1:2, :], lo_mask).astype(BF16)
            return carry
        lax.fori_loop(0, seq // 512, body, 0)

    lf = lamv_ref[...].astype(F32)
    lam = (jnp.exp(jnp.sum(lf[0:1] * lf[1:2], axis=-1, keepdims=True))
           - jnp.exp(jnp.sum(lf[2:3] * lf[3:4], axis=-1, keepdims=True)) + lam_init)

    qn = _half_rmsnorm(q_ref[0].astype(F32), qkg_ref[0:1, :], lo_mask) * (A_HEAD_DIM ** -0.5)
    q0 = jnp.where(lo_mask, qn, 0.0).astype(BF16)
    q1 = jnp.where(lo_mask, 0.0, qn).astype(BF16)

    near = pl.multiple_of(jnp.maximum(i - 1, 0) * tq, tq)
    kt = kn_ref[pl.ds(near, 2 * tq), :]
    vt = v_ref[0, pl.ds(near, 2 * tq), :]
    bias = bias_ref[0, 0]
    p0, m0, l0 = _softmax_first(_nt_dot(q0, kt) + bias)
    p1, m1, l1 = _softmax_first(_nt_dot(q1, kt) + bias)
    acc_ref[0] = jnp.dot(p0.astype(BF16), vt, preferred_element_type=F32)
    acc_ref[1] = jnp.dot(p1.astype(BF16), vt, preferred_element_type=F32)

    def far_tile(width):
        def body(j, carry):
            m0, l0, m1, l1, start = carry
            kt = kn_ref[pl.ds(pl.multiple_of(start, tq), width), :]
            vt = v_ref[0, pl.ds(pl.multiple_of(start, tq), width), :]
            p0, m0, l0, a0 = _softmax_next(_nt_dot(q0, kt), m0, l0)
            p1, m1, l1, a1 = _softmax_next(_nt_dot(q1, kt), m1, l1)
            acc_ref[0] = a0 * acc_ref[0] + jnp.dot(p0.astype(BF16), vt,
                                                   preferred_element_type=F32)
            acc_ref[1] = a1 * acc_ref[1] + jnp.dot(p1.astype(BF16), vt,
                                                   preferred_element_type=F32)
            return m0, l0, m1, l1, start + width
        return body

    n_far = jnp.maximum(i - 1, 0)
    carry = (m0, l0, m1, l1, jnp.int32(0))
    carry = lax.fori_loop(0, n_far // 2, far_tile(2 * tq), carry)
    m0, l0, m1, l1, _ = lax.fori_loop(0, n_far % 2, far_tile(tq), carry)

    o = acc_ref[0] * (1.0 / l0) - lam * (acc_ref[1] * (1.0 / l1))
    ms = jnp.mean(o * o, axis=-1, keepdims=True)
    o = o * lax.rsqrt(ms + EPS) * sub_ref[...] * (1.0 - lam_init)
    o_ref[0] = (o * _silu(g_ref[0].astype(F32))).astype(o_ref.dtype)


def _diff_attention(proj, lam_vecs, qk_g, subln_g, bias_tiles, layer_idx):
    b, s, four_w = proj.shape
    width = four_w // 4
    hblocks = width // LANES
    tq = ATTN_TQ
    lam_init = 0.8 - 0.6 * math.exp(-0.3 * layer_idx)
    qkg = jnp.concatenate([qk_g, qk_g], axis=-1).astype(F32)
    return pl.pallas_call(
        functools.partial(_diff_attn_kernel, tq=tq, lam_init=lam_init),
        grid=(b, hblocks, s // tq),
        in_specs=[
            pl.BlockSpec((4, A_HEAD_DIM), lambda bi, h, i: (0, 0)),
            pl.BlockSpec((1, tq, LANES), lambda bi, h, i: (bi, i, h)),
            pl.BlockSpec((1, s, LANES), lambda bi, h, i: (bi, 0, hblocks + h)),
            pl.BlockSpec((1, s, LANES), lambda bi, h, i: (bi, 0, 2 * hblocks + h)),
            pl.BlockSpec((1, tq, LANES), lambda bi, h, i: (bi, i, 3 * hblocks + h)),
            pl.BlockSpec((1, 1, tq, 2 * tq),
                         lambda bi, h, i: (h, jnp.minimum(i, 1), 0, 0)),
            pl.BlockSpec((2, LANES), lambda bi, h, i: (0, 0)),
            pl.BlockSpec((1, LANES), lambda bi, h, i: (0, 0)),
        ],
        out_specs=pl.BlockSpec((1, tq, LANES), lambda bi, h, i: (bi, i, h)),
        out_shape=jax.ShapeDtypeStruct((b, s, width), BF16),
        scratch_shapes=[pltpu.VMEM((s, LANES), BF16),
                        pltpu.VMEM((2, tq, LANES), F32)],
        compiler_params=_cparams("parallel", "parallel", "arbitrary"),
        name="diff_attention",
    )(lam_vecs.astype(F32), proj, proj, proj, proj, bias_tiles, qkg,
      subln_g.reshape(1, LANES).astype(F32))


def _gla_kernel(q_ref, k_ref, v_ref, g_ref, lr_ref, wgt_ref, gb_ref, og_ref,
                o_ref, state_ref, *, ts):
    @pl.when(pl.program_id(2) == 0)
    def _():
        state_ref[...] = jnp.zeros_like(state_ref)

    dk = q_ref.shape[2]
    tri_r = lax.broadcasted_iota(jnp.int32, (CHUNK, CHUNK), 0)
    tri_c = lax.broadcasted_iota(jnp.int32, (CHUNK, CHUNK), 1)
    upper = jnp.where(tri_r <= tri_c, 1.0, 0.0).astype(BF16)

    def chunk(c, carry):
        rows = pl.ds(pl.multiple_of(c * CHUNK, CHUNK), CHUNK)
        z_t = _nt_dot(wgt_ref[...], lr_ref[0, rows, :]) + gb_ref[...]
        la_t = -_softplus(-z_t) * (1.0 / B_GATE_TAU)
        la_hi = la_t.astype(BF16)
        la_lo = (la_t - la_hi.astype(F32)).astype(BF16)
        cum_t = (jnp.dot(la_hi, upper, preferred_element_type=F32)
                 + jnp.dot(la_lo, upper, preferred_element_type=F32))
        tot_t = cum_t[:, CHUNK - 1:CHUNK]
        k_t = k_ref[0, rows, :].astype(F32).T
        kd_t = (k_t * jnp.exp(tot_t - cum_t)).astype(BF16)
        state = (jnp.exp(tot_t) * state_ref[...]
                 + jnp.dot(kd_t, v_ref[0, rows, :], preferred_element_type=F32))
        state_ref[...] = state
        q = (q_ref[0, rows, :].astype(F32) * (dk ** -0.5)).astype(BF16)
        o = jnp.dot(q, state.astype(BF16), preferred_element_type=F32)
        ms = jnp.mean(o * o, axis=-1, keepdims=True)
        o = o * lax.rsqrt(ms + EPS) * og_ref[...]
        o_ref[0, rows, :] = (o * _silu(g_ref[0, rows, :].astype(F32))).astype(o_ref.dtype)
        return carry

    lax.fori_loop(0, ts // CHUNK, chunk, 0)


def _gla(proj, lr, w_gate_t, gate_bias, out_g, *, ts):
    b, s, n = proj.shape
    heads = B_HEADS
    dk = n // (6 * heads)
    dv = 2 * dk
    kq = (heads * dk) // dk
    kv = (2 * heads * dk) // dv
    return pl.pallas_call(
        functools.partial(_gla_kernel, ts=ts),
        grid=(b, heads, s // ts),
        in_specs=[
            pl.BlockSpec((1, ts, dk), lambda bi, h, i: (bi, i, h)),
            pl.BlockSpec((1, ts, dk), lambda bi, h, i: (bi, i, kq + h)),
            pl.BlockSpec((1, ts, dv), lambda bi, h, i: (bi, i, kv + h)),
            pl.BlockSpec((1, ts, dv), lambda bi, h, i: (bi, i, kv + heads + h)),
            pl.BlockSpec((1, ts, LANES), lambda bi, h, i: (bi, i, 0)),
            pl.BlockSpec((dk, LANES), lambda bi, h, i: (h, 0)),
            pl.BlockSpec((dk, 1), lambda bi, h, i: (h, 0)),
            pl.BlockSpec((1, dv), lambda bi, h, i: (0, 0)),
        ],
        out_specs=pl.BlockSpec((1, ts, dv), lambda bi, h, i: (bi, i, h)),
        out_shape=jax.ShapeDtypeStruct((b, s, heads * dv), BF16),
        scratch_shapes=[pltpu.VMEM((dk, dv), F32)],
        compiler_params=_cparams("parallel", "parallel", "arbitrary"),
        name="gla",
    )(proj, proj, proj, proj, lr, w_gate_t, gate_bias.reshape(-1, 1).astype(F32),
      out_g.reshape(1, dv).astype(F32))


SUBLANES = 8


def _rglru_kernel(u_ref, g_ref, cw_ref, cb_ref, wr_ref, br_ref, wi_ref, bi_ref,
                  lam_ref, o_ref, ubuf_ref, xc_ref, a_ref, b_ref, h_ref, *, ts):
    width = u_ref.shape[2]
    bd = width // C_BLOCKS

    @pl.when(pl.program_id(1) == 0)
    def _():
        ubuf_ref[0:SUBLANES, :] = jnp.zeros((SUBLANES, width), F32)
        h_ref[...] = jnp.zeros_like(h_ref)

    ubuf_ref[SUBLANES:SUBLANES + ts, :] = u_ref[0]
    xc = cb_ref[...]
    for t in range(C_CONV):
        off = SUBLANES - (C_CONV - 1) + t
        xc = xc + ubuf_ref[off:off + ts, :] * cw_ref[t:t + 1, :]
    xc_ref[...] = xc
    ubuf_ref[0:SUBLANES, :] = ubuf_ref[ts:ts + SUBLANES, :]

    sp = _softplus(-lam_ref[...])
    for n in range(C_BLOCKS):
        cols = slice(n * bd, (n + 1) * bd)
        xb = xc_ref[:, cols]
        xb16 = xb.astype(BF16)
        r = jax.nn.sigmoid(jnp.dot(xb16, wr_ref[n], preferred_element_type=F32)
                           + br_ref[:, cols])
        gate_i = jax.nn.sigmoid(jnp.dot(xb16, wi_ref[n], preferred_element_type=F32)
                                + bi_ref[:, cols])
        log_a = -C_C * r * sp[:, cols]
        a_ref[:, cols] = jnp.exp(log_a)
        b_ref[:, cols] = jnp.sqrt(-_expm1(2.0 * log_a)) * (gate_i * xb)

    row = lax.broadcasted_iota(jnp.int32, (SUBLANES, width), 0)

    def tile(t, h_prev):
        rows = pl.ds(pl.multiple_of(t * SUBLANES, SUBLANES), SUBLANES)
        a = a_ref[rows, :]
        b = b_ref[rows, :]
        d = 1
        while d < SUBLANES:
            a_up = pltpu.roll(a, d, 0)
            b_up = pltpu.roll(b, d, 0)
            keep = row >= d
            b = jnp.where(keep, a * b_up + b, b)
            a = jnp.where(keep, a * a_up, a)
            d *= 2
        h = b + a * h_prev
        b_ref[rows, :] = h
        return h[SUBLANES - 1:SUBLANES, :]

    h_ref[...] = lax.fori_loop(0, ts // SUBLANES, tile, h_ref[...])
    o_ref[0] = (b_ref[...] * _silu(g_ref[0].astype(F32))).astype(o_ref.dtype)


def _rglru(u, gate, gate_block, conv_w, conv_b, w_rg, b_rg, w_ig, b_ig, lam, *, ts):
    b, s, width = u.shape
    bd = width // C_BLOCKS
    row = lambda a: a.reshape(1, width).astype(F32)
    const2 = lambda bi, i: (0, 0)
    return pl.pallas_call(
        functools.partial(_rglru_kernel, ts=ts),
        grid=(b, s // ts),
        in_specs=[
            pl.BlockSpec((1, ts, width), lambda bi, i: (bi, i, 0)),
            pl.BlockSpec((1, ts, width), lambda bi, i: (bi, i, gate_block)),
            pl.BlockSpec((C_CONV, width), const2),
            pl.BlockSpec((1, width), const2),
            pl.BlockSpec((C_BLOCKS, bd, bd), lambda bi, i: (0, 0, 0)),
            pl.BlockSpec((1, width), const2),
            pl.BlockSpec((C_BLOCKS, bd, bd), lambda bi, i: (0, 0, 0)),
            pl.BlockSpec((1, width), const2),
            pl.BlockSpec((1, width), const2),
        ],
        out_specs=pl.BlockSpec((1, ts, width), lambda bi, i: (bi, i, 0)),
        out_shape=jax.ShapeDtypeStruct((b, s, width), BF16),
        scratch_shapes=[pltpu.VMEM((ts + SUBLANES, width), F32),
                        pltpu.VMEM((ts, width), F32),
                        pltpu.VMEM((ts, width), F32),
                        pltpu.VMEM((ts, width), F32),
                        pltpu.VMEM((1, width), F32)],
        compiler_params=_cparams("parallel", "arbitrary"),
        name="rglru",
    )(u, gate, conv_w.astype(F32), row(conv_b), w_rg.astype(BF16), row(b_rg),
      w_ig.astype(BF16), row(b_ig), row(lam))


def _swap_rope_halves(x):
    lane = lax.broadcasted_iota(jnp.int32, (1, LANES), 1)
    first = (lane % D_ROPE) < (D_ROPE // 2)
    return jnp.where(first, pltpu.roll(x, LANES - D_ROPE // 2, 1),
                     pltpu.roll(x, D_ROPE // 2, 1))


def _rope(x, cos, sin_signed):
    return x * cos + _swap_rope_halves(x) * sin_signed


def _mla_attn_kernel(qn_ref, qp_ref, kn_ref, kp_ref, v_ref, g_ref, cos_ref,
                     sin_ref, gains_ref, o_ref, kcat_ref, acc_ref, *, tq):
    i = pl.program_id(2)
    seq = kn_ref.shape[1]
    lane = lax.broadcasted_iota(jnp.int32, (1, LANES), 1)
    lo_mask = lane < D_ROPE
    g_q_nope, g_q_pe = gains_ref[0:1, :], gains_ref[1:2, :]
    g_k_nope, g_k_pe = gains_ref[2:3, :], gains_ref[3:4, :]

    def rmsnorm(x, gain):
        ms = jnp.mean(x * x, axis=-1, keepdims=True)
        return x * lax.rsqrt(ms + EPS) * gain

    @pl.when(i == 0)
    def _():
        def body(c, carry):
            rows = pl.ds(pl.multiple_of(c * 512, 512), 512)
            kp = _rope(_half_rmsnorm(kp_ref[0, rows, :].astype(F32), g_k_pe, lo_mask),
                       cos_ref[rows, :], sin_ref[rows, :])
            kn = kn_ref[0, rows, :].astype(F32)
            kcat_ref[0, rows, 0:LANES] = rmsnorm(kn[:, 0:LANES], g_k_nope).astype(BF16)
            kcat_ref[0, rows, LANES:] = jnp.where(lo_mask, kp, 0.0).astype(BF16)
            kcat_ref[1, rows, 0:LANES] = rmsnorm(kn[:, LANES:], g_k_nope).astype(BF16)
            kcat_ref[1, rows, LANES:] = jnp.where(lo_mask, 0.0, kp).astype(BF16)
            return carry
        lax.fori_loop(0, seq // 512, body, 0)

    scale = (D_NOPE + D_ROPE) ** -0.5
    qrows = pl.ds(pl.multiple_of(i * tq, tq), tq)
    qp = _rope(_half_rmsnorm(qp_ref[0].astype(F32), g_q_pe, lo_mask),
               cos_ref[qrows, :], sin_ref[qrows, :]) * scale
    qn = qn_ref[0].astype(F32)
    qcat = (
        jnp.concatenate([(rmsnorm(qn[:, 0:LANES], g_q_nope) * scale).astype(BF16),
                         jnp.where(lo_mask, qp, 0.0).astype(BF16)], axis=-1),
        jnp.concatenate([(rmsnorm(qn[:, LANES:], g_q_nope) * scale).astype(BF16),
                         jnp.where(lo_mask, 0.0, qp).astype(BF16)], axis=-1),
    )

    first = jnp.minimum(i, 1)
    near = pl.multiple_of(jnp.maximum(i - 1, 0) * tq, tq)
    r = lax.broadcasted_iota(jnp.int32, (tq, 2 * tq), 0)
    c = lax.broadcasted_iota(jnp.int32, (tq, 2 * tq), 1)
    visible = (lax.shift_right_arithmetic(c - tq * first, 6)
               <= lax.shift_right_arithmetic(r, 6))
    mask_add = jnp.where(visible, 0.0, NEG_INF).astype(F32)

    stats = []
    for hh in range(2):
        kt = kcat_ref[hh, pl.ds(near, 2 * tq), :]
        vt = v_ref[0, pl.ds(near, 2 * tq), hh * D_V:(hh + 1) * D_V]
        p, m, l = _softmax_first(_nt_dot(qcat[hh], kt) + mask_add)
        acc_ref[hh] = jnp.dot(p.astype(BF16), vt, preferred_element_type=F32)
        stats += [m, l]

    def far_tile(width):
        def body(j, carry):
            ma, la, mb, lb, start = carry
            ks = pl.ds(pl.multiple_of(start, tq), width)
            out = []
            for hh, (m, l) in enumerate(((ma, la), (mb, lb))):
                kt = kcat_ref[hh, ks, :]
                vt = v_ref[0, ks, hh * D_V:(hh + 1) * D_V]
                p, m, l, alpha = _softmax_next(_nt_dot(qcat[hh], kt), m, l)
                acc_ref[hh] = alpha * acc_ref[hh] + jnp.dot(
                    p.astype(BF16), vt, preferred_element_type=F32)
                out += [m, l]
            return (*out, start + width)
        return body

    n_far = jnp.maximum(i - 1, 0)
    carry = (*stats, jnp.int32(0))
    carry = lax.fori_loop(0, n_far // 2, far_tile(2 * tq), carry)
    ma, la, mb, lb, _ = lax.fori_loop(0, n_far % 2, far_tile(tq), carry)

    g = g_ref[0].astype(F32)
    o_ref[0, :, 0:D_V] = (acc_ref[0] * (1.0 / la) * _silu(g[:, 0:D_V])).astype(o_ref.dtype)
    o_ref[0, :, D_V:] = (acc_ref[1] * (1.0 / lb) * _silu(g[:, D_V:])).astype(o_ref.dtype)


def _mla_attention(q, kv, kpe, proj, gate_block0, cos, sin_signed, gains):
    b, s, _ = q.shape
    heads = D_HEADS
    pairs = heads // 2
    tq = ATTN_TQ
    pw = 2 * LANES
    return pl.pallas_call(
        functools.partial(_mla_attn_kernel, tq=tq),
        grid=(b, pairs, s // tq),
        in_specs=[
            pl.BlockSpec((1, tq, pw), lambda bi, p, i: (bi, i, p)),
            pl.BlockSpec((1, tq, LANES), lambda bi, p, i: (bi, i, 2 * pairs + p)),
            pl.BlockSpec((1, s, pw), lambda bi, p, i: (bi, 0, p)),
            pl.BlockSpec((1, s, LANES), lambda bi, p, i: (bi, 0, 0)),
            pl.BlockSpec((1, s, pw), lambda bi, p, i: (bi, 0, pairs + p)),
            pl.BlockSpec((1, tq, pw), lambda bi, p, i: (bi, i, gate_block0 + p)),
            pl.BlockSpec((s, LANES), lambda bi, p, i: (0, 0)),
            pl.BlockSpec((s, LANES), lambda bi, p, i: (0, 0)),
            pl.BlockSpec((4, LANES), lambda bi, p, i: (0, 0)),
        ],
        out_specs=pl.BlockSpec((1, tq, pw), lambda bi, p, i: (bi, i, p)),
        out_shape=jax.ShapeDtypeStruct((b, s, heads * D_V), BF16),
        scratch_shapes=[pltpu.VMEM((2, s, pw), BF16),
                        pltpu.VMEM((2, tq, D_V), F32)],
        compiler_params=_cparams("parallel", "parallel", "arbitrary"),
        name="mla_attention",
    )(q, q, kv, kpe, kv, proj, cos, sin_signed, gains)


MM_TM = 1024
MM_TN = 1024


def _layer_diff(x2, b, s, norm_g, w_in, qk_g, lam_vecs, subln_g, w_out, bias_tiles,
                layer_idx):
    proj = _norm_matmul(x2, norm_g, w_in.astype(BF16), out_dtype=BF16,
                        tm=MM_TM, tn=MM_TN)
    y = _diff_attention(proj.reshape(b, s, -1), lam_vecs, qk_g, subln_g,
                        bias_tiles, layer_idx)
    return _matmul_residual(y.reshape(b * s, -1), w_out.astype(BF16), x2,
                            tm=MM_TM, tn=MM_TN)


def _layer_gla(x2, b, s, norm_g, w_in, w_gate, gate_bias, out_g, w_out):
    rank, hdk = w_gate.shape
    n_main = w_in.shape[1] - rank
    w_main = w_in[:, :n_main].astype(BF16)
    w_lr = jnp.pad(w_in[:, n_main:], ((0, 0), (0, LANES - rank))).astype(BF16)
    w_gate_t = jnp.pad(w_gate, ((0, LANES - rank), (0, 0))).T.astype(BF16)
    proj = _norm_matmul(x2, norm_g, w_main, out_dtype=BF16, tm=MM_TM, tn=MM_TN)
    lr = _norm_matmul(x2, norm_g, w_lr, out_dtype=BF16, tm=MM_TM, tn=LANES)
    y = _gla(proj.reshape(b, s, -1), lr.reshape(b, s, LANES), w_gate_t, gate_bias,
             out_g, ts=512)
    return _matmul_residual(y.reshape(b * s, -1), w_out.astype(BF16), x2,
                            tm=MM_TM, tn=MM_TN)


def _layer_rglru(x2, b, s, norm_g, w_in, conv_w, conv_b, w_rg, b_rg, w_ig, b_ig,
                 lam, w_out):
    width = w_in.shape[1] // 2
    u = _norm_matmul(x2, norm_g, w_in[:, :width].astype(BF16), out_dtype=F32,
                     tm=MM_TM, tn=MM_TN)
    gate = _norm_matmul(x2, norm_g, w_in[:, width:].astype(BF16), out_dtype=BF16,
                        tm=MM_TM, tn=MM_TN)
    y = _rglru(u.reshape(b, s, width), gate.reshape(b, s, width), 0, conv_w, conv_b,
               w_rg, b_rg, w_ig, b_ig, lam, ts=256)
    return _matmul_residual(y.reshape(b * s, -1), w_out.astype(BF16), x2,
                            tm=MM_TM, tn=MM_TN)


def _rope_tables(s):
    half = D_ROPE // 2
    inv = ROPE_THETA ** (-jnp.arange(half, dtype=F32) / half)
    ang = jnp.arange(s, dtype=F32)[:, None] * inv[None, :]
    cos, sin = jnp.cos(ang), jnp.sin(ang)
    return (jnp.concatenate([cos, cos, cos, cos], axis=-1),
            jnp.concatenate([-sin, sin, -sin, sin], axis=-1))


def _layer_mla(x2, b, s, norm_g, w_in, q_lat_g, kv_lat_g, w_uq, w_ukv, qk_g, w_out):
    q_rank, kv_rank = q_lat_g.shape[0], kv_lat_g.shape[0]
    heads = D_HEADS
    lat = q_rank + kv_rank
    w_main = jnp.concatenate([w_in[:, :lat], w_in[:, lat + D_ROPE:]], axis=1).astype(BF16)
    w_kpe = w_in[:, lat:lat + D_ROPE]
    w_kpe = jnp.concatenate([w_kpe, w_kpe], axis=1).astype(BF16)
    uq = w_uq.reshape(q_rank, heads, D_NOPE + D_ROPE)
    w_uq_p = jnp.concatenate([uq[:, :, :D_NOPE].reshape(q_rank, -1),
                              uq[:, :, D_NOPE:].reshape(q_rank, -1)], axis=1).astype(BF16)
    ukv = w_ukv.reshape(kv_rank, heads, D_NOPE + D_V)
    w_ukv_p = jnp.concatenate([ukv[:, :, :D_NOPE].reshape(kv_rank, -1),
                               ukv[:, :, D_NOPE:].reshape(kv_rank, -1)], axis=1).astype(BF16)
    dup = lambda v: jnp.concatenate([v, v])
    gains = jnp.stack([qk_g[0, :D_NOPE], dup(qk_g[0, D_NOPE:]),
                       qk_g[1, :D_NOPE], dup(qk_g[1, D_NOPE:])]).astype(F32)
    cos, sin_signed = _rope_tables(s)

    proj = _norm_matmul(x2, norm_g, w_main, out_dtype=BF16, tm=MM_TM, tn=MM_TN)
    kpe = _norm_matmul(x2, norm_g, w_kpe, out_dtype=BF16, tm=MM_TM, tn=LANES)
    q = _norm_matmul(proj, q_lat_g, w_uq_p, out_dtype=BF16, tm=MM_TM, tn=MM_TN,
                     col_block=0)
    kv = _norm_matmul(proj, kv_lat_g, w_ukv_p, out_dtype=BF16, tm=MM_TM, tn=MM_TN,
                      col_block=1)
    gate_block0 = lat // (2 * LANES)
    y = _mla_attention(q.reshape(b, s, -1), kv.reshape(b, s, -1),
                       kpe.reshape(b, s, LANES), proj.reshape(b, s, -1),
                       gate_block0, cos, sin_signed, gains)
    return _matmul_residual(y.reshape(b * s, -1), w_out.astype(BF16), x2,
                            tm=MM_TM, tn=MM_TN)


def kernel(x, norm_g, rel_bias, a_w_in, a_qk_g, a_lambda, a_subln_g, a_w_out, b_w_in, b_w_gate, b_gate_bias, b_out_g, b_w_out, c_w_in, c_conv_w, c_conv_b, c_w_rgate, c_b_rgate, c_w_igate, c_b_igate, c_lambda, c_w_out, d_w_in, d_q_lat_g, d_kv_lat_g, d_w_uq, d_w_ukv, d_qk_g, d_w_out):
    b, s, d = x.shape
    depth = norm_g.shape[0]
    x2 = x.reshape(b * s, d)
    bias_tiles = _t5_tiles(rel_bias, ATTN_TQ)
    for i in range(depth):
        m, j = i % 4, i // 4
        if m == 0:
            x2 = _layer_diff(x2, b, s, norm_g[i], a_w_in[j], a_qk_g[j], a_lambda[j],
                             a_subln_g[j], a_w_out[j], bias_tiles, i)
        elif m == 1:
            x2 = _layer_gla(x2, b, s, norm_g[i], b_w_in[j], b_w_gate[j],
                            b_gate_bias[j], b_out_g[j], b_w_out[j])
        elif m == 2:
            x2 = _layer_rglru(x2, b, s, norm_g[i], c_w_in[j], c_conv_w[j], c_conv_b[j],
                              c_w_rgate[j], c_b_rgate[j], c_w_igate[j], c_b_igate[j],
                              c_lambda[j], c_w_out[j])
        else:
            x2 = _layer_mla(x2, b, s, norm_g[i], d_w_in[j], d_q_lat_g[j],
                            d_kv_lat_g[j], d_w_uq[j], d_w_ukv[j], d_qk_g[j], d_w_out[j])
    return x2.reshape(b, s, d)
```

```python
import functools
import math

import jax
import jax.numpy as jnp
from jax import lax
from jax.experimental import pallas as pl
from jax.experimental.pallas import tpu as pltpu

F32 = jnp.float32
BF16 = jnp.bfloat16

EPS = 1e-6
NEG_INF = -1e30
LOG2E = math.log2(math.e)
CHUNK = 64
LANES = 128
SUBLANES = 8
BF16_ROWS = 16
V7X_VMEM_BYTES = 64 * 1024 * 1024
VMEM_LIMIT = V7X_VMEM_BYTES - 8 * 1024 * 1024

A_HEADS = 16
A_HEAD_DIM = 64
T5_BUCKETS = 32
B_HEADS = 4
B_GATE_TAU = 16.0
C_BLOCKS = 8
C_CONV = 4
C_C = 8.0
D_HEADS = 16
D_NOPE = 128
D_ROPE = 64
D_V = 128
ROPE_THETA = 10000.0

ATTN_TQ = 512
V_ROWS = LANES + BF16_ROWS


def _cparams(*semantics):
    return pltpu.CompilerParams(dimension_semantics=semantics,
                                vmem_limit_bytes=VMEM_LIMIT)


def _silu(g):
    return g * jax.nn.sigmoid(g)


def _softplus(x):
    return jnp.maximum(x, 0.0) + jnp.log(1.0 + jnp.exp(-jnp.abs(x)))


_EXPM1_SERIES_BOUND = 0.35
_EXPM1_SERIES_TERMS = 9


def _expm1(x):
    poly = jnp.full_like(x, 1.0 / math.factorial(_EXPM1_SERIES_TERMS))
    for n in range(_EXPM1_SERIES_TERMS - 1, 0, -1):
        poly = poly * x + 1.0 / math.factorial(n)
    return jnp.where(jnp.abs(x) < _EXPM1_SERIES_BOUND, poly * x, jnp.exp(x) - 1.0)


def _nt_dot(a, b):
    return lax.dot_general(a, b, (((1,), (1,)), ((), ())),
                           preferred_element_type=F32)


def _norm_matmul_kernel(x_ref, g_ref, w_ref, o_ref, h_ref):
    @pl.when(pl.program_id(1) == 0)
    def _():
        x = x_ref[...].astype(F32)
        ms = jnp.mean(x * x, axis=-1, keepdims=True)
        h_ref[...] = (x * lax.rsqrt(ms + EPS) * g_ref[...]).astype(BF16)

    o_ref[...] = jnp.dot(h_ref[...], w_ref[...],
                         preferred_element_type=F32).astype(o_ref.dtype)


def _norm_matmul(x, gain, w, *, out_dtype, tm, tn, col_block=0):
    m = x.shape[0]
    k, n = w.shape
    return pl.pallas_call(
        _norm_matmul_kernel,
        grid=(m // tm, n // tn),
        in_specs=[pl.BlockSpec((tm, k), lambda i, j: (i, col_block)),
                  pl.BlockSpec((1, k), lambda i, j: (0, 0)),
                  pl.BlockSpec((k, tn), lambda i, j: (0, j))],
        out_specs=pl.BlockSpec((tm, tn), lambda i, j: (i, j)),
        out_shape=jax.ShapeDtypeStruct((m, n), out_dtype),
        scratch_shapes=[pltpu.VMEM((tm, k), BF16)],
        compiler_params=_cparams("parallel", "arbitrary"),
        name="norm_matmul",
    )(x, gain.reshape(1, k).astype(F32), w)


def _matmul_residual_kernel(y_ref, w_ref, x_ref, o_ref):
    o_ref[...] = x_ref[...] + jnp.dot(y_ref[...], w_ref[...],
                                      preferred_element_type=F32)


def _matmul_residual(y, w, x, *, tm, tn):
    m, k = y.shape
    n = w.shape[1]
    return pl.pallas_call(
        _matmul_residual_kernel,
        grid=(m // tm, n // tn),
        in_specs=[pl.BlockSpec((tm, k), lambda i, j: (i, 0)),
                  pl.BlockSpec((k, tn), lambda i, j: (0, j)),
                  pl.BlockSpec((tm, tn), lambda i, j: (i, j))],
        out_specs=pl.BlockSpec((tm, tn), lambda i, j: (i, j)),
        out_shape=jax.ShapeDtypeStruct((m, n), F32),
        compiler_params=_cparams("parallel", "arbitrary"),
        name="matmul_residual",
    )(y, w, x)


def _half_rmsnorm(x, gain, lo_mask):
    x2 = x * x
    lo = jnp.sum(jnp.where(lo_mask, x2, 0.0), axis=-1, keepdims=True)
    hi = jnp.sum(jnp.where(lo_mask, 0.0, x2), axis=-1, keepdims=True)
    ms = jnp.where(lo_mask, lo, hi) * (1.0 / 64)
    return x * lax.rsqrt(ms + EPS) * gain


def _store_vt_block(vt_ref, lead, v_rows):
    tq = v_rows.shape[0]
    vt_ref[(*lead, slice(0, LANES), slice(None))] = v_rows.astype(F32).T.astype(BF16)
    vt_ref[(*lead, slice(LANES, V_ROWS), slice(None))] = jnp.ones((BF16_ROWS, tq), BF16)


def _flash_init(s, vt):
    m = jnp.max(s, axis=0, keepdims=True)
    p = jnp.exp2(s - m).astype(BF16)
    return m, jnp.dot(vt, p, preferred_element_type=F32)


def _flash_update(s, vt, m, acc):
    m_new = jnp.maximum(m, jnp.max(s, axis=0, keepdims=True))
    alpha = jnp.exp2(m - m_new)
    p = jnp.exp2(s - m_new).astype(BF16)
    return m_new, alpha * acc + jnp.dot(vt, p, preferred_element_type=F32)


def _chunk_causal_flash(i, streams, acc_ref, s_ref):
    first = jnp.maximum(i - 1, 0)
    n_far = first

    def issue(blk, slot, add=None):
        for si, (scores, _, near_add) in enumerate(streams):
            s = scores(blk)
            s_ref[si, slot] = s if add is None else s + near_add(add)

    def update(blk, slot, ms):
        out = []
        for si, (_, values, _) in enumerate(streams):
            m, acc_ref[si] = _flash_update(s_ref[si, slot], values(blk), ms[si],
                                           acc_ref[si])
            out.append(m)
        return tuple(out)

    def pair(t2, ms):
        t = 2 * t2
        issue(t + 1, 1)
        ms = update(t, 0, ms)
        issue(t + 2, 0)
        return update(t + 1, 1, ms)

    def last_two(_, ms):
        issue(n_far - 1, 1)
        ms = update(n_far - 2, 0, ms)
        return update(n_far - 1, 1, ms)

    def last_one(_, ms):
        return update(n_far - 1, 0, ms)

    issue(first, 0, add=0)
    issue(first + 1, 1, add=1)
    ms = []
    for si, (_, values, _) in enumerate(streams):
        m, acc_ref[si] = _flash_init(s_ref[si, 0], values(first))
        ms.append(m)
    issue(0, 0)
    ms = update(first + 1, 1, tuple(ms))

    pairs = jnp.maximum(n_far - 1, 0) // 2
    rest = n_far - 2 * pairs
    ms = lax.fori_loop(0, pairs, pair, ms)
    ms = lax.fori_loop(0, (rest == 2).astype(jnp.int32), last_two, ms)
    lax.fori_loop(0, (rest == 1).astype(jnp.int32), last_one, ms)


def _normalised(acc):
    return acc[0:LANES] * (1.0 / acc[LANES:LANES + 1])


_T5_LARGE_THRESHOLDS = (12, 16, 23, 32, 46, 64, 91)


def _t5_tiles_kernel(rb_ref, o_ref, *, tq):
    h = pl.program_id(0)
    variant = pl.program_id(1)
    c = lax.broadcasted_iota(jnp.int32, (2 * tq, tq), 0)
    r = lax.broadcasted_iota(jnp.int32, (2 * tq, tq), 1)
    key_off = c - tq * variant
    rel = key_off - r
    n = jnp.abs(rel)
    large = jnp.full_like(n, 8)
    for t in _T5_LARGE_THRESHOLDS:
        large = large + (n >= t).astype(jnp.int32)
    bucket = jnp.where(rel > 0, T5_BUCKETS // 2, 0) + jnp.where(n < 8, n, large)
    far = rb_ref[T5_BUCKETS // 2 - 1, h]
    bias = jnp.zeros((2 * tq, tq), F32)
    for b in range(T5_BUCKETS):
        bias = jnp.where(bucket == b, (rb_ref[b, h] - far) * LOG2E, bias)
    visible = (lax.shift_right_arithmetic(key_off, 6)
               <= lax.shift_right_arithmetic(r, 6))
    o_ref[0, 0] = jnp.where(visible, bias, NEG_INF)


def _t5_tiles(rel_bias, tq):
    nb, nh = rel_bias.shape
    return pl.pallas_call(
        functools.partial(_t5_tiles_kernel, tq=tq),
        grid=(nh, 2),
        in_specs=[pl.BlockSpec(memory_space=pltpu.SMEM)],
        out_specs=pl.BlockSpec((1, 1, 2 * tq, tq), lambda h, v: (h, v, 0, 0)),
        out_shape=jax.ShapeDtypeStruct((nh, 2, 2 * tq, tq), F32),
        compiler_params=_cparams("parallel", "arbitrary"),
        name="t5_tiles",
    )(rel_bias.astype(F32))


def _diff_attn_kernel(lamv_ref, q_ref, k_ref, v_ref, g_ref, bias_ref, qkg_ref,
                      sub_ref, o_ref, kn_ref, vt_ref, acc_ref, s_ref, *, tq, lam_init):
    i = pl.program_id(2)
    seq = k_ref.shape[1]
    lane = lax.broadcasted_iota(jnp.int32, (1, LANES), 1)
    lo_mask = lane < A_HEAD_DIM

    @pl.when(i == 0)
    def _():
        def body(c, carry):
            rows = pl.ds(pl.multiple_of(c * tq, tq), tq)
            kn_ref[rows, :] = _half_rmsnorm(
                k_ref[0, rows, :].astype(F32), qkg_ref[1:2, :], lo_mask).astype(BF16)
            _store_vt_block(vt_ref, (c,), v_ref[0, rows, :])
            return carry
        lax.fori_loop(0, seq // tq, body, 0)

    lf = lamv_ref[...].astype(F32)
    lam = (jnp.exp(jnp.sum(lf[0:1] * lf[1:2], axis=-1, keepdims=True))
           - jnp.exp(jnp.sum(lf[2:3] * lf[3:4], axis=-1, keepdims=True)) + lam_init)

    qn = (_half_rmsnorm(q_ref[0].astype(F32), qkg_ref[0:1, :], lo_mask)
          * (A_HEAD_DIM ** -0.5 * LOG2E))
    q01 = jnp.concatenate([jnp.where(lo_mask, qn, 0.0).astype(BF16),
                           jnp.where(lo_mask, 0.0, qn).astype(BF16)], axis=0)

    def scores(blk):
        return _nt_dot(kn_ref[pl.ds(pl.multiple_of(blk * tq, tq), tq), :], q01)

    def near_add(j):
        bias_t = bias_ref[0, 0, j * tq:(j + 1) * tq, :]
        return jnp.concatenate([bias_t, bias_t], axis=1)

    _chunk_causal_flash(i, [(scores, lambda blk: vt_ref[blk], near_add)], acc_ref,
                        s_ref)

    o01 = _normalised(acc_ref[0])
    o_t = o01[:, 0:tq] - lam * o01[:, tq:]
    ms_t = jnp.mean(o_t * o_t, axis=0, keepdims=True)
    o_t = o_t * lax.rsqrt(ms_t + EPS) * sub_ref[...] * (1.0 - lam_init)
    o_ref[0] = (o_t.T * _silu(g_ref[0].astype(F32))).astype(o_ref.dtype)


def _diff_attention(proj, lam_vecs, qk_g, subln_g, bias_tiles, layer_idx):
    b, s, four_w = proj.shape
    width = four_w // 4
    hblocks = width // LANES
    tq = ATTN_TQ
    lam_init = 0.8 - 0.6 * math.exp(-0.3 * layer_idx)
    qkg = jnp.concatenate([qk_g, qk_g], axis=-1).astype(F32)
    return pl.pallas_call(
        functools.partial(_diff_attn_kernel, tq=tq, lam_init=lam_init),
        grid=(b, hblocks, s // tq),
        in_specs=[
            pl.BlockSpec((4, A_HEAD_DIM), lambda bi, h, i: (0, 0)),
            pl.BlockSpec((1, tq, LANES), lambda bi, h, i: (bi, i, h)),
            pl.BlockSpec((1, s, LANES), lambda bi, h, i: (bi, 0, hblocks + h)),
            pl.BlockSpec((1, s, LANES), lambda bi, h, i: (bi, 0, 2 * hblocks + h)),
            pl.BlockSpec((1, tq, LANES), lambda bi, h, i: (bi, i, 3 * hblocks + h)),
            pl.BlockSpec((1, 1, 2 * tq, tq),
                         lambda bi, h, i: (h, jnp.minimum(i, 1), 0, 0)),
            pl.BlockSpec((2, LANES), lambda bi, h, i: (0, 0)),
            pl.BlockSpec((LANES, 1), lambda bi, h, i: (0, 0)),
        ],
        out_specs=pl.BlockSpec((1, tq, LANES), lambda bi, h, i: (bi, i, h)),
        out_shape=jax.ShapeDtypeStruct((b, s, width), BF16),
        scratch_shapes=[pltpu.VMEM((s, LANES), BF16),
                        pltpu.VMEM((s // tq, V_ROWS, tq), BF16),
                        pltpu.VMEM((1, V_ROWS, 2 * tq), F32),
                        pltpu.VMEM((1, 2, tq, 2 * tq), F32)],
        compiler_params=_cparams("parallel", "parallel", "arbitrary"),
        name="diff_attention",
    )(lam_vecs.astype(F32), proj, proj, proj, proj, bias_tiles, qkg,
      subln_g.reshape(LANES, 1).astype(F32))


def _gla_kernel(q_ref, k_ref, v_ref, g_ref, lr_ref, wgt_ref, gb_ref, og_ref,
                o_ref, state_ref, *, ts):
    @pl.when(pl.program_id(2) == 0)
    def _():
        state_ref[...] = jnp.zeros_like(state_ref)

    dk = q_ref.shape[2]
    tri_r = lax.broadcasted_iota(jnp.int32, (CHUNK, CHUNK), 0)
    tri_c = lax.broadcasted_iota(jnp.int32, (CHUNK, CHUNK), 1)
    upper = jnp.where(tri_r <= tri_c, 1.0, 0.0).astype(BF16)

    def chunk(c, carry):
        rows = pl.ds(pl.multiple_of(c * CHUNK, CHUNK), CHUNK)
        z_t = _nt_dot(wgt_ref[...], lr_ref[0, rows, :]) + gb_ref[...]
        la_t = -_softplus(-z_t) * (1.0 / B_GATE_TAU)
        la_hi = la_t.astype(BF16)
        la_lo = (la_t - la_hi.astype(F32)).astype(BF16)
        cum_t = (jnp.dot(la_hi, upper, preferred_element_type=F32)
                 + jnp.dot(la_lo, upper, preferred_element_type=F32))
        tot_t = cum_t[:, CHUNK - 1:CHUNK]
        k_t = k_ref[0, rows, :].astype(F32).T
        kd_t = (k_t * jnp.exp(tot_t - cum_t)).astype(BF16)
        state = (jnp.exp(tot_t) * state_ref[...]
                 + jnp.dot(kd_t, v_ref[0, rows, :], preferred_element_type=F32))
        state_ref[...] = state
        q = (q_ref[0, rows, :].astype(F32) * (dk ** -0.5)).astype(BF16)
        o = jnp.dot(q, state.astype(BF16), preferred_element_type=F32)
        ms = jnp.mean(o * o, axis=-1, keepdims=True)
        o = o * lax.rsqrt(ms + EPS) * og_ref[...]
        o_ref[0, rows, :] = (o * _silu(g_ref[0, rows, :].astype(F32))).astype(o_ref.dtype)
        return carry

    lax.fori_loop(0, ts // CHUNK, chunk, 0)


def _gla(proj, lr, w_gate_t, gate_bias, out_g, *, ts):
    b, s, n = proj.shape
    heads = B_HEADS
    dk = n // (6 * heads)
    dv = 2 * dk
    kq = (heads * dk) // dk
    kv = (2 * heads * dk) // dv
    return pl.pallas_call(
        functools.partial(_gla_kernel, ts=ts),
        grid=(b, heads, s // ts),
        in_specs=[
            pl.BlockSpec((1, ts, dk), lambda bi, h, i: (bi, i, h)),
            pl.BlockSpec((1, ts, dk), lambda bi, h, i: (bi, i, kq + h)),
            pl.BlockSpec((1, ts, dv), lambda bi, h, i: (bi, i, kv + h)),
            pl.BlockSpec((1, ts, dv), lambda bi, h, i: (bi, i, kv + heads + h)),
            pl.BlockSpec((1, ts, LANES), lambda bi, h, i: (bi, i, 0)),
            pl.BlockSpec((dk, LANES), lambda bi, h, i: (h, 0)),
            pl.BlockSpec((dk, 1), lambda bi, h, i: (h, 0)),
            pl.BlockSpec((1, dv), lambda bi, h, i: (0, 0)),
        ],
        out_specs=pl.BlockSpec((1, ts, dv), lambda bi, h, i: (bi, i, h)),
        out_shape=jax.ShapeDtypeStruct((b, s, heads * dv), BF16),
        scratch_shapes=[pltpu.VMEM((dk, dv), F32)],
        compiler_params=_cparams("parallel", "parallel", "arbitrary"),
        name="gla",
    )(proj, proj, proj, proj, lr, w_gate_t, gate_bias.reshape(-1, 1).astype(F32),
      out_g.reshape(1, dv).astype(F32))


def _rglru_kernel(u_ref, g_ref, cw_ref, cb_ref, wr_ref, br_ref, wi_ref, bi_ref,
                  lam_ref, o_ref, ubuf_ref, xc_ref, a_ref, b_ref, h_ref, *, ts):
    width = u_ref.shape[2]
    bd = width // C_BLOCKS

    @pl.when(pl.program_id(1) == 0)
    def _():
        ubuf_ref[0:SUBLANES, :] = jnp.zeros((SUBLANES, width), F32)
        h_ref[...] = jnp.zeros_like(h_ref)

    ubuf_ref[SUBLANES:SUBLANES + ts, :] = u_ref[0]
    xc = cb_ref[...]
    for t in range(C_CONV):
        off = SUBLANES - (C_CONV - 1) + t
        xc = xc + ubuf_ref[off:off + ts, :] * cw_ref[t:t + 1, :]
    xc_ref[...] = xc
    ubuf_ref[0:SUBLANES, :] = ubuf_ref[ts:ts + SUBLANES, :]

    sp = _softplus(-lam_ref[...])
    for n in range(C_BLOCKS):
        cols = slice(n * bd, (n + 1) * bd)
        xb = xc_ref[:, cols]
        xb16 = xb.astype(BF16)
        r = jax.nn.sigmoid(jnp.dot(xb16, wr_ref[n], preferred_element_type=F32)
                           + br_ref[:, cols])
        gate_i = jax.nn.sigmoid(jnp.dot(xb16, wi_ref[n], preferred_element_type=F32)
                                + bi_ref[:, cols])
        log_a = -C_C * r * sp[:, cols]
        a_ref[:, cols] = jnp.exp(log_a)
        b_ref[:, cols] = jnp.sqrt(-_expm1(2.0 * log_a)) * (gate_i * xb)

    row = lax.broadcasted_iota(jnp.int32, (SUBLANES, width), 0)

    def tile(t, h_prev):
        rows = pl.ds(pl.multiple_of(t * SUBLANES, SUBLANES), SUBLANES)
        a = a_ref[rows, :]
        b = b_ref[rows, :]
        d = 1
        while d < SUBLANES:
            a_up = pltpu.roll(a, d, 0)
            b_up = pltpu.roll(b, d, 0)
            keep = row >= d
            b = jnp.where(keep, a * b_up + b, b)
            a = jnp.where(keep, a * a_up, a)
            d *= 2
        h = b + a * h_prev
        b_ref[rows, :] = h
        return h[SUBLANES - 1:SUBLANES, :]

    h_ref[...] = lax.fori_loop(0, ts // SUBLANES, tile, h_ref[...])
    o_ref[0] = (b_ref[...] * _silu(g_ref[0].astype(F32))).astype(o_ref.dtype)


def _rglru(u, gate, gate_block, conv_w, conv_b, w_rg, b_rg, w_ig, b_ig, lam, *, ts):
    b, s, width = u.shape
    bd = width // C_BLOCKS
    row = lambda a: a.reshape(1, width).astype(F32)
    const2 = lambda bi, i: (0, 0)
    return pl.pallas_call(
        functools.partial(_rglru_kernel, ts=ts),
        grid=(b, s // ts),
        in_specs=[
            pl.BlockSpec((1, ts, width), lambda bi, i: (bi, i, 0)),
            pl.BlockSpec((1, ts, width), lambda bi, i: (bi, i, gate_block)),
            pl.BlockSpec((C_CONV, width), const2),
            pl.BlockSpec((1, width), const2),
            pl.BlockSpec((C_BLOCKS, bd, bd), lambda bi, i: (0, 0, 0)),
            pl.BlockSpec((1, width), const2),
            pl.BlockSpec((C_BLOCKS, bd, bd), lambda bi, i: (0, 0, 0)),
            pl.BlockSpec((1, width), const2),
            pl.BlockSpec((1, width), const2),
        ],
        out_specs=pl.BlockSpec((1, ts, width), lambda bi, i: (bi, i, 0)),
        out_shape=jax.ShapeDtypeStruct((b, s, width), BF16),
        scratch_shapes=[pltpu.VMEM((ts + SUBLANES, width), F32),
                        pltpu.VMEM((ts, width), F32),
                        pltpu.VMEM((ts, width), F32),
                        pltpu.VMEM((ts, width), F32),
                        pltpu.VMEM((1, width), F32)],
        compiler_params=_cparams("parallel", "arbitrary"),
        name="rglru",
    )(u, gate, conv_w.astype(F32), row(conv_b), w_rg.astype(BF16), row(b_rg),
      w_ig.astype(BF16), row(b_ig), row(lam))


def _swap_rope_halves(x):
    lane = lax.broadcasted_iota(jnp.int32, (1, LANES), 1)
    first = (lane % D_ROPE) < (D_ROPE // 2)
    return jnp.where(first, pltpu.roll(x, LANES - D_ROPE // 2, 1),
                     pltpu.roll(x, D_ROPE // 2, 1))


def _rope(x, cos, sin_signed):
    return x * cos + _swap_rope_halves(x) * sin_signed


def _mla_attn_kernel(qn_ref, qp_ref, kn_ref, kp_ref, v_ref, g_ref, cos_ref,
                     sin_ref, gains_ref, o_ref, kcat_ref, vt_ref, acc_ref, s_ref,
                     *, tq):
    i = pl.program_id(2)
    seq = kn_ref.shape[1]
    lane = lax.broadcasted_iota(jnp.int32, (1, LANES), 1)
    lo_mask = lane < D_ROPE
    g_q_nope, g_q_pe = gains_ref[0:1, :], gains_ref[1:2, :]
    g_k_nope, g_k_pe = gains_ref[2:3, :], gains_ref[3:4, :]

    def rmsnorm(x, gain):
        ms = jnp.mean(x * x, axis=-1, keepdims=True)
        return x * lax.rsqrt(ms + EPS) * gain

    @pl.when(i == 0)
    def _():
        def body(c, carry):
            rows = pl.ds(pl.multiple_of(c * tq, tq), tq)
            kp = _rope(_half_rmsnorm(kp_ref[0, rows, :].astype(F32), g_k_pe, lo_mask),
                       cos_ref[rows, :], sin_ref[rows, :])
            kn = kn_ref[0, rows, :].astype(F32)
            kcat_ref[0, rows, 0:LANES] = rmsnorm(kn[:, 0:LANES], g_k_nope).astype(BF16)
            kcat_ref[0, rows, LANES:] = jnp.where(lo_mask, kp, 0.0).astype(BF16)
            kcat_ref[1, rows, 0:LANES] = rmsnorm(kn[:, LANES:], g_k_nope).astype(BF16)
            kcat_ref[1, rows, LANES:] = jnp.where(lo_mask, 0.0, kp).astype(BF16)
            _store_vt_block(vt_ref, (0, c), v_ref[0, rows, 0:D_V])
            _store_vt_block(vt_ref, (1, c), v_ref[0, rows, D_V:])
            return carry
        lax.fori_loop(0, seq // tq, body, 0)

    scale = (D_NOPE + D_ROPE) ** -0.5 * LOG2E
    qrows = pl.ds(pl.multiple_of(i * tq, tq), tq)
    qp = _rope(_half_rmsnorm(qp_ref[0].astype(F32), g_q_pe, lo_mask),
               cos_ref[qrows, :], sin_ref[qrows, :]) * scale
    qn = qn_ref[0].astype(F32)
    qcat = (
        jnp.concatenate([(rmsnorm(qn[:, 0:LANES], g_q_nope) * scale).astype(BF16),
                         jnp.where(lo_mask, qp, 0.0).astype(BF16)], axis=-1),
        jnp.concatenate([(rmsnorm(qn[:, LANES:], g_q_nope) * scale).astype(BF16),
                         jnp.where(lo_mask, 0.0, qp).astype(BF16)], axis=-1),
    )

    def near_add(j):
        c = lax.broadcasted_iota(jnp.int32, (tq, tq), 0) + tq * (j - jnp.minimum(i, 1))
        r = lax.broadcasted_iota(jnp.int32, (tq, tq), 1)
        visible = (lax.shift_right_arithmetic(c, 6) <= lax.shift_right_arithmetic(r, 6))
        return jnp.where(visible, 0.0, NEG_INF).astype(F32)

    def stream(hh):
        def scores(blk):
            rows = pl.ds(pl.multiple_of(blk * tq, tq), tq)
            return _nt_dot(kcat_ref[hh, rows, :], qcat[hh])
        return scores, lambda blk: vt_ref[hh, blk], near_add

    _chunk_causal_flash(i, [stream(0), stream(1)], acc_ref, s_ref)

    g = g_ref[0].astype(F32)
    o_ref[0, :, 0:D_V] = (_normalised(acc_ref[0]).T * _silu(g[:, 0:D_V])).astype(o_ref.dtype)
    o_ref[0, :, D_V:] = (_normalised(acc_ref[1]).T * _silu(g[:, D_V:])).astype(o_ref.dtype)


def _mla_attention(q, kv, kpe, proj, gate_block0, cos, sin_signed, gains):
    b, s, _ = q.shape
    heads = D_HEADS
    pairs = heads // 2
    tq = ATTN_TQ
    pw = 2 * LANES
    return pl.pallas_call(
        functools.partial(_mla_attn_kernel, tq=tq),
        grid=(b, pairs, s // tq),
        in_specs=[
            pl.BlockSpec((1, tq, pw), lambda bi, p, i: (bi, i, p)),
            pl.BlockSpec((1, tq, LANES), lambda bi, p, i: (bi, i, 2 * pairs + p)),
            pl.BlockSpec((1, s, pw), lambda bi, p, i: (bi, 0, p)),
            pl.BlockSpec((1, s, LANES), lambda bi, p, i: (bi, 0, 0)),
            pl.BlockSpec((1, s, pw), lambda bi, p, i: (bi, 0, pairs + p)),
            pl.BlockSpec((1, tq, pw), lambda bi, p, i: (bi, i, gate_block0 + p)),
            pl.BlockSpec((s, LANES), lambda bi, p, i: (0, 0)),
            pl.BlockSpec((s, LANES), lambda bi, p, i: (0, 0)),
            pl.BlockSpec((4, LANES), lambda bi, p, i: (0, 0)),
        ],
        out_specs=pl.BlockSpec((1, tq, pw), lambda bi, p, i: (bi, i, p)),
        out_shape=jax.ShapeDtypeStruct((b, s, heads * D_V), BF16),
        scratch_shapes=[pltpu.VMEM((2, s, pw), BF16),
                        pltpu.VMEM((2, s // tq, V_ROWS, tq), BF16),
                        pltpu.VMEM((2, V_ROWS, tq), F32),
                        pltpu.VMEM((2, 2, tq, tq), F32)],
        compiler_params=_cparams("parallel", "parallel", "arbitrary"),
        name="mla_attention",
    )(q, q, kv, kpe, kv, proj, cos, sin_signed, gains)


MM_TM = 1024
MM_TN = 1024


def _layer_diff(x2, b, s, norm_g, w_in, qk_g, lam_vecs, subln_g, w_out, bias_tiles,
                layer_idx):
    proj = _norm_matmul(x2, norm_g, w_in.astype(BF16), out_dtype=BF16,
                        tm=MM_TM, tn=MM_TN)
    y = _diff_attention(proj.reshape(b, s, -1), lam_vecs, qk_g, subln_g,
                        bias_tiles, layer_idx)
    return _matmul_residual(y.reshape(b * s, -1), w_out.astype(BF16), x2,
                            tm=MM_TM, tn=MM_TN)


def _layer_gla(x2, b, s, norm_g, w_in, w_gate, gate_bias, out_g, w_out):
    rank, hdk = w_gate.shape
    n_main = w_in.shape[1] - rank
    w_main = w_in[:, :n_main].astype(BF16)
    w_lr = jnp.pad(w_in[:, n_main:], ((0, 0), (0, LANES - rank))).astype(BF16)
    w_gate_t = jnp.pad(w_gate, ((0, LANES - rank), (0, 0))).T.astype(BF16)
    proj = _norm_matmul(x2, norm_g, w_main, out_dtype=BF16, tm=MM_TM, tn=MM_TN)
    lr = _norm_matmul(x2, norm_g, w_lr, out_dtype=BF16, tm=MM_TM, tn=LANES)
    y = _gla(proj.reshape(b, s, -1), lr.reshape(b, s, LANES), w_gate_t, gate_bias,
             out_g, ts=512)
    return _matmul_residual(y.reshape(b * s, -1), w_out.astype(BF16), x2,
                            tm=MM_TM, tn=MM_TN)


def _layer_rglru(x2, b, s, norm_g, w_in, conv_w, conv_b, w_rg, b_rg, w_ig, b_ig,
                 lam, w_out):
    width = w_in.shape[1] // 2
    u = _norm_matmul(x2, norm_g, w_in[:, :width].astype(BF16), out_dtype=F32,
                     tm=MM_TM, tn=MM_TN)
    gate = _norm_matmul(x2, norm_g, w_in[:, width:].astype(BF16), out_dtype=BF16,
                        tm=MM_TM, tn=MM_TN)
    y = _rglru(u.reshape(b, s, width), gate.reshape(b, s, width), 0, conv_w, conv_b,
               w_rg, b_rg, w_ig, b_ig, lam, ts=256)
    return _matmul_residual(y.reshape(b * s, -1), w_out.astype(BF16), x2,
                            tm=MM_TM, tn=MM_TN)


def _rope_tables(s):
    half = D_ROPE // 2
    inv = ROPE_THETA ** (-jnp.arange(half, dtype=F32) / half)
    ang = jnp.arange(s, dtype=F32)[:, None] * inv[None, :]
    cos, sin = jnp.cos(ang), jnp.sin(ang)
    return (jnp.concatenate([cos, cos, cos, cos], axis=-1),
            jnp.concatenate([-sin, sin, -sin, sin], axis=-1))


def _layer_mla(x2, b, s, norm_g, w_in, q_lat_g, kv_lat_g, w_uq, w_ukv, qk_g, w_out):
    q_rank, kv_rank = q_lat_g.shape[0], kv_lat_g.shape[0]
    heads = D_HEADS
    lat = q_rank + kv_rank
    w_main = jnp.concatenate([w_in[:, :lat], w_in[:, lat + D_ROPE:]], axis=1).astype(BF16)
    w_kpe = w_in[:, lat:lat + D_ROPE]
    w_kpe = jnp.concatenate([w_kpe, w_kpe], axis=1).astype(BF16)
    uq = w_uq.reshape(q_rank, heads, D_NOPE + D_ROPE)
    w_uq_p = jnp.concatenate([uq[:, :, :D_NOPE].reshape(q_rank, -1),
                              uq[:, :, D_NOPE:].reshape(q_rank, -1)], axis=1).astype(BF16)
    ukv = w_ukv.reshape(kv_rank, heads, D_NOPE + D_V)
    w_ukv_p = jnp.concatenate([ukv[:, :, :D_NOPE].reshape(kv_rank, -1),
                               ukv[:, :, D_NOPE:].reshape(kv_rank, -1)], axis=1).astype(BF16)
    dup = lambda v: jnp.concatenate([v, v])
    gains = jnp.stack([qk_g[0, :D_NOPE], dup(qk_g[0, D_NOPE:]),
                       qk_g[1, :D_NOPE], dup(qk_g[1, D_NOPE:])]).astype(F32)
    cos, sin_signed = _rope_tables(s)

    proj = _norm_matmul(x2, norm_g, w_main, out_dtype=BF16, tm=MM_TM, tn=MM_TN)
    kpe = _norm_matmul(x2, norm_g, w_kpe, out_dtype=BF16, tm=MM_TM, tn=LANES)
    q = _norm_matmul(proj, q_lat_g, w_uq_p, out_dtype=BF16, tm=MM_TM, tn=MM_TN,
                     col_block=0)
    kv = _norm_matmul(proj, kv_lat_g, w_ukv_p, out_dtype=BF16, tm=MM_TM, tn=MM_TN,
                      col_block=1)
    gate_block0 = lat // (2 * LANES)
    y = _mla_attention(q.reshape(b, s, -1), kv.reshape(b, s, -1),
                       kpe.reshape(b, s, LANES), proj.reshape(b, s, -1),
                       gate_block0, cos, sin_signed, gains)
    return _matmul_residual(y.reshape(b * s, -1), w_out.astype(BF16), x2,
                            tm=MM_TM, tn=MM_TN)


def kernel(x, norm_g, rel_bias, a_w_in, a_qk_g, a_lambda, a_subln_g, a_w_out, b_w_in, b_w_gate, b_gate_bias, b_out_g, b_w_out, c_w_in, c_conv_w, c_conv_b, c_w_rgate, c_b_rgate, c_w_igate, c_b_igate, c_lambda, c_w_out, d_w_in, d_q_lat_g, d_kv_lat_g, d_w_uq, d_w_ukv, d_qk_g, d_w_out):
    b, s, d = x.shape
    depth = norm_g.shape[0]
    x2 = x.reshape(b * s, d)
    bias_tiles = _t5_tiles(rel_bias, ATTN_TQ)
    for i in range(depth):
        m, j = i % 4, i // 4
        if m == 0:
            x2 = _layer_diff(x2, b, s, norm_g[i], a_w_in[j], a_qk_g[j], a_lambda[j],
                             a_subln_g[j], a_w_out[j], bias_tiles, i)
        elif m == 1:
            x2 = _layer_gla(x2, b, s, norm_g[i], b_w_in[j], b_w_gate[j],
                            b_gate_bias[j], b_out_g[j], b_w_out[j])
        elif m == 2:
            x2 = _layer_rglru(x2, b, s, norm_g[i], c_w_in[j], c_conv_w[j], c_conv_b[j],
                              c_w_rgate[j], c_b_rgate[j], c_w_igate[j], c_b_igate[j],
                              c_lambda[j], c_w_out[j])
        else:
            x2 = _layer_mla(x2, b, s, norm_g[i], d_w_in[j], d_q_lat_g[j],
                            d_kv_lat_g[j], d_w_uq[j], d_w_ukv[j], d_qk_g[j], d_w_out[j])
    return x2.reshape(b, s, d)
```

```python
import functools
import math

import jax
import jax.numpy as jnp
from jax import lax
from jax.experimental import pallas as pl
from jax.experimental.pallas import tpu as pltpu

F32 = jnp.float32
BF16 = jnp.bfloat16

EPS = 1e-6
NEG_INF = -1e30
LOG2E = math.log2(math.e)
CHUNK = 64
LANES = 128
SUBLANES = 8
BF16_ROWS = 16
V7X_VMEM_BYTES = 64 * 1024 * 1024
VMEM_LIMIT = V7X_VMEM_BYTES - 8 * 1024 * 1024

A_HEADS = 16
A_HEAD_DIM = 64
T5_BUCKETS = 32
B_HEADS = 4
B_GATE_TAU = 16.0
C_BLOCKS = 8
C_CONV = 4
C_C = 8.0
D_HEADS = 16
D_NOPE = 128
D_ROPE = 64
D_V = 128
ROPE_THETA = 10000.0

ATTN_TQ = 512
V_ROWS = LANES + BF16_ROWS


def _cparams(*semantics):
    return pltpu.CompilerParams(dimension_semantics=semantics,
                                vmem_limit_bytes=VMEM_LIMIT)


def _sigmoid(x):
    return 0.5 * jnp.tanh(0.5 * x) + 0.5


def _silu(g):
    return g * _sigmoid(g)


def _softplus(x):
    return jnp.maximum(x, 0.0) + jnp.log(1.0 + jnp.exp(-jnp.abs(x)))


_EXPM1_SERIES_BOUND = 0.1
_EXPM1_SERIES_TERMS = 5


def _one_minus_exp(x, exp_x):
    poly = jnp.full_like(x, 1.0 / math.factorial(_EXPM1_SERIES_TERMS))
    for n in range(_EXPM1_SERIES_TERMS - 1, 0, -1):
        poly = poly * x + 1.0 / math.factorial(n)
    return jnp.where(x > -_EXPM1_SERIES_BOUND, -x * poly, 1.0 - exp_x)


def _nt_dot(a, b):
    return lax.dot_general(a, b, (((1,), (1,)), ((), ())),
                           preferred_element_type=F32)


def _norm_matmul_kernel(x_ref, g_ref, w_ref, o_ref, h_ref):
    @pl.when(pl.program_id(1) == 0)
    def _():
        x = x_ref[...].astype(F32)
        ms = jnp.mean(x * x, axis=-1, keepdims=True)
        h_ref[...] = (x * lax.rsqrt(ms + EPS) * g_ref[...]).astype(BF16)

    o_ref[...] = jnp.dot(h_ref[...], w_ref[...],
                         preferred_element_type=F32).astype(o_ref.dtype)


def _norm_matmul(x, gain, w, *, out_dtype, tm, tn, col_block=0):
    m = x.shape[0]
    k, n = w.shape
    return pl.pallas_call(
        _norm_matmul_kernel,
        grid=(m // tm, n // tn),
        in_specs=[pl.BlockSpec((tm, k), lambda i, j: (i, col_block)),
                  pl.BlockSpec((1, k), lambda i, j: (0, 0)),
                  pl.BlockSpec((k, tn), lambda i, j: (0, j))],
        out_specs=pl.BlockSpec((tm, tn), lambda i, j: (i, j)),
        out_shape=jax.ShapeDtypeStruct((m, n), out_dtype),
        scratch_shapes=[pltpu.VMEM((tm, k), BF16)],
        compiler_params=_cparams("parallel", "arbitrary"),
        name="norm_matmul",
    )(x, gain.reshape(1, k).astype(F32), w)


def _matmul_residual_kernel(y_ref, w_ref, x_ref, o_ref):
    o_ref[...] = x_ref[...] + jnp.dot(y_ref[...], w_ref[...],
                                      preferred_element_type=F32)


def _matmul_residual(y, w, x, *, tm, tn):
    m, k = y.shape
    n = w.shape[1]
    return pl.pallas_call(
        _matmul_residual_kernel,
        grid=(m // tm, n // tn),
        in_specs=[pl.BlockSpec((tm, k), lambda i, j: (i, 0)),
                  pl.BlockSpec((k, tn), lambda i, j: (0, j)),
                  pl.BlockSpec((tm, tn), lambda i, j: (i, j))],
        out_specs=pl.BlockSpec((tm, tn), lambda i, j: (i, j)),
        out_shape=jax.ShapeDtypeStruct((m, n), F32),
        compiler_params=_cparams("parallel", "arbitrary"),
        name="matmul_residual",
    )(y, w, x)


def _half_rmsnorm(x, gain, lo_mask):
    x2 = x * x
    lo = jnp.sum(jnp.where(lo_mask, x2, 0.0), axis=-1, keepdims=True)
    hi = jnp.sum(jnp.where(lo_mask, 0.0, x2), axis=-1, keepdims=True)
    ms = jnp.where(lo_mask, lo, hi) * (1.0 / 64)
    return x * lax.rsqrt(ms + EPS) * gain


def _store_vt_block(vt_ref, lead, v_rows):
    tq = v_rows.shape[0]
    vt_ref[(*lead, slice(0, LANES), slice(None))] = v_rows.astype(F32).T.astype(BF16)
    vt_ref[(*lead, slice(LANES, V_ROWS), slice(None))] = jnp.ones((BF16_ROWS, tq), BF16)


def _flash_init(s, vt):
    m = jnp.max(s, axis=0, keepdims=True)
    p = jnp.exp2(s - m).astype(BF16)
    return m, jnp.dot(vt, p, preferred_element_type=F32)


def _flash_update(s, vt, m, acc):
    m_new = jnp.maximum(m, jnp.max(s, axis=0, keepdims=True))
    alpha = jnp.exp2(m - m_new)
    p = jnp.exp2(s - m_new).astype(BF16)
    return m_new, alpha * acc + jnp.dot(vt, p, preferred_element_type=F32)


def _chunk_causal_flash(i, streams, acc_ref, s_ref):
    first = jnp.maximum(i - 1, 0)
    n_far = first

    def issue(blk, slot, add=None):
        for si, (scores, _, near_add) in enumerate(streams):
            s = scores(blk)
            s_ref[si, slot] = s if add is None else s + near_add(add)

    def update(blk, slot, ms):
        out = []
        for si, (_, values, _) in enumerate(streams):
            m, acc_ref[si] = _flash_update(s_ref[si, slot], values(blk), ms[si],
                                           acc_ref[si])
            out.append(m)
        return tuple(out)

    def pair(t2, ms):
        t = 2 * t2
        issue(t + 1, 1)
        ms = update(t, 0, ms)
        issue(t + 2, 0)
        return update(t + 1, 1, ms)

    def last_two(_, ms):
        issue(n_far - 1, 1)
        ms = update(n_far - 2, 0, ms)
        return update(n_far - 1, 1, ms)

    def last_one(_, ms):
        return update(n_far - 1, 0, ms)

    issue(first, 0, add=0)
    issue(first + 1, 1, add=1)
    ms = []
    for si, (_, values, _) in enumerate(streams):
        m, acc_ref[si] = _flash_init(s_ref[si, 0], values(first))
        ms.append(m)
    issue(0, 0)
    ms = update(first + 1, 1, tuple(ms))

    pairs = jnp.maximum(n_far - 1, 0) // 2
    rest = n_far - 2 * pairs
    ms = lax.fori_loop(0, pairs, pair, ms)
    ms = lax.fori_loop(0, (rest == 2).astype(jnp.int32), last_two, ms)
    lax.fori_loop(0, (rest == 1).astype(jnp.int32), last_one, ms)


def _normalised(acc):
    return acc[0:LANES] * (1.0 / acc[LANES:LANES + 1])


_T5_LARGE_THRESHOLDS = (12, 16, 23, 32, 46, 64, 91)


def _t5_tiles_kernel(rb_ref, o_ref, *, tq):
    h = pl.program_id(0)
    variant = pl.program_id(1)
    j = lax.broadcasted_iota(jnp.int32, (1, tq), 1)
    rel_row = jnp.where(j < tq // 2, -j, tq - j)
    n = jnp.abs(rel_row)
    large = jnp.full_like(n, 8)
    for t in _T5_LARGE_THRESHOLDS:
        large = large + (n >= t).astype(jnp.int32)
    bucket = jnp.where(rel_row > 0, T5_BUCKETS // 2, 0) + jnp.where(n < 8, n, large)
    far = rb_ref[T5_BUCKETS // 2 - 1, h]
    row = jnp.zeros((1, tq), F32)
    for b in range(T5_BUCKETS):
        row = jnp.where(bucket == b, (rb_ref[b, h] - far) * LOG2E, row)
    tile = pltpu.roll(jnp.broadcast_to(row, (2 * tq, tq)), 0, 1, stride=1, stride_axis=0)
    c = lax.broadcasted_iota(jnp.int32, (2 * tq, tq), 0)
    r = lax.broadcasted_iota(jnp.int32, (2 * tq, tq), 1)
    key_off = c - tq * variant
    bias = jnp.where(key_off - r > -_T5_LARGE_THRESHOLDS[-1], tile, 0.0)
    visible = (lax.shift_right_arithmetic(key_off, 6)
               <= lax.shift_right_arithmetic(r, 6))
    o_ref[0, 0] = jnp.where(visible, bias, NEG_INF)


def _t5_tiles(rel_bias, tq):
    nb, nh = rel_bias.shape
    return pl.pallas_call(
        functools.partial(_t5_tiles_kernel, tq=tq),
        grid=(nh, 2),
        in_specs=[pl.BlockSpec(memory_space=pltpu.SMEM)],
        out_specs=pl.BlockSpec((1, 1, 2 * tq, tq), lambda h, v: (h, v, 0, 0)),
        out_shape=jax.ShapeDtypeStruct((nh, 2, 2 * tq, tq), F32),
        compiler_params=_cparams("parallel", "arbitrary"),
        name="t5_tiles",
    )(rel_bias.astype(F32))


def _diff_attn_kernel(lamv_ref, q_ref, k_ref, v_ref, g_ref, bias_ref, qkg_ref,
                      sub_ref, o_ref, kn_ref, vt_ref, acc_ref, s_ref, *, tq, lam_init):
    i = pl.program_id(2)
    seq = k_ref.shape[1]
    lane = lax.broadcasted_iota(jnp.int32, (1, LANES), 1)
    lo_mask = lane < A_HEAD_DIM

    @pl.when(i == 0)
    def _():
        def body(c, carry):
            rows = pl.ds(pl.multiple_of(c * tq, tq), tq)
            kn_ref[rows, :] = _half_rmsnorm(
                k_ref[0, rows, :].astype(F32), qkg_ref[1:2, :], lo_mask).astype(BF16)
            _store_vt_block(vt_ref, (c,), v_ref[0, rows, :])
            return carry
        lax.fori_loop(0, seq // tq, body, 0)

    lf = lamv_ref[...].astype(F32)
    lam = (jnp.exp(jnp.sum(lf[0:1] * lf[1:2], axis=-1, keepdims=True))
           - jnp.exp(jnp.sum(lf[2:3] * lf[3:4], axis=-1, keepdims=True)) + lam_init)

    qn = (_half_rmsnorm(q_ref[0].astype(F32), qkg_ref[0:1, :], lo_mask)
          * (A_HEAD_DIM ** -0.5 * LOG2E))
    q01 = jnp.concatenate([jnp.where(lo_mask, qn, 0.0).astype(BF16),
                           jnp.where(lo_mask, 0.0, qn).astype(BF16)], axis=0)

    def scores(blk):
        return _nt_dot(kn_ref[pl.ds(pl.multiple_of(blk * tq, tq), tq), :], q01)

    def near_add(j):
        bias_t = bias_ref[0, 0, j * tq:(j + 1) * tq, :]
        return jnp.concatenate([bias_t, bias_t], axis=1)

    _chunk_causal_flash(i, [(scores, lambda blk: vt_ref[blk], near_add)], acc_ref,
                        s_ref)

    o01 = _normalised(acc_ref[0])
    o_t = o01[:, 0:tq] - lam * o01[:, tq:]
    ms_t = jnp.mean(o_t * o_t, axis=0, keepdims=True)
    o_t = o_t * lax.rsqrt(ms_t + EPS) * sub_ref[...] * (1.0 - lam_init)
    o_ref[0] = (o_t.T * _silu(g_ref[0].astype(F32))).astype(o_ref.dtype)


def _diff_attention(proj, lam_vecs, qk_g, subln_g, bias_tiles, layer_idx):
    b, s, four_w = proj.shape
    width = four_w // 4
    hblocks = width // LANES
    tq = ATTN_TQ
    lam_init = 0.8 - 0.6 * math.exp(-0.3 * layer_idx)
    qkg = jnp.concatenate([qk_g, qk_g], axis=-1).astype(F32)
    return pl.pallas_call(
        functools.partial(_diff_attn_kernel, tq=tq, lam_init=lam_init),
        grid=(b, hblocks, s // tq),
        in_specs=[
            pl.BlockSpec((4, A_HEAD_DIM), lambda bi, h, i: (0, 0)),
            pl.BlockSpec((1, tq, LANES), lambda bi, h, i: (bi, i, h)),
            pl.BlockSpec((1, s, LANES), lambda bi, h, i: (bi, 0, hblocks + h)),
            pl.BlockSpec((1, s, LANES), lambda bi, h, i: (bi, 0, 2 * hblocks + h)),
            pl.BlockSpec((1, tq, LANES), lambda bi, h, i: (bi, i, 3 * hblocks + h)),
            pl.BlockSpec((1, 1, 2 * tq, tq),
                         lambda bi, h, i: (h, jnp.minimum(i, 1), 0, 0)),
            pl.BlockSpec((2, LANES), lambda bi, h, i: (0, 0)),
            pl.BlockSpec((LANES, 1), lambda bi, h, i: (0, 0)),
        ],
        out_specs=pl.BlockSpec((1, tq, LANES), lambda bi, h, i: (bi, i, h)),
        out_shape=jax.ShapeDtypeStruct((b, s, width), BF16),
        scratch_shapes=[pltpu.VMEM((s, LANES), BF16),
                        pltpu.VMEM((s // tq, V_ROWS, tq), BF16),
                        pltpu.VMEM((1, V_ROWS, 2 * tq), F32),
                        pltpu.VMEM((1, 2, tq, 2 * tq), F32)],
        compiler_params=_cparams("parallel", "parallel", "arbitrary"),
        name="diff_attention",
    )(lam_vecs.astype(F32), proj, proj, proj, proj, bias_tiles, qkg,
      subln_g.reshape(LANES, 1).astype(F32))


def _gla_kernel(q_ref, k_ref, v_ref, g_ref, lr_ref, wgt_ref, gb_ref, og_ref,
                cm_ref, o_ref, state_ref, *, ts):
    @pl.when(pl.program_id(2) == 0)
    def _():
        state_ref[...] = jnp.zeros_like(state_ref)

    dk = q_ref.shape[2]
    z_t = _nt_dot(wgt_ref[...], lr_ref[0]) + gb_ref[...]
    la_t = -_softplus(-z_t) * (1.0 / B_GATE_TAU)
    la_hi = la_t.astype(BF16)
    la_lo = (la_t - la_hi.astype(F32)).astype(BF16)

    span = cm_ref.shape[0]
    rest_t = jnp.concatenate(
        [jnp.dot(la_hi[:, s0:s0 + span], cm_ref[...], preferred_element_type=F32)
         + jnp.dot(la_lo[:, s0:s0 + span], cm_ref[...], preferred_element_type=F32)
         for s0 in range(0, ts, span)], axis=1)
    kd_t = (k_ref[0].astype(F32).T * jnp.exp(rest_t)).astype(BF16)

    lane = lax.broadcasted_iota(jnp.int32, (1, LANES), 1)
    first_chunk = lane < CHUNK
    state = state_ref[...]
    for c in range(ts // CHUNK):
        group = slice((c // 2) * LANES, (c // 2 + 1) * LANES)
        keep = first_chunk if c % 2 == 0 else jnp.logical_not(first_chunk)
        kd_c = jnp.where(keep, kd_t[:, group], jnp.zeros((), BF16))
        t0 = slice(c * CHUNK, c * CHUNK + 1)
        state = (jnp.exp(rest_t[:, t0] + la_t[:, t0]) * state
                 + jnp.dot(kd_c, v_ref[0, group, :], preferred_element_type=F32))
        rows = slice(c * CHUNK, (c + 1) * CHUNK)
        q = (q_ref[0, rows, :].astype(F32) * (dk ** -0.5)).astype(BF16)
        o = jnp.dot(q, state.astype(BF16), preferred_element_type=F32)
        ms = jnp.mean(o * o, axis=-1, keepdims=True)
        o = o * lax.rsqrt(ms + EPS) * og_ref[...]
        o_ref[0, rows, :] = (o * _silu(g_ref[0, rows, :].astype(F32))).astype(o_ref.dtype)
    state_ref[...] = state


def _gla(proj, lr, w_gate_t, gate_bias, out_g, *, ts):
    b, s, n = proj.shape
    heads = B_HEADS
    dk = n // (6 * heads)
    dv = 2 * dk
    kq = (heads * dk) // dk
    kv = (2 * heads * dk) // dv
    span = 2 * LANES
    t_in = jnp.arange(span)[:, None]
    t_out = jnp.arange(span)[None, :]
    chunk_masks = (((t_in // CHUNK) == (t_out // CHUNK)) & (t_in > t_out)).astype(BF16)
    return pl.pallas_call(
        functools.partial(_gla_kernel, ts=ts),
        grid=(b, heads, s // ts),
        in_specs=[
            pl.BlockSpec((1, ts, dk), lambda bi, h, i: (bi, i, h)),
            pl.BlockSpec((1, ts, dk), lambda bi, h, i: (bi, i, kq + h)),
            pl.BlockSpec((1, ts, dv), lambda bi, h, i: (bi, i, kv + h)),
            pl.BlockSpec((1, ts, dv), lambda bi, h, i: (bi, i, kv + heads + h)),
            pl.BlockSpec((1, ts, LANES), lambda bi, h, i: (bi, i, 0)),
            pl.BlockSpec((dk, LANES), lambda bi, h, i: (h, 0)),
            pl.BlockSpec((dk, 1), lambda bi, h, i: (h, 0)),
            pl.BlockSpec((1, dv), lambda bi, h, i: (0, 0)),
            pl.BlockSpec((span, span), lambda bi, h, i: (0, 0)),
        ],
        out_specs=pl.BlockSpec((1, ts, dv), lambda bi, h, i: (bi, i, h)),
        out_shape=jax.ShapeDtypeStruct((b, s, heads * dv), BF16),
        scratch_shapes=[pltpu.VMEM((dk, dv), F32)],
        compiler_params=_cparams("parallel", "parallel", "arbitrary"),
        name="gla",
    )(proj, proj, proj, proj, lr, w_gate_t, gate_bias.reshape(-1, 1).astype(F32),
      out_g.reshape(1, dv).astype(F32), chunk_masks)


def _rglru_kernel(u_ref, g_ref, cw_ref, cb_ref, wr_ref, br_ref, wi_ref, bi_ref,
                  lam_ref, o_ref, ubuf_ref, xc_ref, a_ref, b_ref, h_ref, *, ts):
    width = u_ref.shape[2]
    bd = width // C_BLOCKS

    @pl.when(pl.program_id(1) == 0)
    def _():
        ubuf_ref[0:SUBLANES, :] = jnp.zeros((SUBLANES, width), F32)
        h_ref[...] = jnp.zeros_like(h_ref)

    ubuf_ref[SUBLANES:SUBLANES + ts, :] = u_ref[0]
    xc = cb_ref[...]
    for t in range(C_CONV):
        off = SUBLANES - (C_CONV - 1) + t
        xc = xc + ubuf_ref[off:off + ts, :] * cw_ref[t:t + 1, :]
    xc_ref[...] = xc
    ubuf_ref[0:SUBLANES, :] = ubuf_ref[ts:ts + SUBLANES, :]

    rate = -C_C * _softplus(-lam_ref[...])
    for n in range(C_BLOCKS):
        cols = slice(n * bd, (n + 1) * bd)
        xb = xc_ref[:, cols]
        xb16 = xb.astype(BF16)
        r = _sigmoid(jnp.dot(xb16, wr_ref[n], preferred_element_type=F32)
                     + br_ref[:, cols])
        gate_i = _sigmoid(jnp.dot(xb16, wi_ref[n], preferred_element_type=F32)
                          + bi_ref[:, cols])
        log_a = r * rate[:, cols]
        a = jnp.exp(log_a)
        a_ref[:, cols] = a
        b_ref[:, cols] = jnp.sqrt(_one_minus_exp(2.0 * log_a, a * a)) * (gate_i * xb)

    row = lax.broadcasted_iota(jnp.int32, (SUBLANES, width), 0)

    def tile(t, h_prev):
        rows = pl.ds(pl.multiple_of(t * SUBLANES, SUBLANES), SUBLANES)
        a = a_ref[rows, :]
        b = b_ref[rows, :]
        d = 1
        while d < SUBLANES:
            a_up = pltpu.roll(a, d, 0)
            b_up = pltpu.roll(b, d, 0)
            keep = row >= d
            b = jnp.where(keep, a * b_up + b, b)
            a = jnp.where(keep, a * a_up, a)
            d *= 2
        h = b + a * h_prev
        b_ref[rows, :] = h
        return h[SUBLANES - 1:SUBLANES, :]

    h_ref[...] = lax.fori_loop(0, ts // SUBLANES, tile, h_ref[...])
    o_ref[0] = (b_ref[...] * _silu(g_ref[0].astype(F32))).astype(o_ref.dtype)


def _rglru(u, gate, gate_block, conv_w, conv_b, w_rg, b_rg, w_ig, b_ig, lam, *, ts):
    b, s, width = u.shape
    bd = width // C_BLOCKS
    row = lambda a: a.reshape(1, width).astype(F32)
    const2 = lambda bi, i: (0, 0)
    return pl.pallas_call(
        functools.partial(_rglru_kernel, ts=ts),
        grid=(b, s // ts),
        in_specs=[
            pl.BlockSpec((1, ts, width), lambda bi, i: (bi, i, 0)),
            pl.BlockSpec((1, ts, width), lambda bi, i: (bi, i, gate_block)),
            pl.BlockSpec((C_CONV, width), const2),
            pl.BlockSpec((1, width), const2),
            pl.BlockSpec((C_BLOCKS, bd, bd), lambda bi, i: (0, 0, 0)),
            pl.BlockSpec((1, width), const2),
            pl.BlockSpec((C_BLOCKS, bd, bd), lambda bi, i: (0, 0, 0)),
            pl.BlockSpec((1, width), const2),
            pl.BlockSpec((1, width), const2),
        ],
        out_specs=pl.BlockSpec((1, ts, width), lambda bi, i: (bi, i, 0)),
        out_shape=jax.ShapeDtypeStruct((b, s, width), BF16),
        scratch_shapes=[pltpu.VMEM((ts + SUBLANES, width), F32),
                        pltpu.VMEM((ts, width), F32),
                        pltpu.VMEM((ts, width), F32),
                        pltpu.VMEM((ts, width), F32),
                        pltpu.VMEM((1, width), F32)],
        compiler_params=_cparams("parallel", "arbitrary"),
        name="rglru",
    )(u, gate, conv_w.astype(F32), row(conv_b), w_rg.astype(BF16), row(b_rg),
      w_ig.astype(BF16), row(b_ig), row(lam))


def _swap_rope_halves(x):
    lane = lax.broadcasted_iota(jnp.int32, (1, LANES), 1)
    first = (lane % D_ROPE) < (D_ROPE // 2)
    return jnp.where(first, pltpu.roll(x, LANES - D_ROPE // 2, 1),
                     pltpu.roll(x, D_ROPE // 2, 1))


def _rope(x, cos, sin_signed):
    return x * cos + _swap_rope_halves(x) * sin_signed


def _mla_attn_kernel(qn_ref, qp_ref, kn_ref, kp_ref, v_ref, g_ref, cos_ref,
                     sin_ref, gains_ref, o_ref, kcat_ref, vt_ref, acc_ref, s_ref,
                     *, tq):
    i = pl.program_id(2)
    seq = kn_ref.shape[1]
    lane = lax.broadcasted_iota(jnp.int32, (1, LANES), 1)
    lo_mask = lane < D_ROPE
    g_q_nope, g_q_pe = gains_ref[0:1, :], gains_ref[1:2, :]
    g_k_nope, g_k_pe = gains_ref[2:3, :], gains_ref[3:4, :]

    def rmsnorm(x, gain):
        ms = jnp.mean(x * x, axis=-1, keepdims=True)
        return x * lax.rsqrt(ms + EPS) * gain

    @pl.when(i == 0)
    def _():
        def body(c, carry):
            rows = pl.ds(pl.multiple_of(c * tq, tq), tq)
            kp = _rope(_half_rmsnorm(kp_ref[0, rows, :].astype(F32), g_k_pe, lo_mask),
                       cos_ref[rows, :], sin_ref[rows, :])
            kn = kn_ref[0, rows, :].astype(F32)
            kcat_ref[0, rows, 0:LANES] = rmsnorm(kn[:, 0:LANES], g_k_nope).astype(BF16)
            kcat_ref[0, rows, LANES:] = jnp.where(lo_mask, kp, 0.0).astype(BF16)
            kcat_ref[1, rows, 0:LANES] = rmsnorm(kn[:, LANES:], g_k_nope).astype(BF16)
            kcat_ref[1, rows, LANES:] = jnp.where(lo_mask, 0.0, kp).astype(BF16)
            _store_vt_block(vt_ref, (0, c), v_ref[0, rows, 0:D_V])
            _store_vt_block(vt_ref, (1, c), v_ref[0, rows, D_V:])
            return carry
        lax.fori_loop(0, seq // tq, body, 0)

    scale = (D_NOPE + D_ROPE) ** -0.5 * LOG2E
    qrows = pl.ds(pl.multiple_of(i * tq, tq), tq)
    qp = _rope(_half_rmsnorm(qp_ref[0].astype(F32), g_q_pe, lo_mask),
               cos_ref[qrows, :], sin_ref[qrows, :]) * scale
    qn = qn_ref[0].astype(F32)
    qcat = (
        jnp.concatenate([(rmsnorm(qn[:, 0:LANES], g_q_nope) * scale).astype(BF16),
                         jnp.where(lo_mask, qp, 0.0).astype(BF16)], axis=-1),
        jnp.concatenate([(rmsnorm(qn[:, LANES:], g_q_nope) * scale).astype(BF16),
                         jnp.where(lo_mask, 0.0, qp).astype(BF16)], axis=-1),
    )

    def near_add(j):
        c = lax.broadcasted_iota(jnp.int32, (tq, tq), 0) + tq * (j - jnp.minimum(i, 1))
        r = lax.broadcasted_iota(jnp.int32, (tq, tq), 1)
        visible = (lax.shift_right_arithmetic(c, 6) <= lax.shift_right_arithmetic(r, 6))
        return jnp.where(visible, 0.0, NEG_INF).astype(F32)

    def stream(hh):
        def scores(blk):
            rows = pl.ds(pl.multiple_of(blk * tq, tq), tq)
            return _nt_dot(kcat_ref[hh, rows, :], qcat[hh])
        return scores, lambda blk: vt_ref[hh, blk], near_add

    _chunk_causal_flash(i, [stream(0), stream(1)], acc_ref, s_ref)

    g = g_ref[0].astype(F32)
    o_ref[0, :, 0:D_V] = (_normalised(acc_ref[0]).T * _silu(g[:, 0:D_V])).astype(o_ref.dtype)
    o_ref[0, :, D_V:] = (_normalised(acc_ref[1]).T * _silu(g[:, D_V:])).astype(o_ref.dtype)


def _mla_attention(q, kv, kpe, proj, gate_block0, cos, sin_signed, gains):
    b, s, _ = q.shape
    heads = D_HEADS
    pairs = heads // 2
    tq = ATTN_TQ
    pw = 2 * LANES
    return pl.pallas_call(
        functools.partial(_mla_attn_kernel, tq=tq),
        grid=(b, pairs, s // tq),
        in_specs=[
            pl.BlockSpec((1, tq, pw), lambda bi, p, i: (bi, i, p)),
            pl.BlockSpec((1, tq, LANES), lambda bi, p, i: (bi, i, 2 * pairs + p)),
            pl.BlockSpec((1, s, pw), lambda bi, p, i: (bi, 0, p)),
            pl.BlockSpec((1, s, LANES), lambda bi, p, i: (bi, 0, 0)),
            pl.BlockSpec((1, s, pw), lambda bi, p, i: (bi, 0, pairs + p)),
            pl.BlockSpec((1, tq, pw), lambda bi, p, i: (bi, i, gate_block0 + p)),
            pl.BlockSpec((s, LANES), lambda bi, p, i: (0, 0)),
            pl.BlockSpec((s, LANES), lambda bi, p, i: (0, 0)),
            pl.BlockSpec((4, LANES), lambda bi, p, i: (0, 0)),
        ],
        out_specs=pl.BlockSpec((1, tq, pw), lambda bi, p, i: (bi, i, p)),
        out_shape=jax.ShapeDtypeStruct((b, s, heads * D_V), BF16),
        scratch_shapes=[pltpu.VMEM((2, s, pw), BF16),
                        pltpu.VMEM((2, s // tq, V_ROWS, tq), BF16),
                        pltpu.VMEM((2, V_ROWS, tq), F32),
                        pltpu.VMEM((2, 2, tq, tq), F32)],
        compiler_params=_cparams("parallel", "parallel", "arbitrary"),
        name="mla_attention",
    )(q, q, kv, kpe, kv, proj, cos, sin_signed, gains)


MM_TM = 1024
MM_TN = 1024


def _layer_diff(x2, b, s, norm_g, w_in, qk_g, lam_vecs, subln_g, w_out, bias_tiles,
                layer_idx):
    proj = _norm_matmul(x2, norm_g, w_in.astype(BF16), out_dtype=BF16,
                        tm=MM_TM, tn=MM_TN)
    y = _diff_attention(proj.reshape(b, s, -1), lam_vecs, qk_g, subln_g,
                        bias_tiles, layer_idx)
    return _matmul_residual(y.reshape(b * s, -1), w_out.astype(BF16), x2,
                            tm=MM_TM, tn=MM_TN)


def _layer_gla(x2, b, s, norm_g, w_in, w_gate, gate_bias, out_g, w_out):
    rank, hdk = w_gate.shape
    n_main = w_in.shape[1] - rank
    w_main = w_in[:, :n_main].astype(BF16)
    w_lr = jnp.pad(w_in[:, n_main:], ((0, 0), (0, LANES - rank))).astype(BF16)
    w_gate_t = jnp.pad(w_gate, ((0, LANES - rank), (0, 0))).T.astype(BF16)
    proj = _norm_matmul(x2, norm_g, w_main, out_dtype=BF16, tm=MM_TM, tn=MM_TN)
    lr = _norm_matmul(x2, norm_g, w_lr, out_dtype=BF16, tm=MM_TM, tn=LANES)
    y = _gla(proj.reshape(b, s, -1), lr.reshape(b, s, LANES), w_gate_t, gate_bias,
             out_g, ts=512)
    return _matmul_residual(y.reshape(b * s, -1), w_out.astype(BF16), x2,
                            tm=MM_TM, tn=MM_TN)


def _layer_rglru(x2, b, s, norm_g, w_in, conv_w, conv_b, w_rg, b_rg, w_ig, b_ig,
                 lam, w_out):
    width = w_in.shape[1] // 2
    u = _norm_matmul(x2, norm_g, w_in[:, :width].astype(BF16), out_dtype=F32,
                     tm=MM_TM, tn=MM_TN)
    gate = _norm_matmul(x2, norm_g, w_in[:, width:].astype(BF16), out_dtype=BF16,
                        tm=MM_TM, tn=MM_TN)
    y = _rglru(u.reshape(b, s, width), gate.reshape(b, s, width), 0, conv_w, conv_b,
               w_rg, b_rg, w_ig, b_ig, lam, ts=256)
    return _matmul_residual(y.reshape(b * s, -1), w_out.astype(BF16), x2,
                            tm=MM_TM, tn=MM_TN)


def _rope_tables(s):
    half = D_ROPE // 2
    inv = ROPE_THETA ** (-jnp.arange(half, dtype=F32) / half)
    ang = jnp.arange(s, dtype=F32)[:, None] * inv[None, :]
    cos, sin = jnp.cos(ang), jnp.sin(ang)
    return (jnp.concatenate([cos, cos, cos, cos], axis=-1),
            jnp.concatenate([-sin, sin, -sin, sin], axis=-1))


def _layer_mla(x2, b, s, norm_g, w_in, q_lat_g, kv_lat_g, w_uq, w_ukv, qk_g, w_out):
    q_rank, kv_rank = q_lat_g.shape[0], kv_lat_g.shape[0]
    heads = D_HEADS
    lat = q_rank + kv_rank
    w_main = jnp.concatenate([w_in[:, :lat], w_in[:, lat + D_ROPE:]], axis=1).astype(BF16)
    w_kpe = w_in[:, lat:lat + D_ROPE]
    w_kpe = jnp.concatenate([w_kpe, w_kpe], axis=1).astype(BF16)
    uq = w_uq.reshape(q_rank, heads, D_NOPE + D_ROPE)
    w_uq_p = jnp.concatenate([uq[:, :, :D_NOPE].reshape(q_rank, -1),
                              uq[:, :, D_NOPE:].reshape(q_rank, -1)], axis=1).astype(BF16)
    ukv = w_ukv.reshape(kv_rank, heads, D_NOPE + D_V)
    w_ukv_p = jnp.concatenate([ukv[:, :, :D_NOPE].reshape(kv_rank, -1),
                               ukv[:, :, D_NOPE:].reshape(kv_rank, -1)], axis=1).astype(BF16)
    dup = lambda v: jnp.concatenate([v, v])
    gains = jnp.stack([qk_g[0, :D_NOPE], dup(qk_g[0, D_NOPE:]),
                       qk_g[1, :D_NOPE], dup(qk_g[1, D_NOPE:])]).astype(F32)
    cos, sin_signed = _rope_tables(s)

    proj = _norm_matmul(x2, norm_g, w_main, out_dtype=BF16, tm=MM_TM, tn=MM_TN)
    kpe = _norm_matmul(x2, norm_g, w_kpe, out_dtype=BF16, tm=MM_TM, tn=LANES)
    q = _norm_matmul(proj, q_lat_g, w_uq_p, out_dtype=BF16, tm=MM_TM, tn=MM_TN,
                     col_block=0)
    kv = _norm_matmul(proj, kv_lat_g, w_ukv_p, out_dtype=BF16, tm=MM_TM, tn=MM_TN,
                      col_block=1)
    gate_block0 = lat // (2 * LANES)
    y = _mla_attention(q.reshape(b, s, -1), kv.reshape(b, s, -1),
                       kpe.reshape(b, s, LANES), proj.reshape(b, s, -1),
                       gate_block0, cos, sin_signed, gains)
    return _matmul_residual(y.reshape(b * s, -1), w_out.astype(BF16), x2,
                            tm=MM_TM, tn=MM_TN)


def kernel(x, norm_g, rel_bias, a_w_in, a_qk_g, a_lambda, a_subln_g, a_w_out, b_w_in, b_w_gate, b_gate_bias, b_out_g, b_w_out, c_w_in, c_conv_w, c_conv_b, c_w_rgate, c_b_rgate, c_w_igate, c_b_igate, c_lambda, c_w_out, d_w_in, d_q_lat_g, d_kv_lat_g, d_w_uq, d_w_ukv, d_qk_g, d_w_out):
    b, s, d = x.shape
    depth = norm_g.shape[0]
    x2 = x.reshape(b * s, d)
    bias_tiles = _t5_tiles(rel_bias, ATTN_TQ)
    for i in range(depth):
        m, j = i % 4, i // 4
        if m == 0:
            x2 = _layer_diff(x2, b, s, norm_g[i], a_w_in[j], a_qk_g[j], a_lambda[j],
                             a_subln_g[j], a_w_out[j], bias_tiles, i)
        elif m == 1:
            x2 = _layer_gla(x2, b, s, norm_g[i], b_w_in[j], b_w_gate[j],
                            b_gate_bias[j], b_out_g[j], b_w_out[j])
        elif m == 2:
            x2 = _layer_rglru(x2, b, s, norm_g[i], c_w_in[j], c_conv_w[j], c_conv_b[j],
                              c_w_rgate[j], c_b_rgate[j], c_w_igate[j], c_b_igate[j],
                              c_lambda[j], c_w_out[j])
        else:
            x2 = _layer_mla(x2, b, s, norm_g[i], d_w_in[j], d_q_lat_g[j],
                            d_kv_lat_g[j], d_w_uq[j], d_w_ukv[j], d_qk_g[j], d_w_out[j])
    return x2.reshape(b, s, d)
```

```python
import functools
import math

import jax
import jax.numpy as jnp
from jax import lax
from jax.experimental import pallas as pl
from jax.experimental.pallas import tpu as pltpu

F32 = jnp.float32
BF16 = jnp.bfloat16

EPS = 1e-6
NEG_INF = -1e30
LOG2E = math.log2(math.e)
CHUNK = 64
LANES = 128
SUBLANES = 8
BF16_ROWS = 16
V7X_VMEM_BYTES = 64 * 1024 * 1024
VMEM_LIMIT = V7X_VMEM_BYTES - 8 * 1024 * 1024

A_HEADS = 16
A_HEAD_DIM = 64
T5_BUCKETS = 32
B_HEADS = 4
B_GATE_TAU = 16.0
C_BLOCKS = 8
C_CONV = 4
C_C = 8.0
D_HEADS = 16
D_NOPE = 128
D_ROPE = 64
D_V = 128
ROPE_THETA = 10000.0

ATTN_TQ = 512
V_ROWS = LANES + BF16_ROWS


def _cparams(*semantics):
    return pltpu.CompilerParams(dimension_semantics=semantics,
                                vmem_limit_bytes=VMEM_LIMIT)


def _sigmoid(x):
    return 0.5 * jnp.tanh(0.5 * x) + 0.5


def _silu(g):
    return g * _sigmoid(g)


def _softplus(x):
    return jnp.maximum(x, 0.0) + jnp.log(1.0 + jnp.exp(-jnp.abs(x)))


_EXPM1_SERIES_BOUND = 0.1
_EXPM1_SERIES_TERMS = 5


def _one_minus_exp(x, exp_x):
    poly = jnp.full_like(x, 1.0 / math.factorial(_EXPM1_SERIES_TERMS))
    for n in range(_EXPM1_SERIES_TERMS - 1, 0, -1):
        poly = poly * x + 1.0 / math.factorial(n)
    return jnp.where(x > -_EXPM1_SERIES_BOUND, -x * poly, 1.0 - exp_x)


def _nt_dot(a, b):
    return lax.dot_general(a, b, (((1,), (1,)), ((), ())),
                           preferred_element_type=F32)


def _norm_matmul_kernel(*refs, has_side, n_first):
    refs = list(refs)
    x_ref, g_ref, w_ref = refs[:3]
    del refs[:3]
    ws_ref = refs.pop(0) if has_side else None
    o_ref = refs.pop(0)
    rest_ref = refs.pop(0) if n_first is not None else None
    side_ref = refs.pop(0) if has_side else None
    h_ref = refs.pop(0)
    j = pl.program_id(1)

    @pl.when(j == 0)
    def _():
        x = x_ref[...].astype(F32)
        ms = jnp.mean(x * x, axis=-1, keepdims=True)
        h_ref[...] = (x * lax.rsqrt(ms + EPS) * g_ref[...]).astype(BF16)
        if has_side:
            side_ref[...] = jnp.dot(h_ref[...], ws_ref[...],
                                    preferred_element_type=F32).astype(side_ref.dtype)

    res = jnp.dot(h_ref[...], w_ref[...], preferred_element_type=F32)
    if n_first is None:
        o_ref[...] = res.astype(o_ref.dtype)
    else:
        @pl.when(j < n_first)
        def _():
            o_ref[...] = res.astype(o_ref.dtype)

        @pl.when(j >= n_first)
        def _():
            rest_ref[...] = res.astype(rest_ref.dtype)


def _norm_matmul(x, gain, w, *, out_dtype, tm, tn, col_block=0, w_side=None,
                 first=None):
    m = x.shape[0]
    k, n = w.shape
    n_first = None if first is None else first[0] // tn
    in_specs = [pl.BlockSpec((tm, k), lambda i, j: (i, col_block)),
                pl.BlockSpec((1, k), lambda i, j: (0, 0)),
                pl.BlockSpec((k, tn), lambda i, j: (0, j))]
    args = [x, gain.reshape(1, k).astype(F32), w]
    if w_side is not None:
        in_specs.append(pl.BlockSpec((k, LANES), lambda i, j: (0, 0)))
        args.append(w_side)
    if first is None:
        out_specs = [pl.BlockSpec((tm, tn), lambda i, j: (i, j))]
        out_shape = [jax.ShapeDtypeStruct((m, n), out_dtype)]
    else:
        out_specs = [pl.BlockSpec((tm, tn), lambda i, j: (i, jnp.minimum(j, n_first - 1))),
                     pl.BlockSpec((tm, tn), lambda i, j: (i, jnp.maximum(j - n_first, 0)))]
        out_shape = [jax.ShapeDtypeStruct((m, first[0]), first[1]),
                     jax.ShapeDtypeStruct((m, n - first[0]), out_dtype)]
    if w_side is not None:
        out_specs.append(pl.BlockSpec((tm, LANES), lambda i, j: (i, 0)))
        out_shape.append(jax.ShapeDtypeStruct((m, LANES), BF16))
    outs = pl.pallas_call(
        functools.partial(_norm_matmul_kernel, has_side=w_side is not None,
                          n_first=n_first),
        grid=(m // tm, n // tn),
        in_specs=in_specs,
        out_specs=out_specs,
        out_shape=out_shape,
        scratch_shapes=[pltpu.VMEM((tm, k), BF16)],
        compiler_params=_cparams("parallel", "arbitrary"),
        name="norm_matmul",
    )(*args)
    return outs[0] if len(outs) == 1 else tuple(outs)


def _matmul_residual_kernel(y_ref, w_ref, x_ref, o_ref):
    o_ref[...] = x_ref[...] + jnp.dot(y_ref[...], w_ref[...],
                                      preferred_element_type=F32)


def _matmul_residual(y, w, x, *, tm):
    m, k = y.shape
    n = w.shape[1]
    return pl.pallas_call(
        _matmul_residual_kernel,
        grid=(m // tm,),
        in_specs=[pl.BlockSpec((tm, k), lambda i: (i, 0)),
                  pl.BlockSpec((k, n), lambda i: (0, 0)),
                  pl.BlockSpec((tm, n), lambda i: (i, 0))],
        out_specs=pl.BlockSpec((tm, n), lambda i: (i, 0)),
        out_shape=jax.ShapeDtypeStruct((m, n), F32),
        compiler_params=_cparams("parallel"),
        name="matmul_residual",
    )(y, w, x)


def _group_rmsnorm(x, gain, group):
    r = lax.broadcasted_iota(jnp.int32, (LANES, LANES), 0)
    c = lax.broadcasted_iota(jnp.int32, (LANES, LANES), 1)
    ones = jnp.where(r // group == c // group, 1.0, 0.0).astype(BF16)
    x2 = x * x
    hi = x2.astype(BF16)
    lo = (x2 - hi.astype(F32)).astype(BF16)
    ss = (jnp.dot(hi, ones, preferred_element_type=F32)
          + jnp.dot(lo, ones, preferred_element_type=F32))
    return x * lax.rsqrt(ss * (1.0 / group) + EPS) * gain


def _half_rmsnorm(x, gain, lo_mask):
    x2 = x * x
    lo = jnp.sum(jnp.where(lo_mask, x2, 0.0), axis=-1, keepdims=True)
    hi = jnp.sum(jnp.where(lo_mask, 0.0, x2), axis=-1, keepdims=True)
    ms = jnp.where(lo_mask, lo, hi) * (1.0 / 64)
    return x * lax.rsqrt(ms + EPS) * gain


def _store_vt_block(vt_ref, lead, v_rows):
    tq = v_rows.shape[0]
    r = lax.broadcasted_iota(jnp.int32, (LANES, LANES), 0)
    c = lax.broadcasted_iota(jnp.int32, (LANES, LANES), 1)
    eye = jnp.where(r == c, 1.0, 0.0).astype(BF16)
    vt_ref[(*lead, slice(0, LANES), slice(None))] = _nt_dot(eye, v_rows).astype(BF16)
    vt_ref[(*lead, slice(LANES, V_ROWS), slice(None))] = jnp.ones((BF16_ROWS, tq), BF16)


def _flash_init(s, vt):
    m = jnp.max(s, axis=0, keepdims=True)
    p = jnp.exp2(s - m).astype(BF16)
    return m, jnp.dot(vt, p, preferred_element_type=F32)


def _flash_update(s, vt, m, acc):
    m_new = jnp.maximum(m, jnp.max(s, axis=0, keepdims=True))
    alpha = jnp.exp2(m - m_new)
    p = jnp.exp2(s - m_new).astype(BF16)
    return m_new, alpha * acc + jnp.dot(vt, p, preferred_element_type=F32)


def _chunk_causal_flash(i, streams, acc_ref, s_ref):
    first = jnp.maximum(i - 1, 0)
    n_far = first

    def issue(blk, slot, add=None):
        for si, (scores, _, near_add) in enumerate(streams):
            s = scores(blk)
            s_ref[si, slot] = s if add is None else s + near_add(add)

    def update(blk, slot, ms):
        out = []
        for si, (_, values, _) in enumerate(streams):
            m, acc_ref[si] = _flash_update(s_ref[si, slot], values(blk), ms[si],
                                           acc_ref[si])
            out.append(m)
        return tuple(out)

    def pair(t2, ms):
        t = 2 * t2
        issue(t + 1, 1)
        ms = update(t, 0, ms)
        issue(t + 2, 0)
        return update(t + 1, 1, ms)

    def last_two(_, ms):
        issue(n_far - 1, 1)
        ms = update(n_far - 2, 0, ms)
        return update(n_far - 1, 1, ms)

    def last_one(_, ms):
        return update(n_far - 1, 0, ms)

    issue(first, 0, add=0)
    issue(first + 1, 1, add=1)
    ms = []
    for si, (_, values, _) in enumerate(streams):
        m, acc_ref[si] = _flash_init(s_ref[si, 0], values(first))
        ms.append(m)
    issue(0, 0)
    ms = update(first + 1, 1, tuple(ms))

    pairs = jnp.maximum(n_far - 1, 0) // 2
    rest = n_far - 2 * pairs
    ms = lax.fori_loop(0, pairs, pair, ms)
    ms = lax.fori_loop(0, (rest == 2).astype(jnp.int32), last_two, ms)
    lax.fori_loop(0, (rest == 1).astype(jnp.int32), last_one, ms)


def _normalised(acc):
    return acc[0:LANES] * (1.0 / acc[LANES:LANES + 1])


_T5_LARGE_THRESHOLDS = (12, 16, 23, 32, 46, 64, 91)


def _t5_tiles_kernel(rb_ref, o_ref, *, tq):
    h = pl.program_id(0)
    variant = pl.program_id(1)
    j = lax.broadcasted_iota(jnp.int32, (1, tq), 1)
    rel_row = jnp.where(j < tq // 2, -j, tq - j)
    n = jnp.abs(rel_row)
    large = jnp.full_like(n, 8)
    for t in _T5_LARGE_THRESHOLDS:
        large = large + (n >= t).astype(jnp.int32)
    bucket = jnp.where(rel_row > 0, T5_BUCKETS // 2, 0) + jnp.where(n < 8, n, large)
    far = rb_ref[T5_BUCKETS // 2 - 1, h]
    row = jnp.zeros((1, tq), F32)
    for b in range(T5_BUCKETS):
        row = jnp.where(bucket == b, (rb_ref[b, h] - far) * LOG2E, row)
    tile = pltpu.roll(jnp.broadcast_to(row, (2 * tq, tq)), 0, 1, stride=1, stride_axis=0)
    c = lax.broadcasted_iota(jnp.int32, (2 * tq, tq), 0)
    r = lax.broadcasted_iota(jnp.int32, (2 * tq, tq), 1)
    key_off = c - tq * variant
    bias = jnp.where(key_off - r > -_T5_LARGE_THRESHOLDS[-1], tile, 0.0)
    visible = (lax.shift_right_arithmetic(key_off, 6)
               <= lax.shift_right_arithmetic(r, 6))
    o_ref[0, 0] = jnp.where(visible, bias, NEG_INF)


def _t5_tiles(rel_bias, tq):
    nb, nh = rel_bias.shape
    return pl.pallas_call(
        functools.partial(_t5_tiles_kernel, tq=tq),
        grid=(nh, 2),
        in_specs=[pl.BlockSpec(memory_space=pltpu.SMEM)],
        out_specs=pl.BlockSpec((1, 1, 2 * tq, tq), lambda h, v: (h, v, 0, 0)),
        out_shape=jax.ShapeDtypeStruct((nh, 2, 2 * tq, tq), F32),
        compiler_params=_cparams("parallel", "arbitrary"),
        name="t5_tiles",
    )(rel_bias.astype(F32))


def _diff_attn_kernel(lamv_ref, q_ref, k_ref, v_ref, g_ref, bias_ref, qkg_ref,
                      sub_ref, o_ref, kn_ref, vt_ref, acc_ref, s_ref, *, tq, lam_init):
    i = pl.program_id(2)
    seq = k_ref.shape[1]
    lane = lax.broadcasted_iota(jnp.int32, (1, LANES), 1)
    lo_mask = lane < A_HEAD_DIM

    @pl.when(i == 0)
    def _():
        def body(c, carry):
            rows = pl.ds(pl.multiple_of(c * tq, tq), tq)
            kn_ref[rows, :] = _group_rmsnorm(
                k_ref[0, rows, :].astype(F32), qkg_ref[1:2, :], A_HEAD_DIM).astype(BF16)
            _store_vt_block(vt_ref, (c,), v_ref[0, rows, :])
            return carry
        lax.fori_loop(0, seq // tq, body, 0)

    lf = lamv_ref[...].astype(F32)
    lam = (jnp.exp(jnp.sum(lf[0:1] * lf[1:2], axis=-1, keepdims=True))
           - jnp.exp(jnp.sum(lf[2:3] * lf[3:4], axis=-1, keepdims=True)) + lam_init)

    qn = (_half_rmsnorm(q_ref[0].astype(F32), qkg_ref[0:1, :], lo_mask)
          * (A_HEAD_DIM ** -0.5 * LOG2E))
    q01 = jnp.concatenate([jnp.where(lo_mask, qn, 0.0).astype(BF16),
                           jnp.where(lo_mask, 0.0, qn).astype(BF16)], axis=0)

    def scores(blk):
        return _nt_dot(kn_ref[pl.ds(pl.multiple_of(blk * tq, tq), tq), :], q01)

    def near_add(j):
        bias_t = bias_ref[0, 0, j * tq:(j + 1) * tq, :]
        return jnp.concatenate([bias_t, bias_t], axis=1)

    _chunk_causal_flash(i, [(scores, lambda blk: vt_ref[blk], near_add)], acc_ref,
                        s_ref)

    o01 = _normalised(acc_ref[0])
    o_t = o01[:, 0:tq] - lam * o01[:, tq:]
    ms_t = jnp.mean(o_t * o_t, axis=0, keepdims=True)
    o_t = o_t * lax.rsqrt(ms_t + EPS) * sub_ref[...] * (1.0 - lam_init)
    o_ref[0] = (o_t.T * _silu(g_ref[0].astype(F32))).astype(o_ref.dtype)


def _diff_attention(proj, lam_vecs, qk_g, subln_g, bias_tiles, layer_idx):
    b, s, four_w = proj.shape
    width = four_w // 4
    hblocks = width // LANES
    tq = ATTN_TQ
    lam_init = 0.8 - 0.6 * math.exp(-0.3 * layer_idx)
    qkg = jnp.concatenate([qk_g, qk_g], axis=-1).astype(F32)
    return pl.pallas_call(
        functools.partial(_diff_attn_kernel, tq=tq, lam_init=lam_init),
        grid=(b, hblocks, s // tq),
        in_specs=[
            pl.BlockSpec((4, A_HEAD_DIM), lambda bi, h, i: (0, 0)),
            pl.BlockSpec((1, tq, LANES), lambda bi, h, i: (bi, i, h)),
            pl.BlockSpec((1, s, LANES), lambda bi, h, i: (bi, 0, hblocks + h)),
            pl.BlockSpec((1, s, LANES), lambda bi, h, i: (bi, 0, 2 * hblocks + h)),
            pl.BlockSpec((1, tq, LANES), lambda bi, h, i: (bi, i, 3 * hblocks + h)),
            pl.BlockSpec((1, 1, 2 * tq, tq),
                         lambda bi, h, i: (h, jnp.minimum(i, 1), 0, 0)),
            pl.BlockSpec((2, LANES), lambda bi, h, i: (0, 0)),
            pl.BlockSpec((LANES, 1), lambda bi, h, i: (0, 0)),
        ],
        out_specs=pl.BlockSpec((1, tq, LANES), lambda bi, h, i: (bi, i, h)),
        out_shape=jax.ShapeDtypeStruct((b, s, width), BF16),
        scratch_shapes=[pltpu.VMEM((s, LANES), BF16),
                        pltpu.VMEM((s // tq, V_ROWS, tq), BF16),
                        pltpu.VMEM((1, V_ROWS, 2 * tq), F32),
                        pltpu.VMEM((1, 2, tq, 2 * tq), F32)],
        compiler_params=_cparams("parallel", "parallel", "arbitrary"),
        name="diff_attention",
    )(lam_vecs.astype(F32), proj, proj, proj, proj, bias_tiles, qkg,
      subln_g.reshape(LANES, 1).astype(F32))


def _gla_kernel(q_ref, k_ref, v_ref, g_ref, lr_ref, wgt_ref, gb_ref, og_ref,
                cm_ref, o_ref, state_ref, *, ts):
    @pl.when(pl.program_id(2) == 0)
    def _():
        state_ref[...] = jnp.zeros_like(state_ref)

    dk = q_ref.shape[2]
    z_t = _nt_dot(wgt_ref[...], lr_ref[0]) + gb_ref[...]
    la_t = -_softplus(-z_t) * (1.0 / B_GATE_TAU)
    la_hi = la_t.astype(BF16)
    la_lo = (la_t - la_hi.astype(F32)).astype(BF16)

    span = cm_ref.shape[0]
    rest_t = jnp.concatenate(
        [jnp.dot(la_hi[:, s0:s0 + span], cm_ref[...], preferred_element_type=F32)
         + jnp.dot(la_lo[:, s0:s0 + span], cm_ref[...], preferred_element_type=F32)
         for s0 in range(0, ts, span)], axis=1)
    kd_t = (k_ref[0].astype(F32).T * jnp.exp(rest_t)).astype(BF16)

    lane = lax.broadcasted_iota(jnp.int32, (1, LANES), 1)
    first_chunk = lane < CHUNK
    state = state_ref[...]
    for c in range(ts // CHUNK):
        group = slice((c // 2) * LANES, (c // 2 + 1) * LANES)
        keep = first_chunk if c % 2 == 0 else jnp.logical_not(first_chunk)
        kd_c = jnp.where(keep, kd_t[:, group], jnp.zeros((), BF16))
        t0 = slice(c * CHUNK, c * CHUNK + 1)
        state = (jnp.exp(rest_t[:, t0] + la_t[:, t0]) * state
                 + jnp.dot(kd_c, v_ref[0, group, :], preferred_element_type=F32))
        rows = slice(c * CHUNK, (c + 1) * CHUNK)
        q = (q_ref[0, rows, :].astype(F32) * (dk ** -0.5)).astype(BF16)
        o = jnp.dot(q, state.astype(BF16), preferred_element_type=F32)
        ms = jnp.mean(o * o, axis=-1, keepdims=True)
        o = o * lax.rsqrt(ms + EPS) * og_ref[...]
        o_ref[0, rows, :] = (o * _silu(g_ref[0, rows, :].astype(F32))).astype(o_ref.dtype)
    state_ref[...] = state


def _gla(proj, lr, w_gate_t, gate_bias, out_g, *, ts):
    b, s, n = proj.shape
    heads = B_HEADS
    dk = n // (6 * heads)
    dv = 2 * dk
    kq = (heads * dk) // dk
    kv = (2 * heads * dk) // dv
    span = 2 * LANES
    t_in = jnp.arange(span)[:, None]
    t_out = jnp.arange(span)[None, :]
    chunk_masks = (((t_in // CHUNK) == (t_out // CHUNK)) & (t_in > t_out)).astype(BF16)
    return pl.pallas_call(
        functools.partial(_gla_kernel, ts=ts),
        grid=(b, heads, s // ts),
        in_specs=[
            pl.BlockSpec((1, ts, dk), lambda bi, h, i: (bi, i, h)),
            pl.BlockSpec((1, ts, dk), lambda bi, h, i: (bi, i, kq + h)),
            pl.BlockSpec((1, ts, dv), lambda bi, h, i: (bi, i, kv + h)),
            pl.BlockSpec((1, ts, dv), lambda bi, h, i: (bi, i, kv + heads + h)),
            pl.BlockSpec((1, ts, LANES), lambda bi, h, i: (bi, i, 0)),
            pl.BlockSpec((dk, LANES), lambda bi, h, i: (h, 0)),
            pl.BlockSpec((dk, 1), lambda bi, h, i: (h, 0)),
            pl.BlockSpec((1, dv), lambda bi, h, i: (0, 0)),
            pl.BlockSpec((span, span), lambda bi, h, i: (0, 0)),
        ],
        out_specs=pl.BlockSpec((1, ts, dv), lambda bi, h, i: (bi, i, h)),
        out_shape=jax.ShapeDtypeStruct((b, s, heads * dv), BF16),
        scratch_shapes=[pltpu.VMEM((dk, dv), F32)],
        compiler_params=_cparams("parallel", "parallel", "arbitrary"),
        name="gla",
    )(proj, proj, proj, proj, lr, w_gate_t, gate_bias.reshape(-1, 1).astype(F32),
      out_g.reshape(1, dv).astype(F32), chunk_masks)


def _rglru_kernel(u_ref, g_ref, cw_ref, cb_ref, wr_ref, br_ref, wi_ref, bi_ref,
                  lam_ref, o_ref, ubuf_ref, xc_ref, a_ref, b_ref, h_ref, *, ts):
    width = u_ref.shape[2]
    bd = width // C_BLOCKS

    @pl.when(pl.program_id(1) == 0)
    def _():
        ubuf_ref[0:SUBLANES, :] = jnp.zeros((SUBLANES, width), F32)
        h_ref[...] = jnp.zeros_like(h_ref)

    ubuf_ref[SUBLANES:SUBLANES + ts, :] = u_ref[0]
    xc = cb_ref[...]
    for t in range(C_CONV):
        off = SUBLANES - (C_CONV - 1) + t
        xc = xc + ubuf_ref[off:off + ts, :] * cw_ref[t:t + 1, :]
    xc_ref[...] = xc
    ubuf_ref[0:SUBLANES, :] = ubuf_ref[ts:ts + SUBLANES, :]

    rate = -C_C * _softplus(-lam_ref[...])
    for n in range(C_BLOCKS):
        cols = slice(n * bd, (n + 1) * bd)
        xb = xc_ref[:, cols]
        xb16 = xb.astype(BF16)
        r = _sigmoid(jnp.dot(xb16, wr_ref[n], preferred_element_type=F32)
                     + br_ref[:, cols])
        gate_i = _sigmoid(jnp.dot(xb16, wi_ref[n], preferred_element_type=F32)
                          + bi_ref[:, cols])
        log_a = r * rate[:, cols]
        a = jnp.exp(log_a)
        a_ref[:, cols] = a
        b_ref[:, cols] = jnp.sqrt(_one_minus_exp(2.0 * log_a, a * a)) * (gate_i * xb)

    row = lax.broadcasted_iota(jnp.int32, (SUBLANES, width), 0)

    def tile(t, h_prev):
        rows = pl.ds(pl.multiple_of(t * SUBLANES, SUBLANES), SUBLANES)
        a = a_ref[rows, :]
        b = b_ref[rows, :]
        d = 1
        while d < SUBLANES:
            a_up = pltpu.roll(a, d, 0)
            b_up = pltpu.roll(b, d, 0)
            keep = row >= d
            b = jnp.where(keep, a * b_up + b, b)
            a = jnp.where(keep, a * a_up, a)
            d *= 2
        h = b + a * h_prev
        b_ref[rows, :] = h
        return h[SUBLANES - 1:SUBLANES, :]

    h_ref[...] = lax.fori_loop(0, ts // SUBLANES, tile, h_ref[...])
    o_ref[0] = (b_ref[...] * _silu(g_ref[0].astype(F32))).astype(o_ref.dtype)


def _rglru(u, gate, gate_block, conv_w, conv_b, w_rg, b_rg, w_ig, b_ig, lam, *, ts):
    b, s, width = u.shape
    bd = width // C_BLOCKS
    row = lambda a: a.reshape(1, width).astype(F32)
    const2 = lambda bi, i: (0, 0)
    return pl.pallas_call(
        functools.partial(_rglru_kernel, ts=ts),
        grid=(b, s // ts),
        in_specs=[
            pl.BlockSpec((1, ts, width), lambda bi, i: (bi, i, 0)),
            pl.BlockSpec((1, ts, width), lambda bi, i: (bi, i, gate_block)),
            pl.BlockSpec((C_CONV, width), const2),
            pl.BlockSpec((1, width), const2),
            pl.BlockSpec((C_BLOCKS, bd, bd), lambda bi, i: (0, 0, 0)),
            pl.BlockSpec((1, width), const2),
            pl.BlockSpec((C_BLOCKS, bd, bd), lambda bi, i: (0, 0, 0)),
            pl.BlockSpec((1, width), const2),
            pl.BlockSpec((1, width), const2),
        ],
        out_specs=pl.BlockSpec((1, ts, width), lambda bi, i: (bi, i, 0)),
        out_shape=jax.ShapeDtypeStruct((b, s, width), BF16),
        scratch_shapes=[pltpu.VMEM((ts + SUBLANES, width), F32),
                        pltpu.VMEM((ts, width), F32),
                        pltpu.VMEM((ts, width), F32),
                        pltpu.VMEM((ts, width), F32),
                        pltpu.VMEM((1, width), F32)],
        compiler_params=_cparams("parallel", "arbitrary"),
        name="rglru",
    )(u, gate, conv_w.astype(F32), row(conv_b), w_rg.astype(BF16), row(b_rg),
      w_ig.astype(BF16), row(b_ig), row(lam))


def _swap_rope_halves(x):
    lane = lax.broadcasted_iota(jnp.int32, (1, LANES), 1)
    first = (lane % D_ROPE) < (D_ROPE // 2)
    return jnp.where(first, pltpu.roll(x, LANES - D_ROPE // 2, 1),
                     pltpu.roll(x, D_ROPE // 2, 1))


def _rope(x, cos, sin_signed):
    return x * cos + _swap_rope_halves(x) * sin_signed


def _mla_attn_kernel(qn_ref, qp_ref, kn_ref, kp_ref, v_ref, g_ref, cos_ref,
                     sin_ref, gains_ref, o_ref, kcat_ref, vt_ref, acc_ref, s_ref,
                     *, tq):
    i = pl.program_id(2)
    seq = kn_ref.shape[1]
    lane = lax.broadcasted_iota(jnp.int32, (1, LANES), 1)
    lo_mask = lane < D_ROPE
    g_q_nope, g_q_pe = gains_ref[0:1, :], gains_ref[1:2, :]
    g_k_nope, g_k_pe = gains_ref[2:3, :], gains_ref[3:4, :]

    @pl.when(i == 0)
    def _():
        def body(c, carry):
            rows = pl.ds(pl.multiple_of(c * tq, tq), tq)
            kp = _rope(_group_rmsnorm(kp_ref[0, rows, :].astype(F32), g_k_pe, D_ROPE),
                       cos_ref[rows, :], sin_ref[rows, :])
            kn = kn_ref[0, rows, :].astype(F32)
            kcat_ref[0, rows, 0:LANES] = _group_rmsnorm(
                kn[:, 0:LANES], g_k_nope, D_NOPE).astype(BF16)
            kcat_ref[0, rows, LANES:] = jnp.where(lo_mask, kp, 0.0).astype(BF16)
            kcat_ref[1, rows, 0:LANES] = _group_rmsnorm(
                kn[:, LANES:], g_k_nope, D_NOPE).astype(BF16)
            kcat_ref[1, rows, LANES:] = jnp.where(lo_mask, 0.0, kp).astype(BF16)
            _store_vt_block(vt_ref, (0, c), v_ref[0, rows, 0:D_V])
            _store_vt_block(vt_ref, (1, c), v_ref[0, rows, D_V:])
            return carry
        lax.fori_loop(0, seq // tq, body, 0)

    scale = (D_NOPE + D_ROPE) ** -0.5 * LOG2E
    qrows = pl.ds(pl.multiple_of(i * tq, tq), tq)
    qp = _rope(_group_rmsnorm(qp_ref[0].astype(F32), g_q_pe, D_ROPE),
               cos_ref[qrows, :], sin_ref[qrows, :]) * scale
    qn = qn_ref[0].astype(F32)
    qcat = (
        jnp.concatenate([(_group_rmsnorm(qn[:, 0:LANES], g_q_nope, D_NOPE)
                          * scale).astype(BF16),
                         jnp.where(lo_mask, qp, 0.0).astype(BF16)], axis=-1),
        jnp.concatenate([(_group_rmsnorm(qn[:, LANES:], g_q_nope, D_NOPE)
                          * scale).astype(BF16),
                         jnp.where(lo_mask, 0.0, qp).astype(BF16)], axis=-1),
    )

    def near_add(j):
        c = lax.broadcasted_iota(jnp.int32, (tq, tq), 0) + tq * (j - jnp.minimum(i, 1))
        r = lax.broadcasted_iota(jnp.int32, (tq, tq), 1)
        visible = (lax.shift_right_arithmetic(c, 6) <= lax.shift_right_arithmetic(r, 6))
        return jnp.where(visible, 0.0, NEG_INF).astype(F32)

    def stream(hh):
        def scores(blk):
            rows = pl.ds(pl.multiple_of(blk * tq, tq), tq)
            return _nt_dot(kcat_ref[hh, rows, :], qcat[hh])
        return scores, lambda blk: vt_ref[hh, blk], near_add

    _chunk_causal_flash(i, [stream(0), stream(1)], acc_ref, s_ref)

    g = g_ref[0].astype(F32)
    o_ref[0, :, 0:D_V] = (_normalised(acc_ref[0]).T * _silu(g[:, 0:D_V])).astype(o_ref.dtype)
    o_ref[0, :, D_V:] = (_normalised(acc_ref[1]).T * _silu(g[:, D_V:])).astype(o_ref.dtype)


def _mla_attention(q, kv, kpe, proj, gate_block0, cos, sin_signed, gains):
    b, s, _ = q.shape
    heads = D_HEADS
    pairs = heads // 2
    tq = ATTN_TQ
    pw = 2 * LANES
    return pl.pallas_call(
        functools.partial(_mla_attn_kernel, tq=tq),
        grid=(b, pairs, s // tq),
        in_specs=[
            pl.BlockSpec((1, tq, pw), lambda bi, p, i: (bi, i, p)),
            pl.BlockSpec((1, tq, LANES), lambda bi, p, i: (bi, i, 2 * pairs + p)),
            pl.BlockSpec((1, s, pw), lambda bi, p, i: (bi, 0, p)),
            pl.BlockSpec((1, s, LANES), lambda bi, p, i: (bi, 0, 0)),
            pl.BlockSpec((1, s, pw), lambda bi, p, i: (bi, 0, pairs + p)),
            pl.BlockSpec((1, tq, pw), lambda bi, p, i: (bi, i, gate_block0 + p)),
            pl.BlockSpec((s, LANES), lambda bi, p, i: (0, 0)),
            pl.BlockSpec((s, LANES), lambda bi, p, i: (0, 0)),
            pl.BlockSpec((4, LANES), lambda bi, p, i: (0, 0)),
        ],
        out_specs=pl.BlockSpec((1, tq, pw), lambda bi, p, i: (bi, i, p)),
        out_shape=jax.ShapeDtypeStruct((b, s, heads * D_V), BF16),
        scratch_shapes=[pltpu.VMEM((2, s, pw), BF16),
                        pltpu.VMEM((2, s // tq, V_ROWS, tq), BF16),
                        pltpu.VMEM((2, V_ROWS, tq), F32),
                        pltpu.VMEM((2, 2, tq, tq), F32)],
        compiler_params=_cparams("parallel", "parallel", "arbitrary"),
        name="mla_attention",
    )(q, q, kv, kpe, kv, proj, cos, sin_signed, gains)


MM_TM = 1024
MM_TN = 1024
RES_TM = 512


def _layer_diff(x2, b, s, norm_g, w_in, qk_g, lam_vecs, subln_g, w_out, bias_tiles,
                layer_idx):
    proj = _norm_matmul(x2, norm_g, w_in.astype(BF16), out_dtype=BF16,
                        tm=MM_TM, tn=MM_TN)
    y = _diff_attention(proj.reshape(b, s, -1), lam_vecs, qk_g, subln_g,
                        bias_tiles, layer_idx)
    return _matmul_residual(y.reshape(b * s, -1), w_out.astype(BF16), x2,
                            tm=RES_TM)


def _layer_gla(x2, b, s, norm_g, w_in, w_gate, gate_bias, out_g, w_out):
    rank, hdk = w_gate.shape
    n_main = w_in.shape[1] - rank
    w_main = w_in[:, :n_main].astype(BF16)
    w_lr = jnp.pad(w_in[:, n_main:], ((0, 0), (0, LANES - rank))).astype(BF16)
    w_gate_t = jnp.pad(w_gate, ((0, LANES - rank), (0, 0))).T.astype(BF16)
    proj, lr = _norm_matmul(x2, norm_g, w_main, out_dtype=BF16, tm=MM_TM, tn=MM_TN,
                            w_side=w_lr)
    y = _gla(proj.reshape(b, s, -1), lr.reshape(b, s, LANES), w_gate_t, gate_bias,
             out_g, ts=512)
    return _matmul_residual(y.reshape(b * s, -1), w_out.astype(BF16), x2,
                            tm=RES_TM)


def _layer_rglru(x2, b, s, norm_g, w_in, conv_w, conv_b, w_rg, b_rg, w_ig, b_ig,
                 lam, w_out):
    width = w_in.shape[1] // 2
    u, gate = _norm_matmul(x2, norm_g, w_in.astype(BF16), out_dtype=BF16,
                           tm=MM_TM, tn=MM_TN, first=(width, F32))
    y = _rglru(u.reshape(b, s, width), gate.reshape(b, s, width), 0, conv_w, conv_b,
               w_rg, b_rg, w_ig, b_ig, lam, ts=256)
    return _matmul_residual(y.reshape(b * s, -1), w_out.astype(BF16), x2,
                            tm=RES_TM)


def _rope_tables(s):
    half = D_ROPE // 2
    inv = ROPE_THETA ** (-jnp.arange(half, dtype=F32) / half)
    ang = jnp.arange(s, dtype=F32)[:, None] * inv[None, :]
    cos, sin = jnp.cos(ang), jnp.sin(ang)
    return (jnp.concatenate([cos, cos, cos, cos], axis=-1),
            jnp.concatenate([-sin, sin, -sin, sin], axis=-1))


def _layer_mla(x2, b, s, norm_g, w_in, q_lat_g, kv_lat_g, w_uq, w_ukv, qk_g, w_out):
    q_rank, kv_rank = q_lat_g.shape[0], kv_lat_g.shape[0]
    heads = D_HEADS
    lat = q_rank + kv_rank
    w_main = jnp.concatenate([w_in[:, :lat], w_in[:, lat + D_ROPE:]], axis=1).astype(BF16)
    w_kpe = w_in[:, lat:lat + D_ROPE]
    w_kpe = jnp.concatenate([w_kpe, w_kpe], axis=1).astype(BF16)
    uq = w_uq.reshape(q_rank, heads, D_NOPE + D_ROPE)
    w_uq_p = jnp.concatenate([uq[:, :, :D_NOPE].reshape(q_rank, -1),
                              uq[:, :, D_NOPE:].reshape(q_rank, -1)], axis=1).astype(BF16)
    ukv = w_ukv.reshape(kv_rank, heads, D_NOPE + D_V)
    w_ukv_p = jnp.concatenate([ukv[:, :, :D_NOPE].reshape(kv_rank, -1),
                               ukv[:, :, D_NOPE:].reshape(kv_rank, -1)], axis=1).astype(BF16)
    dup = lambda v: jnp.concatenate([v, v])
    gains = jnp.stack([qk_g[0, :D_NOPE], dup(qk_g[0, D_NOPE:]),
                       qk_g[1, :D_NOPE], dup(qk_g[1, D_NOPE:])]).astype(F32)
    cos, sin_signed = _rope_tables(s)

    proj, kpe = _norm_matmul(x2, norm_g, w_main, out_dtype=BF16, tm=MM_TM, tn=MM_TN,
                             w_side=w_kpe)
    q = _norm_matmul(proj, q_lat_g, w_uq_p, out_dtype=BF16, tm=MM_TM, tn=MM_TN,
                     col_block=0)
    kv = _norm_matmul(proj, kv_lat_g, w_ukv_p, out_dtype=BF16, tm=MM_TM, tn=MM_TN,
                      col_block=1)
    gate_block0 = lat // (2 * LANES)
    y = _mla_attention(q.reshape(b, s, -1), kv.reshape(b, s, -1),
                       kpe.reshape(b, s, LANES), proj.reshape(b, s, -1),
                       gate_block0, cos, sin_signed, gains)
    return _matmul_residual(y.reshape(b * s, -1), w_out.astype(BF16), x2,
                            tm=RES_TM)


def kernel(x, norm_g, rel_bias, a_w_in, a_qk_g, a_lambda, a_subln_g, a_w_out, b_w_in, b_w_gate, b_gate_bias, b_out_g, b_w_out, c_w_in, c_conv_w, c_conv_b, c_w_rgate, c_b_rgate, c_w_igate, c_b_igate, c_lambda, c_w_out, d_w_in, d_q_lat_g, d_kv_lat_g, d_w_uq, d_w_ukv, d_qk_g, d_w_out):
    b, s, d = x.shape
    depth = norm_g.shape[0]
    x2 = x.reshape(b * s, d)
    bias_tiles = _t5_tiles(rel_bias, ATTN_TQ)
    for i in range(depth):
        m, j = i % 4, i // 4
        if m == 0:
            x2 = _layer_diff(x2, b, s, norm_g[i], a_w_in[j], a_qk_g[j], a_lambda[j],
                             a_subln_g[j], a_w_out[j], bias_tiles, i)
        elif m == 1:
            x2 = _layer_gla(x2, b, s, norm_g[i], b_w_in[j], b_w_gate[j],
                            b_gate_bias[j], b_out_g[j], b_w_out[j])
        elif m == 2:
            x2 = _layer_rglru(x2, b, s, norm_g[i], c_w_in[j], c_conv_w[j], c_conv_b[j],
                              c_w_rgate[j], c_b_rgate[j], c_w_igate[j], c_b_igate[j],
                              c_lambda[j], c_w_out[j])
        else:
            x2 = _layer_mla(x2, b, s, norm_g[i], d_w_in[j], d_q_lat_g[j],
                            d_kv_lat_g[j], d_w_uq[j], d_w_ukv[j], d_qk_g[j], d_w_out[j])
    return x2.reshape(b, s, d)
```

```python
import functools
import math

import jax
import jax.numpy as jnp
from jax import lax
from jax.experimental import pallas as pl
from jax.experimental.pallas import tpu as pltpu

F32 = jnp.float32
BF16 = jnp.bfloat16

EPS = 1e-6
NEG_INF = -1e30
LOG2E = math.log2(math.e)
CHUNK = 64
LANES = 128
SUBLANES = 8
BF16_ROWS = 16
V7X_VMEM_BYTES = 64 * 1024 * 1024
VMEM_LIMIT = V7X_VMEM_BYTES - 8 * 1024 * 1024

A_HEADS = 16
A_HEAD_DIM = 64
T5_BUCKETS = 32
B_HEADS = 4
B_GATE_TAU = 16.0
C_BLOCKS = 8
C_CONV = 4
C_C = 8.0
D_HEADS = 16
D_NOPE = 128
D_ROPE = 64
D_V = 128
ROPE_THETA = 10000.0

ATTN_TQ = 512
V_ROWS = LANES + BF16_ROWS
FAR_UNROLL = 4


def _cparams(*semantics):
    return pltpu.CompilerParams(dimension_semantics=semantics,
                                vmem_limit_bytes=VMEM_LIMIT)


def _sigmoid(x):
    return 0.5 * jnp.tanh(0.5 * x) + 0.5


def _silu(g):
    return g * _sigmoid(g)


def _softplus(x):
    return jnp.maximum(x, 0.0) + jnp.log(1.0 + jnp.exp(-jnp.abs(x)))


_EXPM1_SERIES_BOUND = 0.1
_EXPM1_SERIES_TERMS = 5


def _one_minus_exp(x, exp_x):
    poly = jnp.full_like(x, 1.0 / math.factorial(_EXPM1_SERIES_TERMS))
    for n in range(_EXPM1_SERIES_TERMS - 1, 0, -1):
        poly = poly * x + 1.0 / math.factorial(n)
    return jnp.where(x > -_EXPM1_SERIES_BOUND, -x * poly, 1.0 - exp_x)


def _nt_dot(a, b):
    return lax.dot_general(a, b, (((1,), (1,)), ((), ())),
                           preferred_element_type=F32)


def _norm_matmul_kernel(*refs, has_side, n_first):
    refs = list(refs)
    x_ref, g_ref, w_ref = refs[:3]
    del refs[:3]
    ws_ref = refs.pop(0) if has_side else None
    o_ref = refs.pop(0)
    rest_ref = refs.pop(0) if n_first is not None else None
    side_ref = refs.pop(0) if has_side else None
    h_ref = refs.pop(0)
    j = pl.program_id(1)

    @pl.when(j == 0)
    def _():
        x = x_ref[...].astype(F32)
        ms = jnp.mean(x * x, axis=-1, keepdims=True)
        h_ref[...] = (x * lax.rsqrt(ms + EPS) * g_ref[...]).astype(BF16)
        if has_side:
            side_ref[...] = jnp.dot(h_ref[...], ws_ref[...],
                                    preferred_element_type=F32).astype(side_ref.dtype)

    res = jnp.dot(h_ref[...], w_ref[...], preferred_element_type=F32)
    if n_first is None:
        o_ref[...] = res.astype(o_ref.dtype)
    else:
        @pl.when(j < n_first)
        def _():
            o_ref[...] = res.astype(o_ref.dtype)

        @pl.when(j >= n_first)
        def _():
            rest_ref[...] = res.astype(rest_ref.dtype)


def _norm_matmul(x, gain, w, *, out_dtype, tm, tn, col_block=0, n_cols=None,
                 side_block=None, first=None):
    m = x.shape[0]
    k = w.shape[0]
    n = w.shape[1] if n_cols is None else n_cols
    n_first = None if first is None else first[0] // tn
    in_specs = [pl.BlockSpec((tm, k), lambda i, j: (i, col_block)),
                pl.BlockSpec((1, k), lambda i, j: (0, 0)),
                pl.BlockSpec((k, tn), lambda i, j: (0, j))]
    args = [x, gain.reshape(1, k).astype(F32), w]
    if side_block is not None:
        in_specs.append(pl.BlockSpec((k, LANES), lambda i, j: (0, side_block)))
        args.append(w)
    if first is None:
        out_specs = [pl.BlockSpec((tm, tn), lambda i, j: (i, j))]
        out_shape = [jax.ShapeDtypeStruct((m, n), out_dtype)]
    else:
        out_specs = [pl.BlockSpec((tm, tn), lambda i, j: (i, jnp.minimum(j, n_first - 1))),
                     pl.BlockSpec((tm, tn), lambda i, j: (i, jnp.maximum(j - n_first, 0)))]
        out_shape = [jax.ShapeDtypeStruct((m, first[0]), first[1]),
                     jax.ShapeDtypeStruct((m, n - first[0]), out_dtype)]
    if side_block is not None:
        out_specs.append(pl.BlockSpec((tm, LANES), lambda i, j: (i, 0)))
        out_shape.append(jax.ShapeDtypeStruct((m, LANES), BF16))
    outs = pl.pallas_call(
        functools.partial(_norm_matmul_kernel, has_side=side_block is not None,
                          n_first=n_first),
        grid=(m // tm, n // tn),
        in_specs=in_specs,
        out_specs=out_specs,
        out_shape=out_shape,
        scratch_shapes=[pltpu.VMEM((tm, k), BF16)],
        compiler_params=_cparams("parallel", "arbitrary"),
        name="norm_matmul",
    )(*args)
    return outs[0] if len(outs) == 1 else tuple(outs)


def _matmul_residual_kernel(y_ref, w_ref, x_ref, o_ref):
    o_ref[...] = x_ref[...] + jnp.dot(y_ref[...], w_ref[...],
                                      preferred_element_type=F32)


def _matmul_residual(y, w, x, *, tm):
    m, k = y.shape
    n = w.shape[1]
    return pl.pallas_call(
        _matmul_residual_kernel,
        grid=(m // tm,),
        in_specs=[pl.BlockSpec((tm, k), lambda i: (i, 0)),
                  pl.BlockSpec((k, n), lambda i: (0, 0)),
                  pl.BlockSpec((tm, n), lambda i: (i, 0))],
        out_specs=pl.BlockSpec((tm, n), lambda i: (i, 0)),
        out_shape=jax.ShapeDtypeStruct((m, n), F32),
        compiler_params=_cparams("parallel"),
        name="matmul_residual",
    )(y, w, x)


def _group_rmsnorm(x, gain, group):
    r = lax.broadcasted_iota(jnp.int32, (LANES, LANES), 0)
    c = lax.broadcasted_iota(jnp.int32, (LANES, LANES), 1)
    ones = jnp.where(r // group == c // group, 1.0, 0.0).astype(BF16)
    x2 = x * x
    hi = x2.astype(BF16)
    lo = (x2 - hi.astype(F32)).astype(BF16)
    ss = (jnp.dot(hi, ones, preferred_element_type=F32)
          + jnp.dot(lo, ones, preferred_element_type=F32))
    return x * lax.rsqrt(ss * (1.0 / group) + EPS) * gain


def _half_rmsnorm(x, gain, lo_mask):
    x2 = x * x
    lo = jnp.sum(jnp.where(lo_mask, x2, 0.0), axis=-1, keepdims=True)
    hi = jnp.sum(jnp.where(lo_mask, 0.0, x2), axis=-1, keepdims=True)
    ms = jnp.where(lo_mask, lo, hi) * (1.0 / 64)
    return x * lax.rsqrt(ms + EPS) * gain


def _store_vt_block(vt_ref, lead, v_rows):
    tq = v_rows.shape[0]
    r = lax.broadcasted_iota(jnp.int32, (LANES, LANES), 0)
    c = lax.broadcasted_iota(jnp.int32, (LANES, LANES), 1)
    eye = jnp.where(r == c, 1.0, 0.0).astype(BF16)
    vt_ref[(*lead, slice(0, LANES), slice(None))] = _nt_dot(eye, v_rows).astype(BF16)
    vt_ref[(*lead, slice(LANES, V_ROWS), slice(None))] = jnp.ones((BF16_ROWS, tq), BF16)


def _flash_init(s, vt):
    m = jnp.max(s, axis=0, keepdims=True)
    p = jnp.exp2(s - m).astype(BF16)
    return m, jnp.dot(vt, p, preferred_element_type=F32)


def _flash_update(s, vt, m, acc):
    m_new = jnp.maximum(m, jnp.max(s, axis=0, keepdims=True))
    alpha = jnp.exp2(m - m_new)
    p = jnp.exp2(s - m_new).astype(BF16)
    return m_new, alpha * acc + jnp.dot(vt, p, preferred_element_type=F32)


def _chunk_causal_flash(i, streams, acc_ref, s_ref):
    first = jnp.maximum(i - 1, 0)
    n_far = first

    def issue(blk, slot, add=None):
        for si, (scores, _, near_add) in enumerate(streams):
            s = scores(blk)
            s_ref[si, slot] = s if add is None else s + near_add(add)

    def update(blk, slot, ms):
        out = []
        for si, (_, values, _) in enumerate(streams):
            m, acc_ref[si] = _flash_update(s_ref[si, slot], values(blk), ms[si],
                                           acc_ref[si])
            out.append(m)
        return tuple(out)

    def group(t0, count, prefetch_next, ms):
        for k in range(count):
            if k + 1 < count or prefetch_next:
                issue(t0 + k + 1, (k + 1) % 2)
            ms = update(t0 + k, k % 2, ms)
        return ms

    issue(first, 0, add=0)
    issue(first + 1, 1, add=1)
    ms = []
    for si, (_, values, _) in enumerate(streams):
        m, acc_ref[si] = _flash_init(s_ref[si, 0], values(first))
        ms.append(m)
    issue(0, 0)
    ms = update(first + 1, 1, tuple(ms))

    full = jnp.maximum(n_far - 1, 0) // FAR_UNROLL
    rest = n_far - FAR_UNROLL * full
    ms = lax.fori_loop(0, full,
                       lambda g, ms: group(g * FAR_UNROLL, FAR_UNROLL, True, ms), ms)
    for count in range(1, FAR_UNROLL + 1):
        ms = lax.fori_loop(0, (rest == count).astype(jnp.int32),
                           lambda _, ms, count=count: group(n_far - count, count, False, ms),
                           ms)


def _normalised(acc):
    return acc[0:LANES] * (1.0 / acc[LANES:LANES + 1])


_T5_LARGE_THRESHOLDS = (12, 16, 23, 32, 46, 64, 91)


def _t5_tiles_kernel(rb_ref, o_ref, *, tq):
    h = pl.program_id(0)
    variant = pl.program_id(1)
    j = lax.broadcasted_iota(jnp.int32, (1, tq), 1)
    rel_row = jnp.where(j < tq // 2, -j, tq - j)
    n = jnp.abs(rel_row)
    large = jnp.full_like(n, 8)
    for t in _T5_LARGE_THRESHOLDS:
        large = large + (n >= t).astype(jnp.int32)
    bucket = jnp.where(rel_row > 0, T5_BUCKETS // 2, 0) + jnp.where(n < 8, n, large)
    far = rb_ref[T5_BUCKETS // 2 - 1, h]
    row = jnp.zeros((1, tq), F32)
    for b in range(T5_BUCKETS):
        row = jnp.where(bucket == b, (rb_ref[b, h] - far) * LOG2E, row)
    tile = pltpu.roll(jnp.broadcast_to(row, (2 * tq, tq)), 0, 1, stride=1, stride_axis=0)
    c = lax.broadcasted_iota(jnp.int32, (2 * tq, tq), 0)
    r = lax.broadcasted_iota(jnp.int32, (2 * tq, tq), 1)
    key_off = c - tq * variant
    bias = jnp.where(key_off - r > -_T5_LARGE_THRESHOLDS[-1], tile, 0.0)
    visible = (lax.shift_right_arithmetic(key_off, 6)
               <= lax.shift_right_arithmetic(r, 6))
    o_ref[0, 0] = jnp.where(visible, bias, NEG_INF)


def _t5_tiles(rel_bias, tq):
    nb, nh = rel_bias.shape
    return pl.pallas_call(
        functools.partial(_t5_tiles_kernel, tq=tq),
        grid=(nh, 2),
        in_specs=[pl.BlockSpec(memory_space=pltpu.SMEM)],
        out_specs=pl.BlockSpec((1, 1, 2 * tq, tq), lambda h, v: (h, v, 0, 0)),
        out_shape=jax.ShapeDtypeStruct((nh, 2, 2 * tq, tq), F32),
        compiler_params=_cparams("parallel", "arbitrary"),
        name="t5_tiles",
    )(rel_bias.astype(F32))


def _diff_attn_kernel(lamv_ref, q_ref, k_ref, v_ref, g_ref, bias_ref, qkg_ref,
                      sub_ref, o_ref, kn_ref, vt_ref, acc_ref, s_ref, *, tq, lam_init):
    i = pl.program_id(2)
    seq = k_ref.shape[1]
    lane = lax.broadcasted_iota(jnp.int32, (1, LANES), 1)
    lo_mask = lane < A_HEAD_DIM

    @pl.when(i == 0)
    def _():
        def body(c, carry):
            rows = pl.ds(pl.multiple_of(c * tq, tq), tq)
            kn_ref[rows, :] = _group_rmsnorm(
                k_ref[0, rows, :].astype(F32), qkg_ref[1:2, :], A_HEAD_DIM).astype(BF16)
            _store_vt_block(vt_ref, (c,), v_ref[0, rows, :])
            return carry
        lax.fori_loop(0, seq // tq, body, 0)

    lf = lamv_ref[...].astype(F32)
    lam = (jnp.exp(jnp.sum(lf[0:1] * lf[1:2], axis=-1, keepdims=True))
           - jnp.exp(jnp.sum(lf[2:3] * lf[3:4], axis=-1, keepdims=True)) + lam_init)

    qn = (_half_rmsnorm(q_ref[0].astype(F32), qkg_ref[0:1, :], lo_mask)
          * (A_HEAD_DIM ** -0.5 * LOG2E))
    q01 = jnp.concatenate([jnp.where(lo_mask, qn, 0.0).astype(BF16),
                           jnp.where(lo_mask, 0.0, qn).astype(BF16)], axis=0)

    def scores(blk):
        return _nt_dot(kn_ref[pl.ds(pl.multiple_of(blk * tq, tq), tq), :], q01)

    def near_add(j):
        bias_t = bias_ref[0, 0, j * tq:(j + 1) * tq, :]
        return jnp.concatenate([bias_t, bias_t], axis=1)

    _chunk_causal_flash(i, [(scores, lambda blk: vt_ref[blk], near_add)], acc_ref,
                        s_ref)

    o01 = _normalised(acc_ref[0])
    o_t = o01[:, 0:tq] - lam * o01[:, tq:]
    ms_t = jnp.mean(o_t * o_t, axis=0, keepdims=True)
    o_t = o_t * lax.rsqrt(ms_t + EPS) * sub_ref[...] * (1.0 - lam_init)
    o_ref[0] = (o_t.T * _silu(g_ref[0].astype(F32))).astype(o_ref.dtype)


def _diff_attention(proj, lam_vecs, qk_g, subln_g, bias_tiles, layer_idx):
    b, s, four_w = proj.shape
    width = four_w // 4
    hblocks = width // LANES
    tq = ATTN_TQ
    lam_init = 0.8 - 0.6 * math.exp(-0.3 * layer_idx)
    qkg = jnp.concatenate([qk_g, qk_g], axis=-1).astype(F32)
    return pl.pallas_call(
        functools.partial(_diff_attn_kernel, tq=tq, lam_init=lam_init),
        grid=(b, hblocks, s // tq),
        in_specs=[
            pl.BlockSpec((4, A_HEAD_DIM), lambda bi, h, i: (0, 0)),
            pl.BlockSpec((1, tq, LANES), lambda bi, h, i: (bi, i, h)),
            pl.BlockSpec((1, s, LANES), lambda bi, h, i: (bi, 0, hblocks + h)),
            pl.BlockSpec((1, s, LANES), lambda bi, h, i: (bi, 0, 2 * hblocks + h)),
            pl.BlockSpec((1, tq, LANES), lambda bi, h, i: (bi, i, 3 * hblocks + h)),
            pl.BlockSpec((1, 1, 2 * tq, tq),
                         lambda bi, h, i: (h, jnp.minimum(i, 1), 0, 0)),
            pl.BlockSpec((2, LANES), lambda bi, h, i: (0, 0)),
            pl.BlockSpec((LANES, 1), lambda bi, h, i: (0, 0)),
        ],
        out_specs=pl.BlockSpec((1, tq, LANES), lambda bi, h, i: (bi, i, h)),
        out_shape=jax.ShapeDtypeStruct((b, s, width), BF16),
        scratch_shapes=[pltpu.VMEM((s, LANES), BF16),
                        pltpu.VMEM((s // tq, V_ROWS, tq), BF16),
                        pltpu.VMEM((1, V_ROWS, 2 * tq), F32),
                        pltpu.VMEM((1, 2, tq, 2 * tq), F32)],
        compiler_params=_cparams("parallel", "parallel", "arbitrary"),
        name="diff_attention",
    )(lam_vecs.astype(F32), proj, proj, proj, proj, bias_tiles, qkg,
      subln_g.reshape(LANES, 1).astype(F32))


def _gla_kernel(q_ref, k_ref, v_ref, g_ref, lr_ref, wgt_ref, gb_ref, og_ref,
                cm_ref, o_ref, state_ref, *, ts):
    @pl.when(pl.program_id(2) == 0)
    def _():
        state_ref[...] = jnp.zeros_like(state_ref)

    dk = q_ref.shape[2]
    z_t = _nt_dot(wgt_ref[...], lr_ref[0]) + gb_ref[...]
    la_t = -_softplus(-z_t) * (1.0 / B_GATE_TAU)
    la_hi = la_t.astype(BF16)
    la_lo = (la_t - la_hi.astype(F32)).astype(BF16)

    span = cm_ref.shape[0]
    rest_t = jnp.concatenate(
        [jnp.dot(la_hi[:, s0:s0 + span], cm_ref[...], preferred_element_type=F32)
         + jnp.dot(la_lo[:, s0:s0 + span], cm_ref[...], preferred_element_type=F32)
         for s0 in range(0, ts, span)], axis=1)
    kd_t = (k_ref[0].astype(F32).T * jnp.exp(rest_t)).astype(BF16)

    lane = lax.broadcasted_iota(jnp.int32, (1, LANES), 1)
    first_chunk = lane < CHUNK
    state = state_ref[...]
    for c in range(ts // CHUNK):
        group = slice((c // 2) * LANES, (c // 2 + 1) * LANES)
        keep = first_chunk if c % 2 == 0 else jnp.logical_not(first_chunk)
        kd_c = jnp.where(keep, kd_t[:, group], jnp.zeros((), BF16))
        t0 = slice(c * CHUNK, c * CHUNK + 1)
        state = (jnp.exp(rest_t[:, t0] + la_t[:, t0]) * state
                 + jnp.dot(kd_c, v_ref[0, group, :], preferred_element_type=F32))
        rows = slice(c * CHUNK, (c + 1) * CHUNK)
        q = (q_ref[0, rows, :].astype(F32) * (dk ** -0.5)).astype(BF16)
        o = jnp.dot(q, state.astype(BF16), preferred_element_type=F32)
        ms = jnp.mean(o * o, axis=-1, keepdims=True)
        o = o * lax.rsqrt(ms + EPS) * og_ref[...]
        o_ref[0, rows, :] = (o * _silu(g_ref[0, rows, :].astype(F32))).astype(o_ref.dtype)
    state_ref[...] = state


def _gla(proj, lr, w_gate_t, gate_bias, out_g, *, ts):
    b, s, n = proj.shape
    heads = B_HEADS
    dk = n // (6 * heads)
    dv = 2 * dk
    kq = (heads * dk) // dk
    kv = (2 * heads * dk) // dv
    span = 2 * LANES
    t_in = jnp.arange(span)[:, None]
    t_out = jnp.arange(span)[None, :]
    chunk_masks = (((t_in // CHUNK) == (t_out // CHUNK)) & (t_in > t_out)).astype(BF16)
    return pl.pallas_call(
        functools.partial(_gla_kernel, ts=ts),
        grid=(b, heads, s // ts),
        in_specs=[
            pl.BlockSpec((1, ts, dk), lambda bi, h, i: (bi, i, h)),
            pl.BlockSpec((1, ts, dk), lambda bi, h, i: (bi, i, kq + h)),
            pl.BlockSpec((1, ts, dv), lambda bi, h, i: (bi, i, kv + h)),
            pl.BlockSpec((1, ts, dv), lambda bi, h, i: (bi, i, kv + heads + h)),
            pl.BlockSpec((1, ts, LANES), lambda bi, h, i: (bi, i, 0)),
            pl.BlockSpec((dk, LANES), lambda bi, h, i: (h, 0)),
            pl.BlockSpec((dk, 1), lambda bi, h, i: (h, 0)),
            pl.BlockSpec((1, dv), lambda bi, h, i: (0, 0)),
            pl.BlockSpec((span, span), lambda bi, h, i: (0, 0)),
        ],
        out_specs=pl.BlockSpec((1, ts, dv), lambda bi, h, i: (bi, i, h)),
        out_shape=jax.ShapeDtypeStruct((b, s, heads * dv), BF16),
        scratch_shapes=[pltpu.VMEM((dk, dv), F32)],
        compiler_params=_cparams("parallel", "parallel", "arbitrary"),
        name="gla",
    )(proj, proj, proj, proj, lr, w_gate_t, gate_bias.reshape(-1, 1).astype(F32),
      out_g.reshape(1, dv).astype(F32), chunk_masks)


def _rglru_kernel(u_ref, g_ref, cw_ref, cb_ref, wr_ref, br_ref, wi_ref, bi_ref,
                  lam_ref, o_ref, ubuf_ref, xc_ref, a_ref, b_ref, h_ref, *, ts):
    width = u_ref.shape[2]
    bd = width // C_BLOCKS

    @pl.when(pl.program_id(1) == 0)
    def _():
        ubuf_ref[0:SUBLANES, :] = jnp.zeros((SUBLANES, width), F32)
        h_ref[...] = jnp.zeros_like(h_ref)

    ubuf_ref[SUBLANES:SUBLANES + ts, :] = u_ref[0]
    xc = cb_ref[...]
    for t in range(C_CONV):
        off = SUBLANES - (C_CONV - 1) + t
        xc = xc + ubuf_ref[off:off + ts, :] * cw_ref[t:t + 1, :]
    xc_ref[...] = xc
    ubuf_ref[0:SUBLANES, :] = ubuf_ref[ts:ts + SUBLANES, :]

    rate = -C_C * _softplus(-lam_ref[...])
    for n in range(C_BLOCKS):
        cols = slice(n * bd, (n + 1) * bd)
        xb = xc_ref[:, cols]
        xb16 = xb.astype(BF16)
        r = _sigmoid(jnp.dot(xb16, wr_ref[n], preferred_element_type=F32)
                     + br_ref[:, cols])
        gate_i = _sigmoid(jnp.dot(xb16, wi_ref[n], preferred_element_type=F32)
                          + bi_ref[:, cols])
        log_a = r * rate[:, cols]
        a = jnp.exp(log_a)
        a_ref[:, cols] = a
        b_ref[:, cols] = jnp.sqrt(_one_minus_exp(2.0 * log_a, a * a)) * (gate_i * xb)

    row = lax.broadcasted_iota(jnp.int32, (SUBLANES, width), 0)

    def tile(t, h_prev):
        rows = pl.ds(pl.multiple_of(t * SUBLANES, SUBLANES), SUBLANES)
        a = a_ref[rows, :]
        b = b_ref[rows, :]
        d = 1
        while d < SUBLANES:
            a_up = pltpu.roll(a, d, 0)
            b_up = pltpu.roll(b, d, 0)
            keep = row >= d
            b = jnp.where(keep, a * b_up + b, b)
            a = jnp.where(keep, a * a_up, a)
            d *= 2
        h = b + a * h_prev
        b_ref[rows, :] = h
        return h[SUBLANES - 1:SUBLANES, :]

    h_ref[...] = lax.fori_loop(0, ts // SUBLANES, tile, h_ref[...])
    o_ref[0] = (b_ref[...] * _silu(g_ref[0].astype(F32))).astype(o_ref.dtype)


def _rglru(u, gate, gate_block, conv_w, conv_b, w_rg, b_rg, w_ig, b_ig, lam, *, ts):
    b, s, width = u.shape
    bd = width // C_BLOCKS
    row = lambda a: a.reshape(1, width).astype(F32)
    const2 = lambda bi, i: (0, 0)
    return pl.pallas_call(
        functools.partial(_rglru_kernel, ts=ts),
        grid=(b, s // ts),
        in_specs=[
            pl.BlockSpec((1, ts, width), lambda bi, i: (bi, i, 0)),
            pl.BlockSpec((1, ts, width), lambda bi, i: (bi, i, gate_block)),
            pl.BlockSpec((C_CONV, width), const2),
            pl.BlockSpec((1, width), const2),
            pl.BlockSpec((C_BLOCKS, bd, bd), lambda bi, i: (0, 0, 0)),
            pl.BlockSpec((1, width), const2),
            pl.BlockSpec((C_BLOCKS, bd, bd), lambda bi, i: (0, 0, 0)),
            pl.BlockSpec((1, width), const2),
            pl.BlockSpec((1, width), const2),
        ],
        out_specs=pl.BlockSpec((1, ts, width), lambda bi, i: (bi, i, 0)),
        out_shape=jax.ShapeDtypeStruct((b, s, width), BF16),
        scratch_shapes=[pltpu.VMEM((ts + SUBLANES, width), F32),
                        pltpu.VMEM((ts, width), F32),
                        pltpu.VMEM((ts, width), F32),
                        pltpu.VMEM((ts, width), F32),
                        pltpu.VMEM((1, width), F32)],
        compiler_params=_cparams("parallel", "arbitrary"),
        name="rglru",
    )(u, gate, conv_w.astype(F32), row(conv_b), w_rg.astype(BF16), row(b_rg),
      w_ig.astype(BF16), row(b_ig), row(lam))


def _swap_rope_halves(x):
    lane = lax.broadcasted_iota(jnp.int32, (1, LANES), 1)
    first = (lane % D_ROPE) < (D_ROPE // 2)
    return jnp.where(first, pltpu.roll(x, LANES - D_ROPE // 2, 1),
                     pltpu.roll(x, D_ROPE // 2, 1))


def _rope(x, cos, sin_signed):
    return x * cos + _swap_rope_halves(x) * sin_signed


def _mla_attn_kernel(qn_ref, qp_ref, kn_ref, kp_ref, v_ref, g_ref, cos_ref,
                     sin_ref, gains_ref, o_ref, kcat_ref, vt_ref, acc_ref, s_ref,
                     *, tq):
    i = pl.program_id(2)
    seq = kn_ref.shape[1]
    lane = lax.broadcasted_iota(jnp.int32, (1, LANES), 1)
    lo_mask = lane < D_ROPE
    g_q_nope, g_q_pe = gains_ref[0:1, :], gains_ref[1:2, :]
    g_k_nope, g_k_pe = gains_ref[2:3, :], gains_ref[3:4, :]

    @pl.when(i == 0)
    def _():
        def body(c, carry):
            rows = pl.ds(pl.multiple_of(c * tq, tq), tq)
            kp = _rope(_group_rmsnorm(kp_ref[0, rows, :].astype(F32), g_k_pe, D_ROPE),
                       cos_ref[rows, :], sin_ref[rows, :])
            kn = kn_ref[0, rows, :].astype(F32)
            kcat_ref[0, rows, 0:LANES] = _group_rmsnorm(
                kn[:, 0:LANES], g_k_nope, D_NOPE).astype(BF16)
            kcat_ref[0, rows, LANES:] = jnp.where(lo_mask, kp, 0.0).astype(BF16)
            kcat_ref[1, rows, 0:LANES] = _group_rmsnorm(
                kn[:, LANES:], g_k_nope, D_NOPE).astype(BF16)
            kcat_ref[1, rows, LANES:] = jnp.where(lo_mask, 0.0, kp).astype(BF16)
            _store_vt_block(vt_ref, (0, c), v_ref[0, rows, 0:D_V])
            _store_vt_block(vt_ref, (1, c), v_ref[0, rows, D_V:])
            return carry
        lax.fori_loop(0, seq // tq, body, 0)

    scale = (D_NOPE + D_ROPE) ** -0.5 * LOG2E
    qrows = pl.ds(pl.multiple_of(i * tq, tq), tq)
    qp = _rope(_group_rmsnorm(qp_ref[0].astype(F32), g_q_pe, D_ROPE),
               cos_ref[qrows, :], sin_ref[qrows, :]) * scale
    qn = qn_ref[0].astype(F32)
    qcat = (
        jnp.concatenate([(_group_rmsnorm(qn[:, 0:LANES], g_q_nope, D_NOPE)
                          * scale).astype(BF16),
                         jnp.where(lo_mask, qp, 0.0).astype(BF16)], axis=-1),
        jnp.concatenate([(_group_rmsnorm(qn[:, LANES:], g_q_nope, D_NOPE)
                          * scale).astype(BF16),
                         jnp.where(lo_mask, 0.0, qp).astype(BF16)], axis=-1),
    )

    def near_add(j):
        c = lax.broadcasted_iota(jnp.int32, (tq, tq), 0) + tq * (j - jnp.minimum(i, 1))
        r = lax.broadcasted_iota(jnp.int32, (tq, tq), 1)
        visible = (lax.shift_right_arithmetic(c, 6) <= lax.shift_right_arithmetic(r, 6))
        return jnp.where(visible, 0.0, NEG_INF).astype(F32)

    def stream(hh):
        def scores(blk):
            rows = pl.ds(pl.multiple_of(blk * tq, tq), tq)
            return _nt_dot(kcat_ref[hh, rows, :], qcat[hh])
        return scores, lambda blk: vt_ref[hh, blk], near_add

    _chunk_causal_flash(i, [stream(0), stream(1)], acc_ref, s_ref)

    g = g_ref[0].astype(F32)
    o_ref[0, :, 0:D_V] = (_normalised(acc_ref[0]).T * _silu(g[:, 0:D_V])).astype(o_ref.dtype)
    o_ref[0, :, D_V:] = (_normalised(acc_ref[1]).T * _silu(g[:, D_V:])).astype(o_ref.dtype)


def _mla_attention(q, kv, kpe, proj, gate_block0, cos, sin_signed, gains):
    b, s, _ = q.shape
    heads = D_HEADS
    pairs = heads // 2
    tq = ATTN_TQ
    pw = 2 * LANES
    return pl.pallas_call(
        functools.partial(_mla_attn_kernel, tq=tq),
        grid=(b, pairs, s // tq),
        in_specs=[
            pl.BlockSpec((1, tq, pw), lambda bi, p, i: (bi, i, p)),
            pl.BlockSpec((1, tq, LANES), lambda bi, p, i: (bi, i, 2 * pairs + p)),
            pl.BlockSpec((1, s, pw), lambda bi, p, i: (bi, 0, p)),
            pl.BlockSpec((1, s, LANES), lambda bi, p, i: (bi, 0, 0)),
            pl.BlockSpec((1, s, pw), lambda bi, p, i: (bi, 0, pairs + p)),
            pl.BlockSpec((1, tq, pw), lambda bi, p, i: (bi, i, gate_block0 + p)),
            pl.BlockSpec((s, LANES), lambda bi, p, i: (0, 0)),
            pl.BlockSpec((s, LANES), lambda bi, p, i: (0, 0)),
            pl.BlockSpec((4, LANES), lambda bi, p, i: (0, 0)),
        ],
        out_specs=pl.BlockSpec((1, tq, pw), lambda bi, p, i: (bi, i, p)),
        out_shape=jax.ShapeDtypeStruct((b, s, heads * D_V), BF16),
        scratch_shapes=[pltpu.VMEM((2, s, pw), BF16),
                        pltpu.VMEM((2, s // tq, V_ROWS, tq), BF16),
                        pltpu.VMEM((2, V_ROWS, tq), F32),
                        pltpu.VMEM((2, 2, tq, tq), F32)],
        compiler_params=_cparams("parallel", "parallel", "arbitrary"),
        name="mla_attention",
    )(q, q, kv, kpe, kv, proj, cos, sin_signed, gains)


MM_TM = 1024
MM_TN = 1024
RES_TM = 512


def _layer_diff(x2, b, s, norm_g, w_in, qk_g, lam_vecs, subln_g, w_out, bias_tiles,
                layer_idx):
    proj = _norm_matmul(x2, norm_g, w_in.astype(BF16), out_dtype=BF16,
                        tm=MM_TM, tn=MM_TN)
    y = _diff_attention(proj.reshape(b, s, -1), lam_vecs, qk_g, subln_g,
                        bias_tiles, layer_idx)
    return _matmul_residual(y.reshape(b * s, -1), w_out.astype(BF16), x2,
                            tm=RES_TM)


def _layer_gla(x2, b, s, norm_g, w_in, w_gate, gate_bias, out_g, w_out):
    rank, hdk = w_gate.shape
    n_main = w_in.shape[1] - rank
    w_all = jnp.pad(w_in, ((0, 0), (0, LANES - rank))).astype(BF16)
    w_gate_t = jnp.pad(w_gate, ((0, LANES - rank), (0, 0))).T.astype(BF16)
    proj, lr = _norm_matmul(x2, norm_g, w_all, out_dtype=BF16, tm=MM_TM, tn=MM_TN,
                            n_cols=n_main, side_block=n_main // LANES)
    y = _gla(proj.reshape(b, s, -1), lr.reshape(b, s, LANES), w_gate_t, gate_bias,
             out_g, ts=512)
    return _matmul_residual(y.reshape(b * s, -1), w_out.astype(BF16), x2,
                            tm=RES_TM)


def _layer_rglru(x2, b, s, norm_g, w_in, conv_w, conv_b, w_rg, b_rg, w_ig, b_ig,
                 lam, w_out):
    width = w_in.shape[1] // 2
    u, gate = _norm_matmul(x2, norm_g, w_in.astype(BF16), out_dtype=BF16,
                           tm=MM_TM, tn=MM_TN, first=(width, F32))
    y = _rglru(u.reshape(b, s, width), gate.reshape(b, s, width), 0, conv_w, conv_b,
               w_rg, b_rg, w_ig, b_ig, lam, ts=256)
    return _matmul_residual(y.reshape(b * s, -1), w_out.astype(BF16), x2,
                            tm=RES_TM)


def _rope_tables(s):
    half = D_ROPE // 2
    inv = ROPE_THETA ** (-jnp.arange(half, dtype=F32) / half)
    ang = jnp.arange(s, dtype=F32)[:, None] * inv[None, :]
    cos, sin = jnp.cos(ang), jnp.sin(ang)
    return (jnp.concatenate([cos, cos, cos, cos], axis=-1),
            jnp.concatenate([-sin, sin, -sin, sin], axis=-1))


def _layer_mla(x2, b, s, norm_g, w_in, q_lat_g, kv_lat_g, w_uq, w_ukv, qk_g, w_out):
    q_rank, kv_rank = q_lat_g.shape[0], kv_lat_g.shape[0]
    heads = D_HEADS
    lat = q_rank + kv_rank
    w_kpe = w_in[:, lat:lat + D_ROPE]
    w_all = jnp.concatenate([w_in[:, :lat], w_in[:, lat + D_ROPE:], w_kpe, w_kpe],
                            axis=1).astype(BF16)
    n_main = w_in.shape[1] - D_ROPE
    uq = w_uq.reshape(q_rank, heads, D_NOPE + D_ROPE)
    w_uq_p = jnp.concatenate([uq[:, :, :D_NOPE].reshape(q_rank, -1),
                              uq[:, :, D_NOPE:].reshape(q_rank, -1)], axis=1).astype(BF16)
    ukv = w_ukv.reshape(kv_rank, heads, D_NOPE + D_V)
    w_ukv_p = jnp.concatenate([ukv[:, :, :D_NOPE].reshape(kv_rank, -1),
                               ukv[:, :, D_NOPE:].reshape(kv_rank, -1)], axis=1).astype(BF16)
    dup = lambda v: jnp.concatenate([v, v])
    gains = jnp.stack([qk_g[0, :D_NOPE], dup(qk_g[0, D_NOPE:]),
                       qk_g[1, :D_NOPE], dup(qk_g[1, D_NOPE:])]).astype(F32)
    cos, sin_signed = _rope_tables(s)

    proj, kpe = _norm_matmul(x2, norm_g, w_all, out_dtype=BF16, tm=MM_TM, tn=MM_TN,
                             n_cols=n_main, side_block=n_main // LANES)
    q = _norm_matmul(proj, q_lat_g, w_uq_p, out_dtype=BF16, tm=MM_TM, tn=MM_TN,
                     col_block=0)
    kv = _norm_matmul(proj, kv_lat_g, w_ukv_p, out_dtype=BF16, tm=MM_TM, tn=MM_TN,
                      col_block=1)
    gate_block0 = lat // (2 * LANES)
    y = _mla_attention(q.reshape(b, s, -1), kv.reshape(b, s, -1),
                       kpe.reshape(b, s, LANES), proj.reshape(b, s, -1),
                       gate_block0, cos, sin_signed, gains)
    return _matmul_residual(y.reshape(b * s, -1), w_out.astype(BF16), x2,
                            tm=RES_TM)


def kernel(x, norm_g, rel_bias, a_w_in, a_qk_g, a_lambda, a_subln_g, a_w_out, b_w_in, b_w_gate, b_gate_bias, b_out_g, b_w_out, c_w_in, c_conv_w, c_conv_b, c_w_rgate, c_b_rgate, c_w_igate, c_b_igate, c_lambda, c_w_out, d_w_in, d_q_lat_g, d_kv_lat_g, d_w_uq, d_w_ukv, d_qk_g, d_w_out):
    b, s, d = x.shape
    depth = norm_g.shape[0]
    x2 = x.reshape(b * s, d)
    bias_tiles = _t5_tiles(rel_bias, ATTN_TQ)
    for i in range(depth):
        m, j = i % 4, i // 4
        if m == 0:
            x2 = _layer_diff(x2, b, s, norm_g[i], a_w_in[j], a_qk_g[j], a_lambda[j],
                             a_subln_g[j], a_w_out[j], bias_tiles, i)
        elif m == 1:
            x2 = _layer_gla(x2, b, s, norm_g[i], b_w_in[j], b_w_gate[j],
                            b_gate_bias[j], b_out_g[j], b_w_out[j])
        elif m == 2:
            x2 = _layer_rglru(x2, b, s, norm_g[i], c_w_in[j], c_conv_w[j], c_conv_b[j],
                              c_w_rgate[j], c_b_rgate[j], c_w_igate[j], c_b_igate[j],
                              c_lambda[j], c_w_out[j])
        else:
            x2 = _layer_mla(x2, b, s, norm_g[i], d_w_in[j], d_q_lat_g[j],
                            d_kv_lat_g[j], d_w_uq[j], d_w_ukv[j], d_qk_g[j], d_w_out[j])
    return x2.reshape(b, s, d)
```

```python
import functools
import math

import jax
import jax.numpy as jnp
from jax import lax
from jax.experimental import pallas as pl
from jax.experimental.pallas import tpu as pltpu

F32 = jnp.float32
BF16 = jnp.bfloat16

EPS = 1e-6
NEG_INF = -1e30
LOG2E = math.log2(math.e)
CHUNK = 64
LANES = 128
SUBLANES = 8
BF16_ROWS = 16
V7X_VMEM_BYTES = 64 * 1024 * 1024
VMEM_LIMIT = V7X_VMEM_BYTES - 8 * 1024 * 1024

A_HEADS = 16
A_HEAD_DIM = 64
T5_BUCKETS = 32
B_HEADS = 4
B_GATE_TAU = 16.0
C_BLOCKS = 8
C_CONV = 4
C_C = 8.0
D_HEADS = 16
D_NOPE = 128
D_ROPE = 64
D_V = 128
ROPE_THETA = 10000.0

ATTN_TQ = 512
V_ROWS = LANES + BF16_ROWS
FAR_UNROLL = 4


def _cparams(*semantics):
    return pltpu.CompilerParams(dimension_semantics=semantics,
                                vmem_limit_bytes=VMEM_LIMIT)


def _sigmoid(x):
    return 0.5 * jnp.tanh(0.5 * x) + 0.5


def _silu(g):
    return g * _sigmoid(g)


def _softplus(x):
    return jnp.maximum(x, 0.0) + jnp.log(1.0 + jnp.exp(-jnp.abs(x)))


_EXPM1_SERIES_BOUND = 0.1
_EXPM1_SERIES_TERMS = 5


def _one_minus_exp(x, exp_x):
    poly = jnp.full_like(x, 1.0 / math.factorial(_EXPM1_SERIES_TERMS))
    for n in range(_EXPM1_SERIES_TERMS - 1, 0, -1):
        poly = poly * x + 1.0 / math.factorial(n)
    return jnp.where(x > -_EXPM1_SERIES_BOUND, -x * poly, 1.0 - exp_x)


def _nt_dot(a, b):
    return lax.dot_general(a, b, (((1,), (1,)), ((), ())),
                           preferred_element_type=F32)


def _norm_matmul_kernel(*refs, has_side, n_first):
    refs = list(refs)
    x_ref, g_ref, w_ref = refs[:3]
    del refs[:3]
    ws_ref = refs.pop(0) if has_side else None
    o_ref = refs.pop(0)
    rest_ref = refs.pop(0) if n_first is not None else None
    side_ref = refs.pop(0) if has_side else None
    h_ref = refs.pop(0)
    j = pl.program_id(1)

    @pl.when(j == 0)
    def _():
        x = x_ref[...].astype(F32)
        ms = jnp.mean(x * x, axis=-1, keepdims=True)
        h_ref[...] = (x * lax.rsqrt(ms + EPS) * g_ref[...]).astype(BF16)
        if has_side:
            side_ref[...] = jnp.dot(h_ref[...], ws_ref[...],
                                    preferred_element_type=F32).astype(side_ref.dtype)

    w = w_ref[...]
    if w.dtype != BF16:
        w = w.astype(BF16)
    res = jnp.dot(h_ref[...], w, preferred_element_type=F32)
    if n_first is None:
        o_ref[...] = res.astype(o_ref.dtype)
    else:
        @pl.when(j < n_first)
        def _():
            o_ref[...] = res.astype(o_ref.dtype)

        @pl.when(j >= n_first)
        def _():
            rest_ref[...] = res.astype(rest_ref.dtype)


def _norm_matmul(x, gain, w, *, out_dtype, tm, tn, col_block=0, n_cols=None,
                 w_side=None, first=None):
    m = x.shape[0]
    k = w.shape[0]
    n = w.shape[1] if n_cols is None else n_cols
    n_first = None if first is None else first[0] // tn
    in_specs = [pl.BlockSpec((tm, k), lambda i, j: (i, col_block)),
                pl.BlockSpec((1, k), lambda i, j: (0, 0)),
                pl.BlockSpec((k, tn), lambda i, j: (0, j))]
    args = [x, gain.reshape(1, k).astype(F32), w]
    if w_side is not None:
        in_specs.append(pl.BlockSpec((k, LANES), lambda i, j: (0, 0)))
        args.append(w_side)
    if first is None:
        out_specs = [pl.BlockSpec((tm, tn), lambda i, j: (i, j))]
        out_shape = [jax.ShapeDtypeStruct((m, n), out_dtype)]
    else:
        out_specs = [pl.BlockSpec((tm, tn), lambda i, j: (i, jnp.minimum(j, n_first - 1))),
                     pl.BlockSpec((tm, tn), lambda i, j: (i, jnp.maximum(j - n_first, 0)))]
        out_shape = [jax.ShapeDtypeStruct((m, first[0]), first[1]),
                     jax.ShapeDtypeStruct((m, n - first[0]), out_dtype)]
    if w_side is not None:
        out_specs.append(pl.BlockSpec((tm, LANES), lambda i, j: (i, 0)))
        out_shape.append(jax.ShapeDtypeStruct((m, LANES), BF16))
    outs = pl.pallas_call(
        functools.partial(_norm_matmul_kernel, has_side=w_side is not None,
                          n_first=n_first),
        grid=(m // tm, n // tn),
        in_specs=in_specs,
        out_specs=out_specs,
        out_shape=out_shape,
        scratch_shapes=[pltpu.VMEM((tm, k), BF16)],
        compiler_params=_cparams("parallel", "arbitrary"),
        name="norm_matmul",
    )(*args)
    return outs[0] if len(outs) == 1 else tuple(outs)


def _matmul_residual_kernel(y_ref, w_ref, x_ref, o_ref):
    o_ref[...] = x_ref[...] + jnp.dot(y_ref[...], w_ref[...],
                                      preferred_element_type=F32)


def _matmul_residual(y, w, x, *, tm):
    m, k = y.shape
    n = w.shape[1]
    return pl.pallas_call(
        _matmul_residual_kernel,
        grid=(m // tm,),
        in_specs=[pl.BlockSpec((tm, k), lambda i: (i, 0)),
                  pl.BlockSpec((k, n), lambda i: (0, 0)),
                  pl.BlockSpec((tm, n), lambda i: (i, 0))],
        out_specs=pl.BlockSpec((tm, n), lambda i: (i, 0)),
        out_shape=jax.ShapeDtypeStruct((m, n), F32),
        compiler_params=_cparams("parallel"),
        name="matmul_residual",
    )(y, w, x)


def _group_rmsnorm(x, gain, group):
    r = lax.broadcasted_iota(jnp.int32, (LANES, LANES), 0)
    c = lax.broadcasted_iota(jnp.int32, (LANES, LANES), 1)
    ones = jnp.where(r // group == c // group, 1.0, 0.0).astype(BF16)
    x2 = x * x
    hi = x2.astype(BF16)
    lo = (x2 - hi.astype(F32)).astype(BF16)
    ss = (jnp.dot(hi, ones, preferred_element_type=F32)
          + jnp.dot(lo, ones, preferred_element_type=F32))
    return x * lax.rsqrt(ss * (1.0 / group) + EPS) * gain


def _half_rmsnorm(x, gain, lo_mask):
    x2 = x * x
    lo = jnp.sum(jnp.where(lo_mask, x2, 0.0), axis=-1, keepdims=True)
    hi = jnp.sum(jnp.where(lo_mask, 0.0, x2), axis=-1, keepdims=True)
    ms = jnp.where(lo_mask, lo, hi) * (1.0 / 64)
    return x * lax.rsqrt(ms + EPS) * gain


def _store_vt_block(vt_ref, lead, v_rows):
    tq = v_rows.shape[0]
    r = lax.broadcasted_iota(jnp.int32, (LANES, LANES), 0)
    c = lax.broadcasted_iota(jnp.int32, (LANES, LANES), 1)
    eye = jnp.where(r == c, 1.0, 0.0).astype(BF16)
    vt_ref[(*lead, slice(0, LANES), slice(None))] = _nt_dot(eye, v_rows).astype(BF16)
    vt_ref[(*lead, slice(LANES, V_ROWS), slice(None))] = jnp.ones((BF16_ROWS, tq), BF16)


def _flash_init(s, vt):
    m = jnp.max(s, axis=0, keepdims=True)
    p = jnp.exp2(s - m).astype(BF16)
    return m, jnp.dot(vt, p, preferred_element_type=F32)


def _flash_update(s, vt, m, acc):
    m_new = jnp.maximum(m, jnp.max(s, axis=0, keepdims=True))
    alpha = jnp.exp2(m - m_new)
    p = jnp.exp2(s - m_new).astype(BF16)
    return m_new, alpha * acc + jnp.dot(vt, p, preferred_element_type=F32)


def _chunk_causal_flash(i, streams, acc_ref, s_ref):
    first = jnp.maximum(i - 1, 0)
    n_far = first

    def issue(blk, slot, add=None):
        for si, (scores, _, near_add) in enumerate(streams):
            s = scores(blk)
            s_ref[si, slot] = s if add is None else s + near_add(add)

    def update(blk, slot, ms):
        out = []
        for si, (_, values, _) in enumerate(streams):
            m, acc_ref[si] = _flash_update(s_ref[si, slot], values(blk), ms[si],
                                           acc_ref[si])
            out.append(m)
        return tuple(out)

    def group(t0, count, prefetch_next, ms):
        for k in range(count):
            if k + 1 < count or prefetch_next:
                issue(t0 + k + 1, (k + 1) % 2)
            ms = update(t0 + k, k % 2, ms)
        return ms

    issue(first, 0, add=0)
    issue(first + 1, 1, add=1)
    ms = []
    for si, (_, values, _) in enumerate(streams):
        m, acc_ref[si] = _flash_init(s_ref[si, 0], values(first))
        ms.append(m)
    issue(0, 0)
    ms = update(first + 1, 1, tuple(ms))

    full = jnp.maximum(n_far - 1, 0) // FAR_UNROLL
    rest = n_far - FAR_UNROLL * full
    ms = lax.fori_loop(0, full,
                       lambda g, ms: group(g * FAR_UNROLL, FAR_UNROLL, True, ms), ms)
    for count in range(1, FAR_UNROLL + 1):
        ms = lax.fori_loop(0, (rest == count).astype(jnp.int32),
                           lambda _, ms, count=count: group(n_far - count, count, False, ms),
                           ms)


def _normalised(acc):
    return acc[0:LANES] * (1.0 / acc[LANES:LANES + 1])


_T5_LARGE_THRESHOLDS = (12, 16, 23, 32, 46, 64, 91)


def _t5_tiles_kernel(rb_ref, o_ref, *, tq):
    h = pl.program_id(0)
    variant = pl.program_id(1)
    j = lax.broadcasted_iota(jnp.int32, (1, tq), 1)
    rel_row = jnp.where(j < tq // 2, -j, tq - j)
    n = jnp.abs(rel_row)
    large = jnp.full_like(n, 8)
    for t in _T5_LARGE_THRESHOLDS:
        large = large + (n >= t).astype(jnp.int32)
    bucket = jnp.where(rel_row > 0, T5_BUCKETS // 2, 0) + jnp.where(n < 8, n, large)
    far = rb_ref[T5_BUCKETS // 2 - 1, h]
    row = jnp.zeros((1, tq), F32)
    for b in range(T5_BUCKETS):
        row = jnp.where(bucket == b, (rb_ref[b, h] - far) * LOG2E, row)
    tile = pltpu.roll(jnp.broadcast_to(row, (2 * tq, tq)), 0, 1, stride=1, stride_axis=0)
    c = lax.broadcasted_iota(jnp.int32, (2 * tq, tq), 0)
    r = lax.broadcasted_iota(jnp.int32, (2 * tq, tq), 1)
    key_off = c - tq * variant
    bias = jnp.where(key_off - r > -_T5_LARGE_THRESHOLDS[-1], tile, 0.0)
    visible = (lax.shift_right_arithmetic(key_off, 6)
               <= lax.shift_right_arithmetic(r, 6))
    o_ref[0, 0] = jnp.where(visible, bias, NEG_INF)


def _t5_tiles(rel_bias, tq):
    nb, nh = rel_bias.shape
    return pl.pallas_call(
        functools.partial(_t5_tiles_kernel, tq=tq),
        grid=(nh, 2),
        in_specs=[pl.BlockSpec(memory_space=pltpu.SMEM)],
        out_specs=pl.BlockSpec((1, 1, 2 * tq, tq), lambda h, v: (h, v, 0, 0)),
        out_shape=jax.ShapeDtypeStruct((nh, 2, 2 * tq, tq), F32),
        compiler_params=_cparams("parallel", "arbitrary"),
        name="t5_tiles",
    )(rel_bias.astype(F32))


def _diff_attn_kernel(lamv_ref, q_ref, k_ref, v_ref, g_ref, bias_ref, qkg_ref,
                      sub_ref, o_ref, kn_ref, vt_ref, acc_ref, s_ref, *, tq, lam_init):
    i = pl.program_id(2)
    seq = k_ref.shape[1]
    lane = lax.broadcasted_iota(jnp.int32, (1, LANES), 1)
    lo_mask = lane < A_HEAD_DIM

    @pl.when(i == 0)
    def _():
        def body(c, carry):
            rows = pl.ds(pl.multiple_of(c * tq, tq), tq)
            kn_ref[rows, :] = _group_rmsnorm(
                k_ref[0, rows, :].astype(F32), qkg_ref[1:2, :], A_HEAD_DIM).astype(BF16)
            _store_vt_block(vt_ref, (c,), v_ref[0, rows, :])
            return carry
        lax.fori_loop(0, seq // tq, body, 0)

    lf = lamv_ref[...].astype(F32)
    lam = (jnp.exp(jnp.sum(lf[0:1] * lf[1:2], axis=-1, keepdims=True))
           - jnp.exp(jnp.sum(lf[2:3] * lf[3:4], axis=-1, keepdims=True)) + lam_init)

    qn = (_half_rmsnorm(q_ref[0].astype(F32), qkg_ref[0:1, :], lo_mask)
          * (A_HEAD_DIM ** -0.5 * LOG2E))
    q01 = jnp.concatenate([jnp.where(lo_mask, qn, 0.0).astype(BF16),
                           jnp.where(lo_mask, 0.0, qn).astype(BF16)], axis=0)

    def scores(blk):
        return _nt_dot(kn_ref[pl.ds(pl.multiple_of(blk * tq, tq), tq), :], q01)

    def near_add(j):
        bias_t = bias_ref[0, 0, j * tq:(j + 1) * tq, :]
        return jnp.concatenate([bias_t, bias_t], axis=1)

    _chunk_causal_flash(i, [(scores, lambda blk: vt_ref[blk], near_add)], acc_ref,
                        s_ref)

    o01 = _normalised(acc_ref[0])
    o_t = o01[:, 0:tq] - lam * o01[:, tq:]
    ms_t = jnp.mean(o_t * o_t, axis=0, keepdims=True)
    o_t = o_t * lax.rsqrt(ms_t + EPS) * sub_ref[...] * (1.0 - lam_init)
    o_ref[0] = (o_t.T * _silu(g_ref[0].astype(F32))).astype(o_ref.dtype)


def _diff_attention(proj, lam_vecs, qk_g, subln_g, bias_tiles, layer_idx):
    b, s, four_w = proj.shape
    width = four_w // 4
    hblocks = width // LANES
    tq = ATTN_TQ
    lam_init = 0.8 - 0.6 * math.exp(-0.3 * layer_idx)
    qkg = jnp.concatenate([qk_g, qk_g], axis=-1).astype(F32)
    return pl.pallas_call(
        functools.partial(_diff_attn_kernel, tq=tq, lam_init=lam_init),
        grid=(b, hblocks, s // tq),
        in_specs=[
            pl.BlockSpec((4, A_HEAD_DIM), lambda bi, h, i: (0, 0)),
            pl.BlockSpec((1, tq, LANES), lambda bi, h, i: (bi, i, h)),
            pl.BlockSpec((1, s, LANES), lambda bi, h, i: (bi, 0, hblocks + h)),
            pl.BlockSpec((1, s, LANES), lambda bi, h, i: (bi, 0, 2 * hblocks + h)),
            pl.BlockSpec((1, tq, LANES), lambda bi, h, i: (bi, i, 3 * hblocks + h)),
            pl.BlockSpec((1, 1, 2 * tq, tq),
                         lambda bi, h, i: (h, jnp.minimum(i, 1), 0, 0)),
            pl.BlockSpec((2, LANES), lambda bi, h, i: (0, 0)),
            pl.BlockSpec((LANES, 1), lambda bi, h, i: (0, 0)),
        ],
        out_specs=pl.BlockSpec((1, tq, LANES), lambda bi, h, i: (bi, i, h)),
        out_shape=jax.ShapeDtypeStruct((b, s, width), BF16),
        scratch_shapes=[pltpu.VMEM((s, LANES), BF16),
                        pltpu.VMEM((s // tq, V_ROWS, tq), BF16),
                        pltpu.VMEM((1, V_ROWS, 2 * tq), F32),
                        pltpu.VMEM((1, 2, tq, 2 * tq), F32)],
        compiler_params=_cparams("parallel", "parallel", "arbitrary"),
        name="diff_attention",
    )(lam_vecs.astype(F32), proj, proj, proj, proj, bias_tiles, qkg,
      subln_g.reshape(LANES, 1).astype(F32))


def _gla_kernel(q_ref, k_ref, v_ref, g_ref, lr_ref, wgt_ref, gb_ref, og_ref,
                cm_ref, o_ref, state_ref, *, ts):
    @pl.when(pl.program_id(2) == 0)
    def _():
        state_ref[...] = jnp.zeros_like(state_ref)

    dk = q_ref.shape[2]
    z_t = _nt_dot(wgt_ref[...], lr_ref[0]) + gb_ref[...]
    la_t = -_softplus(-z_t) * (1.0 / B_GATE_TAU)
    la_hi = la_t.astype(BF16)
    la_lo = (la_t - la_hi.astype(F32)).astype(BF16)

    span = cm_ref.shape[0]
    rest_t = jnp.concatenate(
        [jnp.dot(la_hi[:, s0:s0 + span], cm_ref[...], preferred_element_type=F32)
         + jnp.dot(la_lo[:, s0:s0 + span], cm_ref[...], preferred_element_type=F32)
         for s0 in range(0, ts, span)], axis=1)
    kd_t = (k_ref[0].astype(F32).T * jnp.exp(rest_t)).astype(BF16)

    lane = lax.broadcasted_iota(jnp.int32, (1, LANES), 1)
    first_chunk = lane < CHUNK
    state = state_ref[...]
    for c in range(ts // CHUNK):
        group = slice((c // 2) * LANES, (c // 2 + 1) * LANES)
        keep = first_chunk if c % 2 == 0 else jnp.logical_not(first_chunk)
        kd_c = jnp.where(keep, kd_t[:, group], jnp.zeros((), BF16))
        t0 = slice(c * CHUNK, c * CHUNK + 1)
        state = (jnp.exp(rest_t[:, t0] + la_t[:, t0]) * state
                 + jnp.dot(kd_c, v_ref[0, group, :], preferred_element_type=F32))
        rows = slice(c * CHUNK, (c + 1) * CHUNK)
        q = (q_ref[0, rows, :].astype(F32) * (dk ** -0.5)).astype(BF16)
        o = jnp.dot(q, state.astype(BF16), preferred_element_type=F32)
        ms = jnp.mean(o * o, axis=-1, keepdims=True)
        o = o * lax.rsqrt(ms + EPS) * og_ref[...]
        o_ref[0, rows, :] = (o * _silu(g_ref[0, rows, :].astype(F32))).astype(o_ref.dtype)
    state_ref[...] = state


def _gla(proj, lr, w_gate_t, gate_bias, out_g, *, ts):
    b, s, n = proj.shape
    heads = B_HEADS
    dk = n // (6 * heads)
    dv = 2 * dk
    kq = (heads * dk) // dk
    kv = (2 * heads * dk) // dv
    span = 2 * LANES
    t_in = jnp.arange(span)[:, None]
    t_out = jnp.arange(span)[None, :]
    chunk_masks = (((t_in // CHUNK) == (t_out // CHUNK)) & (t_in > t_out)).astype(BF16)
    return pl.pallas_call(
        functools.partial(_gla_kernel, ts=ts),
        grid=(b, heads, s // ts),
        in_specs=[
            pl.BlockSpec((1, ts, dk), lambda bi, h, i: (bi, i, h)),
            pl.BlockSpec((1, ts, dk), lambda bi, h, i: (bi, i, kq + h)),
            pl.BlockSpec((1, ts, dv), lambda bi, h, i: (bi, i, kv + h)),
            pl.BlockSpec((1, ts, dv), lambda bi, h, i: (bi, i, kv + heads + h)),
            pl.BlockSpec((1, ts, LANES), lambda bi, h, i: (bi, i, 0)),
            pl.BlockSpec((dk, LANES), lambda bi, h, i: (h, 0)),
            pl.BlockSpec((dk, 1), lambda bi, h, i: (h, 0)),
            pl.BlockSpec((1, dv), lambda bi, h, i: (0, 0)),
            pl.BlockSpec((span, span), lambda bi, h, i: (0, 0)),
        ],
        out_specs=pl.BlockSpec((1, ts, dv), lambda bi, h, i: (bi, i, h)),
        out_shape=jax.ShapeDtypeStruct((b, s, heads * dv), BF16),
        scratch_shapes=[pltpu.VMEM((dk, dv), F32)],
        compiler_params=_cparams("parallel", "parallel", "arbitrary"),
        name="gla",
    )(proj, proj, proj, proj, lr, w_gate_t, gate_bias.reshape(-1, 1).astype(F32),
      out_g.reshape(1, dv).astype(F32), chunk_masks)


def _rglru_kernel(u_ref, g_ref, cw_ref, cb_ref, wr_ref, br_ref, wi_ref, bi_ref,
                  lam_ref, o_ref, ubuf_ref, xc_ref, a_ref, b_ref, h_ref, *, ts):
    width = u_ref.shape[2]
    bd = width // C_BLOCKS

    @pl.when(pl.program_id(1) == 0)
    def _():
        ubuf_ref[0:SUBLANES, :] = jnp.zeros((SUBLANES, width), F32)
        h_ref[...] = jnp.zeros_like(h_ref)

    ubuf_ref[SUBLANES:SUBLANES + ts, :] = u_ref[0]
    xc = cb_ref[...]
    for t in range(C_CONV):
        off = SUBLANES - (C_CONV - 1) + t
        xc = xc + ubuf_ref[off:off + ts, :] * cw_ref[t:t + 1, :]
    xc_ref[...] = xc
    ubuf_ref[0:SUBLANES, :] = ubuf_ref[ts:ts + SUBLANES, :]

    rate = -C_C * _softplus(-lam_ref[...])
    for n in range(C_BLOCKS):
        cols = slice(n * bd, (n + 1) * bd)
        xb = xc_ref[:, cols]
        xb16 = xb.astype(BF16)
        r = _sigmoid(jnp.dot(xb16, wr_ref[n], preferred_element_type=F32)
                     + br_ref[:, cols])
        gate_i = _sigmoid(jnp.dot(xb16, wi_ref[n], preferred_element_type=F32)
                          + bi_ref[:, cols])
        log_a = r * rate[:, cols]
        a = jnp.exp(log_a)
        a_ref[:, cols] = a
        b_ref[:, cols] = jnp.sqrt(_one_minus_exp(2.0 * log_a, a * a)) * (gate_i * xb)

    row = lax.broadcasted_iota(jnp.int32, (SUBLANES, width), 0)

    def tile(t, h_prev):
        rows = pl.ds(pl.multiple_of(t * SUBLANES, SUBLANES), SUBLANES)
        a = a_ref[rows, :]
        b = b_ref[rows, :]
        d = 1
        while d < SUBLANES:
            a_up = pltpu.roll(a, d, 0)
            b_up = pltpu.roll(b, d, 0)
            keep = row >= d
            b = jnp.where(keep, a * b_up + b, b)
            a = jnp.where(keep, a * a_up, a)
            d *= 2
        h = b + a * h_prev
        b_ref[rows, :] = h
        return h[SUBLANES - 1:SUBLANES, :]

    h_ref[...] = lax.fori_loop(0, ts // SUBLANES, tile, h_ref[...])
    o_ref[0] = (b_ref[...] * _silu(g_ref[0].astype(F32))).astype(o_ref.dtype)


def _rglru(u, gate, gate_block, conv_w, conv_b, w_rg, b_rg, w_ig, b_ig, lam, *, ts):
    b, s, width = u.shape
    bd = width // C_BLOCKS
    row = lambda a: a.reshape(1, width).astype(F32)
    const2 = lambda bi, i: (0, 0)
    return pl.pallas_call(
        functools.partial(_rglru_kernel, ts=ts),
        grid=(b, s // ts),
        in_specs=[
            pl.BlockSpec((1, ts, width), lambda bi, i: (bi, i, 0)),
            pl.BlockSpec((1, ts, width), lambda bi, i: (bi, i, gate_block)),
            pl.BlockSpec((C_CONV, width), const2),
            pl.BlockSpec((1, width), const2),
            pl.BlockSpec((C_BLOCKS, bd, bd), lambda bi, i: (0, 0, 0)),
            pl.BlockSpec((1, width), const2),
            pl.BlockSpec((C_BLOCKS, bd, bd), lambda bi, i: (0, 0, 0)),
            pl.BlockSpec((1, width), const2),
            pl.BlockSpec((1, width), const2),
        ],
        out_specs=pl.BlockSpec((1, ts, width), lambda bi, i: (bi, i, 0)),
        out_shape=jax.ShapeDtypeStruct((b, s, width), BF16),
        scratch_shapes=[pltpu.VMEM((ts + SUBLANES, width), F32),
                        pltpu.VMEM((ts, width), F32),
                        pltpu.VMEM((ts, width), F32),
                        pltpu.VMEM((ts, width), F32),
                        pltpu.VMEM((1, width), F32)],
        compiler_params=_cparams("parallel", "arbitrary"),
        name="rglru",
    )(u, gate, conv_w.astype(F32), row(conv_b), w_rg.astype(BF16), row(b_rg),
      w_ig.astype(BF16), row(b_ig), row(lam))


def _swap_rope_halves(x):
    lane = lax.broadcasted_iota(jnp.int32, (1, LANES), 1)
    first = (lane % D_ROPE) < (D_ROPE // 2)
    return jnp.where(first, pltpu.roll(x, LANES - D_ROPE // 2, 1),
                     pltpu.roll(x, D_ROPE // 2, 1))


def _rope(x, cos, sin_signed):
    return x * cos + _swap_rope_halves(x) * sin_signed


def _mla_attn_kernel(qn_ref, qp_ref, kn_ref, kp_ref, v_ref, g_ref, cos_ref,
                     sin_ref, gains_ref, o_ref, kcat_ref, vt_ref, acc_ref, s_ref,
                     *, tq):
    i = pl.program_id(2)
    seq = kn_ref.shape[1]
    lane = lax.broadcasted_iota(jnp.int32, (1, LANES), 1)
    lo_mask = lane < D_ROPE
    g_q_nope, g_q_pe = gains_ref[0:1, :], gains_ref[1:2, :]
    g_k_nope, g_k_pe = gains_ref[2:3, :], gains_ref[3:4, :]

    @pl.when(i == 0)
    def _():
        def body(c, carry):
            rows = pl.ds(pl.multiple_of(c * tq, tq), tq)
            kp = _rope(_group_rmsnorm(kp_ref[0, rows, :].astype(F32), g_k_pe, D_ROPE),
                       cos_ref[rows, :], sin_ref[rows, :])
            kn = kn_ref[0, rows, :].astype(F32)
            kcat_ref[0, rows, 0:LANES] = _group_rmsnorm(
                kn[:, 0:LANES], g_k_nope, D_NOPE).astype(BF16)
            kcat_ref[0, rows, LANES:] = jnp.where(lo_mask, kp, 0.0).astype(BF16)
            kcat_ref[1, rows, 0:LANES] = _group_rmsnorm(
                kn[:, LANES:], g_k_nope, D_NOPE).astype(BF16)
            kcat_ref[1, rows, LANES:] = jnp.where(lo_mask, 0.0, kp).astype(BF16)
            _store_vt_block(vt_ref, (0, c), v_ref[0, rows, 0:D_V])
            _store_vt_block(vt_ref, (1, c), v_ref[0, rows, D_V:])
            return carry
        lax.fori_loop(0, seq // tq, body, 0)

    scale = (D_NOPE + D_ROPE) ** -0.5 * LOG2E
    qrows = pl.ds(pl.multiple_of(i * tq, tq), tq)
    qp = _rope(_group_rmsnorm(qp_ref[0].astype(F32), g_q_pe, D_ROPE),
               cos_ref[qrows, :], sin_ref[qrows, :]) * scale
    qn = qn_ref[0].astype(F32)
    qcat = (
        jnp.concatenate([(_group_rmsnorm(qn[:, 0:LANES], g_q_nope, D_NOPE)
                          * scale).astype(BF16),
                         jnp.where(lo_mask, qp, 0.0).astype(BF16)], axis=-1),
        jnp.concatenate([(_group_rmsnorm(qn[:, LANES:], g_q_nope, D_NOPE)
                          * scale).astype(BF16),
                         jnp.where(lo_mask, 0.0, qp).astype(BF16)], axis=-1),
    )

    def near_add(j):
        c = lax.broadcasted_iota(jnp.int32, (tq, tq), 0) + tq * (j - jnp.minimum(i, 1))
        r = lax.broadcasted_iota(jnp.int32, (tq, tq), 1)
        visible = (lax.shift_right_arithmetic(c, 6) <= lax.shift_right_arithmetic(r, 6))
        return jnp.where(visible, 0.0, NEG_INF).astype(F32)

    def stream(hh):
        def scores(blk):
            rows = pl.ds(pl.multiple_of(blk * tq, tq), tq)
            return _nt_dot(kcat_ref[hh, rows, :], qcat[hh])
        return scores, lambda blk: vt_ref[hh, blk], near_add

    _chunk_causal_flash(i, [stream(0), stream(1)], acc_ref, s_ref)

    g = g_ref[0].astype(F32)
    o_ref[0, :, 0:D_V] = (_normalised(acc_ref[0]).T * _silu(g[:, 0:D_V])).astype(o_ref.dtype)
    o_ref[0, :, D_V:] = (_normalised(acc_ref[1]).T * _silu(g[:, D_V:])).astype(o_ref.dtype)


def _mla_attention(q, kv, kpe, proj, gate_block0, cos, sin_signed, gains):
    b, s, _ = q.shape
    heads = D_HEADS
    pairs = heads // 2
    tq = ATTN_TQ
    pw = 2 * LANES
    return pl.pallas_call(
        functools.partial(_mla_attn_kernel, tq=tq),
        grid=(b, pairs, s // tq),
        in_specs=[
            pl.BlockSpec((1, tq, pw), lambda bi, p, i: (bi, i, p)),
            pl.BlockSpec((1, tq, LANES), lambda bi, p, i: (bi, i, 2 * pairs + p)),
            pl.BlockSpec((1, s, pw), lambda bi, p, i: (bi, 0, p)),
            pl.BlockSpec((1, s, LANES), lambda bi, p, i: (bi, 0, 0)),
            pl.BlockSpec((1, s, pw), lambda bi, p, i: (bi, 0, pairs + p)),
            pl.BlockSpec((1, tq, pw), lambda bi, p, i: (bi, i, gate_block0 + p)),
            pl.BlockSpec((s, LANES), lambda bi, p, i: (0, 0)),
            pl.BlockSpec((s, LANES), lambda bi, p, i: (0, 0)),
            pl.BlockSpec((4, LANES), lambda bi, p, i: (0, 0)),
        ],
        out_specs=pl.BlockSpec((1, tq, pw), lambda bi, p, i: (bi, i, p)),
        out_shape=jax.ShapeDtypeStruct((b, s, heads * D_V), BF16),
        scratch_shapes=[pltpu.VMEM((2, s, pw), BF16),
                        pltpu.VMEM((2, s // tq, V_ROWS, tq), BF16),
                        pltpu.VMEM((2, V_ROWS, tq), F32),
                        pltpu.VMEM((2, 2, tq, tq), F32)],
        compiler_params=_cparams("parallel", "parallel", "arbitrary"),
        name="mla_attention",
    )(q, q, kv, kpe, kv, proj, cos, sin_signed, gains)


MM_TM = 1024
MM_TN = 1024
RES_TM = 512


def _layer_diff(x2, b, s, norm_g, w_in, qk_g, lam_vecs, subln_g, w_out, bias_tiles,
                layer_idx):
    proj = _norm_matmul(x2, norm_g, w_in, out_dtype=BF16, tm=MM_TM, tn=MM_TN)
    y = _diff_attention(proj.reshape(b, s, -1), lam_vecs, qk_g, subln_g,
                        bias_tiles, layer_idx)
    return _matmul_residual(y.reshape(b * s, -1), w_out.astype(BF16), x2,
                            tm=RES_TM)


def _layer_gla(x2, b, s, norm_g, w_in, w_gate, gate_bias, out_g, w_out):
    rank, hdk = w_gate.shape
    n_main = w_in.shape[1] - rank
    w_lr = jnp.pad(w_in[:, n_main:], ((0, 0), (0, LANES - rank))).astype(BF16)
    w_gate_t = jnp.pad(w_gate, ((0, LANES - rank), (0, 0))).T.astype(BF16)
    proj, lr = _norm_matmul(x2, norm_g, w_in, out_dtype=BF16, tm=MM_TM, tn=MM_TN,
                            n_cols=n_main, w_side=w_lr)
    y = _gla(proj.reshape(b, s, -1), lr.reshape(b, s, LANES), w_gate_t, gate_bias,
             out_g, ts=512)
    return _matmul_residual(y.reshape(b * s, -1), w_out.astype(BF16), x2,
                            tm=RES_TM)


def _layer_rglru(x2, b, s, norm_g, w_in, conv_w, conv_b, w_rg, b_rg, w_ig, b_ig,
                 lam, w_out):
    width = w_in.shape[1] // 2
    u, gate = _norm_matmul(x2, norm_g, w_in, out_dtype=BF16,
                           tm=MM_TM, tn=MM_TN, first=(width, F32))
    y = _rglru(u.reshape(b, s, width), gate.reshape(b, s, width), 0, conv_w, conv_b,
               w_rg, b_rg, w_ig, b_ig, lam, ts=256)
    return _matmul_residual(y.reshape(b * s, -1), w_out.astype(BF16), x2,
                            tm=RES_TM)


def _rope_tables(s):
    half = D_ROPE // 2
    inv = ROPE_THETA ** (-jnp.arange(half, dtype=F32) / half)
    ang = jnp.arange(s, dtype=F32)[:, None] * inv[None, :]
    cos, sin = jnp.cos(ang), jnp.sin(ang)
    return (jnp.concatenate([cos, cos, cos, cos], axis=-1),
            jnp.concatenate([-sin, sin, -sin, sin], axis=-1))


def _layer_mla(x2, b, s, norm_g, w_in, q_lat_g, kv_lat_g, w_uq, w_ukv, qk_g, w_out):
    q_rank, kv_rank = q_lat_g.shape[0], kv_lat_g.shape[0]
    heads = D_HEADS
    lat = q_rank + kv_rank
    w_main = jnp.concatenate([w_in[:, :lat], w_in[:, lat + D_ROPE:]], axis=1)
    w_kpe = w_in[:, lat:lat + D_ROPE]
    w_kpe = jnp.concatenate([w_kpe, w_kpe], axis=1).astype(BF16)
    uq = w_uq.reshape(q_rank, heads, D_NOPE + D_ROPE)
    w_uq_p = jnp.concatenate([uq[:, :, :D_NOPE].reshape(q_rank, -1),
                              uq[:, :, D_NOPE:].reshape(q_rank, -1)], axis=1).astype(BF16)
    ukv = w_ukv.reshape(kv_rank, heads, D_NOPE + D_V)
    w_ukv_p = jnp.concatenate([ukv[:, :, :D_NOPE].reshape(kv_rank, -1),
                               ukv[:, :, D_NOPE:].reshape(kv_rank, -1)], axis=1).astype(BF16)
    dup = lambda v: jnp.concatenate([v, v])
    gains = jnp.stack([qk_g[0, :D_NOPE], dup(qk_g[0, D_NOPE:]),
                       qk_g[1, :D_NOPE], dup(qk_g[1, D_NOPE:])]).astype(F32)
    cos, sin_signed = _rope_tables(s)

    proj, kpe = _norm_matmul(x2, norm_g, w_main, out_dtype=BF16, tm=MM_TM, tn=MM_TN,
                             w_side=w_kpe)
    q = _norm_matmul(proj, q_lat_g, w_uq_p, out_dtype=BF16, tm=MM_TM,
                     tn=w_uq_p.shape[1], col_block=0)
    kv = _norm_matmul(proj, kv_lat_g, w_ukv_p, out_dtype=BF16, tm=MM_TM,
                      tn=w_ukv_p.shape[1], col_block=1)
    gate_block0 = lat // (2 * LANES)
    y = _mla_attention(q.reshape(b, s, -1), kv.reshape(b, s, -1),
                       kpe.reshape(b, s, LANES), proj.reshape(b, s, -1),
                       gate_block0, cos, sin_signed, gains)
    return _matmul_residual(y.reshape(b * s, -1), w_out.astype(BF16), x2,
                            tm=RES_TM)


def kernel(x, norm_g, rel_bias, a_w_in, a_qk_g, a_lambda, a_subln_g, a_w_out, b_w_in, b_w_gate, b_gate_bias, b_out_g, b_w_out, c_w_in, c_conv_w, c_conv_b, c_w_rgate, c_b_rgate, c_w_igate, c_b_igate, c_lambda, c_w_out, d_w_in, d_q_lat_g, d_kv_lat_g, d_w_uq, d_w_ukv, d_qk_g, d_w_out):
    b, s, d = x.shape
    depth = norm_g.shape[0]
    x2 = x.reshape(b * s, d)
    bias_tiles = _t5_tiles(rel_bias, ATTN_TQ)
    for i in range(depth):
        m, j = i % 4, i // 4
        if m == 0:
            x2 = _layer_diff(x2, b, s, norm_g[i], a_w_in[j], a_qk_g[j], a_lambda[j],
                             a_subln_g[j], a_w_out[j], bias_tiles, i)
        elif m == 1:
            x2 = _layer_gla(x2, b, s, norm_g[i], b_w_in[j], b_w_gate[j],
                            b_gate_bias[j], b_out_g[j], b_w_out[j])
        elif m == 2:
            x2 = _layer_rglru(x2, b, s, norm_g[i], c_w_in[j], c_conv_w[j], c_conv_b[j],
                              c_w_rgate[j], c_b_rgate[j], c_w_igate[j], c_b_igate[j],
                              c_lambda[j], c_w_out[j])
        else:
            x2 = _layer_mla(x2, b, s, norm_g[i], d_w_in[j], d_q_lat_g[j],
                            d_kv_lat_g[j], d_w_uq[j], d_w_ukv[j], d_qk_g[j], d_w_out[j])
    return x2.reshape(b, s, d)
```

```python
import functools
import math

import jax
import jax.numpy as jnp
from jax import lax
from jax.experimental import pallas as pl
from jax.experimental.pallas import tpu as pltpu

F32 = jnp.float32
BF16 = jnp.bfloat16

EPS = 1e-6
NEG_INF = -1e30
LOG2E = math.log2(math.e)
CHUNK = 64
LANES = 128
SUBLANES = 8
BF16_ROWS = 16
V7X_VMEM_BYTES = 64 * 1024 * 1024
VMEM_LIMIT = V7X_VMEM_BYTES - 8 * 1024 * 1024

A_HEADS = 16
A_HEAD_DIM = 64
T5_BUCKETS = 32
B_HEADS = 4
B_GATE_TAU = 16.0
C_BLOCKS = 8
C_CONV = 4
C_C = 8.0
D_HEADS = 16
D_NOPE = 128
D_ROPE = 64
D_V = 128
ROPE_THETA = 10000.0

ATTN_TQ = 512
V_ROWS = LANES + BF16_ROWS
FAR_UNROLL = 4


def _cparams(*semantics):
    return pltpu.CompilerParams(dimension_semantics=semantics,
                                vmem_limit_bytes=VMEM_LIMIT)


def _sigmoid(x):
    return 0.5 * jnp.tanh(0.5 * x) + 0.5


def _silu(g):
    return g * _sigmoid(g)


def _softplus(x):
    return jnp.maximum(x, 0.0) + jnp.log(1.0 + jnp.exp(-jnp.abs(x)))


_EXPM1_SERIES_BOUND = 0.1
_EXPM1_SERIES_TERMS = 5


def _one_minus_exp(x, exp_x):
    poly = jnp.full_like(x, 1.0 / math.factorial(_EXPM1_SERIES_TERMS))
    for n in range(_EXPM1_SERIES_TERMS - 1, 0, -1):
        poly = poly * x + 1.0 / math.factorial(n)
    return jnp.where(x > -_EXPM1_SERIES_BOUND, -x * poly, 1.0 - exp_x)


def _nt_dot(a, b):
    return lax.dot_general(a, b, (((1,), (1,)), ((), ())),
                           preferred_element_type=F32)


def _norm_matmul_kernel(*refs, has_side, n_first):
    refs = list(refs)
    x_ref, g_ref, w_ref = refs[:3]
    del refs[:3]
    ws_ref = refs.pop(0) if has_side else None
    o_ref = refs.pop(0)
    rest_ref = refs.pop(0) if n_first is not None else None
    side_ref = refs.pop(0) if has_side else None
    h_ref = refs.pop(0)
    j = pl.program_id(1)

    @pl.when(j == 0)
    def _():
        x = x_ref[...].astype(F32)
        ms = jnp.mean(x * x, axis=-1, keepdims=True)
        h_ref[...] = (x * lax.rsqrt(ms + EPS) * g_ref[...]).astype(BF16)
        if has_side:
            side_ref[...] = jnp.dot(h_ref[...], ws_ref[...],
                                    preferred_element_type=F32).astype(side_ref.dtype)

    w = w_ref[...]
    if w.dtype != BF16:
        w = w.astype(BF16)
    res = jnp.dot(h_ref[...], w, preferred_element_type=F32)
    if n_first is None:
        o_ref[...] = res.astype(o_ref.dtype)
    else:
        @pl.when(j < n_first)
        def _():
            o_ref[...] = res.astype(o_ref.dtype)

        @pl.when(j >= n_first)
        def _():
            rest_ref[...] = res.astype(rest_ref.dtype)


def _norm_matmul(x, gain, w, *, out_dtype, tm, tn, col_block=0, n_cols=None,
                 w_side=None, first=None):
    m = x.shape[0]
    k = w.shape[0]
    n = w.shape[1] if n_cols is None else n_cols
    n_first = None if first is None else first[0] // tn
    in_specs = [pl.BlockSpec((tm, k), lambda i, j: (i, col_block)),
                pl.BlockSpec((1, k), lambda i, j: (0, 0)),
                pl.BlockSpec((k, tn), lambda i, j: (0, j))]
    args = [x, gain.reshape(1, k).astype(F32), w]
    if w_side is not None:
        in_specs.append(pl.BlockSpec((k, LANES), lambda i, j: (0, 0)))
        args.append(w_side)
    if first is None:
        out_specs = [pl.BlockSpec((tm, tn), lambda i, j: (i, j))]
        out_shape = [jax.ShapeDtypeStruct((m, n), out_dtype)]
    else:
        out_specs = [pl.BlockSpec((tm, tn), lambda i, j: (i, jnp.minimum(j, n_first - 1))),
                     pl.BlockSpec((tm, tn), lambda i, j: (i, jnp.maximum(j - n_first, 0)))]
        out_shape = [jax.ShapeDtypeStruct((m, first[0]), first[1]),
                     jax.ShapeDtypeStruct((m, n - first[0]), out_dtype)]
    if w_side is not None:
        out_specs.append(pl.BlockSpec((tm, LANES), lambda i, j: (i, 0)))
        out_shape.append(jax.ShapeDtypeStruct((m, LANES), BF16))
    outs = pl.pallas_call(
        functools.partial(_norm_matmul_kernel, has_side=w_side is not None,
                          n_first=n_first),
        grid=(m // tm, n // tn),
        in_specs=in_specs,
        out_specs=out_specs,
        out_shape=out_shape,
        scratch_shapes=[pltpu.VMEM((tm, k), BF16)],
        compiler_params=_cparams("parallel", "arbitrary"),
        name="norm_matmul",
    )(*args)
    return outs[0] if len(outs) == 1 else tuple(outs)


def _matmul_residual_kernel(y_ref, w_ref, x_ref, o_ref):
    o_ref[...] = x_ref[...] + jnp.dot(y_ref[...], w_ref[...],
                                      preferred_element_type=F32)


def _matmul_residual(y, w, x, *, tm):
    m, k = y.shape
    n = w.shape[1]
    return pl.pallas_call(
        _matmul_residual_kernel,
        grid=(m // tm,),
        in_specs=[pl.BlockSpec((tm, k), lambda i: (i, 0)),
                  pl.BlockSpec((k, n), lambda i: (0, 0)),
                  pl.BlockSpec((tm, n), lambda i: (i, 0))],
        out_specs=pl.BlockSpec((tm, n), lambda i: (i, 0)),
        out_shape=jax.ShapeDtypeStruct((m, n), F32),
        compiler_params=_cparams("parallel"),
        name="matmul_residual",
    )(y, w, x)


def _group_rmsnorm(x, gain, group):
    r = lax.broadcasted_iota(jnp.int32, (LANES, LANES), 0)
    c = lax.broadcasted_iota(jnp.int32, (LANES, LANES), 1)
    ones = jnp.where(r // group == c // group, 1.0, 0.0).astype(BF16)
    x2 = x * x
    hi = x2.astype(BF16)
    lo = (x2 - hi.astype(F32)).astype(BF16)
    ss = (jnp.dot(hi, ones, preferred_element_type=F32)
          + jnp.dot(lo, ones, preferred_element_type=F32))
    return x * lax.rsqrt(ss * (1.0 / group) + EPS) * gain


def _half_rmsnorm(x, gain, lo_mask):
    x2 = x * x
    lo = jnp.sum(jnp.where(lo_mask, x2, 0.0), axis=-1, keepdims=True)
    hi = jnp.sum(jnp.where(lo_mask, 0.0, x2), axis=-1, keepdims=True)
    ms = jnp.where(lo_mask, lo, hi) * (1.0 / 64)
    return x * lax.rsqrt(ms + EPS) * gain


def _store_vt_block(vt_ref, lead, v_rows):
    tq = v_rows.shape[0]
    r = lax.broadcasted_iota(jnp.int32, (LANES, LANES), 0)
    c = lax.broadcasted_iota(jnp.int32, (LANES, LANES), 1)
    eye = jnp.where(r == c, 1.0, 0.0).astype(BF16)
    vt_ref[(*lead, slice(0, LANES), slice(None))] = _nt_dot(eye, v_rows).astype(BF16)
    vt_ref[(*lead, slice(LANES, V_ROWS), slice(None))] = jnp.ones((BF16_ROWS, tq), BF16)


def _flash_init(s, vt):
    m = jnp.max(s, axis=0, keepdims=True)
    p = jnp.exp2(s - m).astype(BF16)
    return m, jnp.dot(vt, p, preferred_element_type=F32)


def _flash_update(s, vt, m, acc):
    m_new = jnp.maximum(m, jnp.max(s, axis=0, keepdims=True))
    alpha = jnp.exp2(m - m_new)
    p = jnp.exp2(s - m_new).astype(BF16)
    return m_new, alpha * acc + jnp.dot(vt, p, preferred_element_type=F32)


def _chunk_causal_flash(i, streams, acc_ref, s_ref):
    first = jnp.maximum(i - 1, 0)
    n_far = first

    def issue(blk, slot, add=None):
        for si, (scores, _, near_add) in enumerate(streams):
            s = scores(blk)
            s_ref[si, slot] = s if add is None else s + near_add(add)

    def update(blk, slot, ms):
        out = []
        for si, (_, values, _) in enumerate(streams):
            m, acc_ref[si] = _flash_update(s_ref[si, slot], values(blk), ms[si],
                                           acc_ref[si])
            out.append(m)
        return tuple(out)

    def group(t0, count, prefetch_next, ms):
        for k in range(count):
            if k + 1 < count or prefetch_next:
                issue(t0 + k + 1, (k + 1) % 2)
            ms = update(t0 + k, k % 2, ms)
        return ms

    issue(first, 0, add=0)
    issue(first + 1, 1, add=1)
    ms = []
    for si, (_, values, _) in enumerate(streams):
        m, acc_ref[si] = _flash_init(s_ref[si, 0], values(first))
        ms.append(m)
    issue(0, 0)
    ms = update(first + 1, 1, tuple(ms))

    full = jnp.maximum(n_far - 1, 0) // FAR_UNROLL
    rest = n_far - FAR_UNROLL * full
    ms = lax.fori_loop(0, full,
                       lambda g, ms: group(g * FAR_UNROLL, FAR_UNROLL, True, ms), ms)
    for count in range(1, FAR_UNROLL + 1):
        ms = lax.fori_loop(0, (rest == count).astype(jnp.int32),
                           lambda _, ms, count=count: group(n_far - count, count, False, ms),
                           ms)


def _normalised(acc):
    return acc[0:LANES] * (1.0 / acc[LANES:LANES + 1])


_T5_LARGE_THRESHOLDS = (12, 16, 23, 32, 46, 64, 91)


def _t5_tiles_kernel(rb_ref, o_ref, *, tq):
    h = pl.program_id(0)
    variant = pl.program_id(1)
    j = lax.broadcasted_iota(jnp.int32, (1, tq), 1)
    rel_row = jnp.where(j < tq // 2, -j, tq - j)
    n = jnp.abs(rel_row)
    large = jnp.full_like(n, 8)
    for t in _T5_LARGE_THRESHOLDS:
        large = large + (n >= t).astype(jnp.int32)
    bucket = jnp.where(rel_row > 0, T5_BUCKETS // 2, 0) + jnp.where(n < 8, n, large)
    far = rb_ref[T5_BUCKETS // 2 - 1, h]
    row = jnp.zeros((1, tq), F32)
    for b in range(T5_BUCKETS):
        row = jnp.where(bucket == b, (rb_ref[b, h] - far) * LOG2E, row)
    tile = pltpu.roll(jnp.broadcast_to(row, (2 * tq, tq)), 0, 1, stride=1, stride_axis=0)
    c = lax.broadcasted_iota(jnp.int32, (2 * tq, tq), 0)
    r = lax.broadcasted_iota(jnp.int32, (2 * tq, tq), 1)
    key_off = c - tq * variant
    bias = jnp.where(key_off - r > -_T5_LARGE_THRESHOLDS[-1], tile, 0.0)
    visible = (lax.shift_right_arithmetic(key_off, 6)
               <= lax.shift_right_arithmetic(r, 6))
    o_ref[0, 0] = jnp.where(visible, bias, NEG_INF)


def _t5_tiles(rel_bias, tq):
    nb, nh = rel_bias.shape
    return pl.pallas_call(
        functools.partial(_t5_tiles_kernel, tq=tq),
        grid=(nh, 2),
        in_specs=[pl.BlockSpec(memory_space=pltpu.SMEM)],
        out_specs=pl.BlockSpec((1, 1, 2 * tq, tq), lambda h, v: (h, v, 0, 0)),
        out_shape=jax.ShapeDtypeStruct((nh, 2, 2 * tq, tq), F32),
        compiler_params=_cparams("parallel", "arbitrary"),
        name="t5_tiles",
    )(rel_bias.astype(F32))


def _diff_attn_kernel(lamv_ref, q0_ref, qnext_ref, k_ref, v_ref, g_ref, bias_ref,
                      qkg_ref, sub_ref, o_ref, kn_ref, vt_ref, q01_ref, acc_ref, s_ref,
                      *, tq, lam_init):
    i = pl.program_id(2)
    seq = k_ref.shape[1]
    lane = lax.broadcasted_iota(jnp.int32, (1, LANES), 1)
    lo_mask = lane < A_HEAD_DIM

    def prep_q(q_rows):
        qn = (_half_rmsnorm(q_rows.astype(F32), qkg_ref[0:1, :], lo_mask)
              * (A_HEAD_DIM ** -0.5 * LOG2E))
        return jnp.concatenate([jnp.where(lo_mask, qn, 0.0).astype(BF16),
                                jnp.where(lo_mask, 0.0, qn).astype(BF16)], axis=0)

    @pl.when(i == 0)
    def _():
        q01_ref[0] = prep_q(q0_ref[0])
        def body(c, carry):
            rows = pl.ds(pl.multiple_of(c * tq, tq), tq)
            kn_ref[rows, :] = _group_rmsnorm(
                k_ref[0, rows, :].astype(F32), qkg_ref[1:2, :], A_HEAD_DIM).astype(BF16)
            _store_vt_block(vt_ref, (c,), v_ref[0, rows, :])
            return carry
        lax.fori_loop(0, seq // tq, body, 0)

    lf = lamv_ref[...].astype(F32)
    lam = (jnp.exp(jnp.sum(lf[0:1] * lf[1:2], axis=-1, keepdims=True))
           - jnp.exp(jnp.sum(lf[2:3] * lf[3:4], axis=-1, keepdims=True)) + lam_init)

    q01 = q01_ref[i % 2]
    q01_ref[(i + 1) % 2] = prep_q(qnext_ref[0])

    def scores(blk):
        return _nt_dot(kn_ref[pl.ds(pl.multiple_of(blk * tq, tq), tq), :], q01)

    def near_add(j):
        bias_t = bias_ref[0, 0, j * tq:(j + 1) * tq, :]
        return jnp.concatenate([bias_t, bias_t], axis=1)

    _chunk_causal_flash(i, [(scores, lambda blk: vt_ref[blk], near_add)], acc_ref,
                        s_ref)

    o01 = _normalised(acc_ref[0])
    o_t = o01[:, 0:tq] - lam * o01[:, tq:]
    ms_t = jnp.mean(o_t * o_t, axis=0, keepdims=True)
    o_t = o_t * lax.rsqrt(ms_t + EPS) * sub_ref[...] * (1.0 - lam_init)
    o_ref[0] = (o_t.T * _silu(g_ref[0].astype(F32))).astype(o_ref.dtype)


def _diff_attention(proj, lam_vecs, qk_g, subln_g, bias_tiles, layer_idx):
    b, s, four_w = proj.shape
    width = four_w // 4
    hblocks = width // LANES
    tq = ATTN_TQ
    lam_init = 0.8 - 0.6 * math.exp(-0.3 * layer_idx)
    qkg = jnp.concatenate([qk_g, qk_g], axis=-1).astype(F32)
    return pl.pallas_call(
        functools.partial(_diff_attn_kernel, tq=tq, lam_init=lam_init),
        grid=(b, hblocks, s // tq),
        in_specs=[
            pl.BlockSpec((4, A_HEAD_DIM), lambda bi, h, i: (0, 0)),
            pl.BlockSpec((1, tq, LANES), lambda bi, h, i: (bi, 0, h)),
            pl.BlockSpec((1, tq, LANES),
                         lambda bi, h, i: (bi, jnp.minimum(i + 1, s // tq - 1), h)),
            pl.BlockSpec((1, s, LANES), lambda bi, h, i: (bi, 0, hblocks + h)),
            pl.BlockSpec((1, s, LANES), lambda bi, h, i: (bi, 0, 2 * hblocks + h)),
            pl.BlockSpec((1, tq, LANES), lambda bi, h, i: (bi, i, 3 * hblocks + h)),
            pl.BlockSpec((1, 1, 2 * tq, tq),
                         lambda bi, h, i: (h, jnp.minimum(i, 1), 0, 0)),
            pl.BlockSpec((2, LANES), lambda bi, h, i: (0, 0)),
            pl.BlockSpec((LANES, 1), lambda bi, h, i: (0, 0)),
        ],
        out_specs=pl.BlockSpec((1, tq, LANES), lambda bi, h, i: (bi, i, h)),
        out_shape=jax.ShapeDtypeStruct((b, s, width), BF16),
        scratch_shapes=[pltpu.VMEM((s, LANES), BF16),
                        pltpu.VMEM((s // tq, V_ROWS, tq), BF16),
                        pltpu.VMEM((2, 2 * tq, LANES), BF16),
                        pltpu.VMEM((1, V_ROWS, 2 * tq), F32),
                        pltpu.VMEM((1, 2, tq, 2 * tq), F32)],
        compiler_params=_cparams("parallel", "parallel", "arbitrary"),
        name="diff_attention",
    )(lam_vecs.astype(F32), proj, proj, proj, proj, proj, bias_tiles, qkg,
      subln_g.reshape(LANES, 1).astype(F32))


def _gla_kernel(q_ref, k_ref, v_ref, g_ref, lr_ref, wgt_ref, gb_ref, og_ref,
                cm_ref, o_ref, state_ref, *, ts):
    @pl.when(pl.program_id(2) == 0)
    def _():
        state_ref[...] = jnp.zeros_like(state_ref)

    dk = q_ref.shape[2]
    z_t = _nt_dot(wgt_ref[...], lr_ref[0]) + gb_ref[...]
    la_t = -_softplus(-z_t) * (1.0 / B_GATE_TAU)
    la_hi = la_t.astype(BF16)
    la_lo = (la_t - la_hi.astype(F32)).astype(BF16)

    span = cm_ref.shape[0]
    rest_t = jnp.concatenate(
        [jnp.dot(la_hi[:, s0:s0 + span], cm_ref[...], preferred_element_type=F32)
         + jnp.dot(la_lo[:, s0:s0 + span], cm_ref[...], preferred_element_type=F32)
         for s0 in range(0, ts, span)], axis=1)
    kd_t = (k_ref[0].astype(F32).T * jnp.exp(rest_t)).astype(BF16)

    lane = lax.broadcasted_iota(jnp.int32, (1, LANES), 1)
    first_chunk = lane < CHUNK
    state = state_ref[...]
    for c in range(ts // CHUNK):
        group = slice((c // 2) * LANES, (c // 2 + 1) * LANES)
        keep = first_chunk if c % 2 == 0 else jnp.logical_not(first_chunk)
        kd_c = jnp.where(keep, kd_t[:, group], jnp.zeros((), BF16))
        t0 = slice(c * CHUNK, c * CHUNK + 1)
        state = (jnp.exp(rest_t[:, t0] + la_t[:, t0]) * state
                 + jnp.dot(kd_c, v_ref[0, group, :], preferred_element_type=F32))
        rows = slice(c * CHUNK, (c + 1) * CHUNK)
        q = (q_ref[0, rows, :].astype(F32) * (dk ** -0.5)).astype(BF16)
        o = jnp.dot(q, state.astype(BF16), preferred_element_type=F32)
        ms = jnp.mean(o * o, axis=-1, keepdims=True)
        o = o * lax.rsqrt(ms + EPS) * og_ref[...]
        o_ref[0, rows, :] = (o * _silu(g_ref[0, rows, :].astype(F32))).astype(o_ref.dtype)
    state_ref[...] = state


def _gla(proj, lr, w_gate_t, gate_bias, out_g, *, ts):
    b, s, n = proj.shape
    heads = B_HEADS
    dk = n // (6 * heads)
    dv = 2 * dk
    kq = (heads * dk) // dk
    kv = (2 * heads * dk) // dv
    span = 2 * LANES
    t_in = jnp.arange(span)[:, None]
    t_out = jnp.arange(span)[None, :]
    chunk_masks = (((t_in // CHUNK) == (t_out // CHUNK)) & (t_in > t_out)).astype(BF16)
    return pl.pallas_call(
        functools.partial(_gla_kernel, ts=ts),
        grid=(b, heads, s // ts),
        in_specs=[
            pl.BlockSpec((1, ts, dk), lambda bi, h, i: (bi, i, h)),
            pl.BlockSpec((1, ts, dk), lambda bi, h, i: (bi, i, kq + h)),
            pl.BlockSpec((1, ts, dv), lambda bi, h, i: (bi, i, kv + h)),
            pl.BlockSpec((1, ts, dv), lambda bi, h, i: (bi, i, kv + heads + h)),
            pl.BlockSpec((1, ts, LANES), lambda bi, h, i: (bi, i, 0)),
            pl.BlockSpec((dk, LANES), lambda bi, h, i: (h, 0)),
            pl.BlockSpec((dk, 1), lambda bi, h, i: (h, 0)),
            pl.BlockSpec((1, dv), lambda bi, h, i: (0, 0)),
            pl.BlockSpec((span, span), lambda bi, h, i: (0, 0)),
        ],
        out_specs=pl.BlockSpec((1, ts, dv), lambda bi, h, i: (bi, i, h)),
        out_shape=jax.ShapeDtypeStruct((b, s, heads * dv), BF16),
        scratch_shapes=[pltpu.VMEM((dk, dv), F32)],
        compiler_params=_cparams("parallel", "parallel", "arbitrary"),
        name="gla",
    )(proj, proj, proj, proj, lr, w_gate_t, gate_bias.reshape(-1, 1).astype(F32),
      out_g.reshape(1, dv).astype(F32), chunk_masks)


def _rglru_kernel(u_ref, g_ref, cw_ref, cb_ref, wr_ref, br_ref, wi_ref, bi_ref,
                  lam_ref, o_ref, ubuf_ref, xc_ref, a_ref, b_ref, h_ref, *, ts):
    width = u_ref.shape[2]
    bd = width // C_BLOCKS

    @pl.when(pl.program_id(1) == 0)
    def _():
        ubuf_ref[0:SUBLANES, :] = jnp.zeros((SUBLANES, width), F32)
        h_ref[...] = jnp.zeros_like(h_ref)

    ubuf_ref[SUBLANES:SUBLANES + ts, :] = u_ref[0]
    ue = ubuf_ref[...]
    xc = ue * cw_ref[0:1, :]
    for t in range(1, C_CONV):
        xc = pltpu.roll(xc, 1, 0) + ue * cw_ref[t:t + 1, :]
    xc_ref[...] = xc[SUBLANES:, :] + cb_ref[...]
    ubuf_ref[0:SUBLANES, :] = ubuf_ref[ts:ts + SUBLANES, :]

    rate = -C_C * _softplus(-lam_ref[...])
    for n in range(C_BLOCKS):
        cols = slice(n * bd, (n + 1) * bd)
        xb = xc_ref[:, cols]
        xb16 = xb.astype(BF16)
        r = _sigmoid(jnp.dot(xb16, wr_ref[n], preferred_element_type=F32)
                     + br_ref[:, cols])
        gate_i = _sigmoid(jnp.dot(xb16, wi_ref[n], preferred_element_type=F32)
                          + bi_ref[:, cols])
        log_a = r * rate[:, cols]
        a = jnp.exp(log_a)
        a_ref[:, cols] = a
        b_ref[:, cols] = jnp.sqrt(_one_minus_exp(2.0 * log_a, a * a)) * (gate_i * xb)

    row = lax.broadcasted_iota(jnp.int32, (SUBLANES, width), 0)

    def tile(t, h_prev):
        rows = pl.ds(pl.multiple_of(t * SUBLANES, SUBLANES), SUBLANES)
        a = a_ref[rows, :]
        b = b_ref[rows, :]
        d = 1
        while d < SUBLANES:
            a_up = pltpu.roll(a, d, 0)
            b_up = pltpu.roll(b, d, 0)
            keep = row >= d
            b = jnp.where(keep, a * b_up + b, b)
            a = jnp.where(keep, a * a_up, a)
            d *= 2
        h = b + a * h_prev
        b_ref[rows, :] = h
        return h[SUBLANES - 1:SUBLANES, :]

    h_ref[...] = lax.fori_loop(0, ts // SUBLANES, tile, h_ref[...])
    o_ref[0] = (b_ref[...] * _silu(g_ref[0].astype(F32))).astype(o_ref.dtype)


def _rglru(u, gate, gate_block, conv_w, conv_b, w_rg, b_rg, w_ig, b_ig, lam, *, ts):
    b, s, width = u.shape
    bd = width // C_BLOCKS
    row = lambda a: a.reshape(1, width).astype(F32)
    const2 = lambda bi, i: (0, 0)
    return pl.pallas_call(
        functools.partial(_rglru_kernel, ts=ts),
        grid=(b, s // ts),
        in_specs=[
            pl.BlockSpec((1, ts, width), lambda bi, i: (bi, i, 0)),
            pl.BlockSpec((1, ts, width), lambda bi, i: (bi, i, gate_block)),
            pl.BlockSpec((C_CONV, width), const2),
            pl.BlockSpec((1, width), const2),
            pl.BlockSpec((C_BLOCKS, bd, bd), lambda bi, i: (0, 0, 0)),
            pl.BlockSpec((1, width), const2),
            pl.BlockSpec((C_BLOCKS, bd, bd), lambda bi, i: (0, 0, 0)),
            pl.BlockSpec((1, width), const2),
            pl.BlockSpec((1, width), const2),
        ],
        out_specs=pl.BlockSpec((1, ts, width), lambda bi, i: (bi, i, 0)),
        out_shape=jax.ShapeDtypeStruct((b, s, width), BF16),
        scratch_shapes=[pltpu.VMEM((ts + SUBLANES, width), F32),
                        pltpu.VMEM((ts, width), F32),
                        pltpu.VMEM((ts, width), F32),
                        pltpu.VMEM((ts, width), F32),
                        pltpu.VMEM((1, width), F32)],
        compiler_params=_cparams("parallel", "arbitrary"),
        name="rglru",
    )(u, gate, conv_w.astype(F32), row(conv_b), w_rg.astype(BF16), row(b_rg),
      w_ig.astype(BF16), row(b_ig), row(lam))


def _swap_rope_halves(x):
    lane = lax.broadcasted_iota(jnp.int32, (1, LANES), 1)
    first = (lane % D_ROPE) < (D_ROPE // 2)
    return jnp.where(first, pltpu.roll(x, LANES - D_ROPE // 2, 1),
                     pltpu.roll(x, D_ROPE // 2, 1))


def _rope(x, cos, sin_signed):
    return x * cos + _swap_rope_halves(x) * sin_signed


def _mla_attn_kernel(qn0_ref, qp0_ref, qnn_ref, qpn_ref, kn_ref, kp_ref, v_ref, g_ref,
                     cos_ref, sin_ref, gains_ref, o_ref, kcat_ref, vt_ref, qcat_ref,
                     acc_ref, s_ref, *, tq):
    i = pl.program_id(2)
    seq = kn_ref.shape[1]
    lane = lax.broadcasted_iota(jnp.int32, (1, LANES), 1)
    lo_mask = lane < D_ROPE
    g_q_nope, g_q_pe = gains_ref[0:1, :], gains_ref[1:2, :]
    g_k_nope, g_k_pe = gains_ref[2:3, :], gains_ref[3:4, :]
    scale = (D_NOPE + D_ROPE) ** -0.5 * LOG2E

    def store_q(slot, qn_rows, qp_rows, blk):
        rows = pl.ds(pl.multiple_of(blk * tq, tq), tq)
        qp = _rope(_group_rmsnorm(qp_rows.astype(F32), g_q_pe, D_ROPE),
                   cos_ref[rows, :], sin_ref[rows, :]) * scale
        qn = qn_rows.astype(F32)
        for hh in range(2):
            nope = _group_rmsnorm(qn[:, hh * LANES:(hh + 1) * LANES], g_q_nope, D_NOPE)
            qcat_ref[slot, hh, :, 0:LANES] = (nope * scale).astype(BF16)
            pe = jnp.where(lo_mask, qp, 0.0) if hh == 0 else jnp.where(lo_mask, 0.0, qp)
            qcat_ref[slot, hh, :, LANES:] = pe.astype(BF16)

    @pl.when(i == 0)
    def _():
        store_q(0, qn0_ref[0], qp0_ref[0], 0)
        def body(c, carry):
            rows = pl.ds(pl.multiple_of(c * tq, tq), tq)
            kp = _rope(_group_rmsnorm(kp_ref[0, rows, :].astype(F32), g_k_pe, D_ROPE),
                       cos_ref[rows, :], sin_ref[rows, :])
            kn = kn_ref[0, rows, :].astype(F32)
            kcat_ref[0, rows, 0:LANES] = _group_rmsnorm(
                kn[:, 0:LANES], g_k_nope, D_NOPE).astype(BF16)
            kcat_ref[0, rows, LANES:] = jnp.where(lo_mask, kp, 0.0).astype(BF16)
            kcat_ref[1, rows, 0:LANES] = _group_rmsnorm(
                kn[:, LANES:], g_k_nope, D_NOPE).astype(BF16)
            kcat_ref[1, rows, LANES:] = jnp.where(lo_mask, 0.0, kp).astype(BF16)
            _store_vt_block(vt_ref, (0, c), v_ref[0, rows, 0:D_V])
            _store_vt_block(vt_ref, (1, c), v_ref[0, rows, D_V:])
            return carry
        lax.fori_loop(0, seq // tq, body, 0)

    qcat = (qcat_ref[i % 2, 0], qcat_ref[i % 2, 1])
    store_q((i + 1) % 2, qnn_ref[0], qpn_ref[0], jnp.minimum(i + 1, seq // tq - 1))

    def near_add(j):
        c = lax.broadcasted_iota(jnp.int32, (tq, tq), 0) + tq * (j - jnp.minimum(i, 1))
        r = lax.broadcasted_iota(jnp.int32, (tq, tq), 1)
        visible = (lax.shift_right_arithmetic(c, 6) <= lax.shift_right_arithmetic(r, 6))
        return jnp.where(visible, 0.0, NEG_INF).astype(F32)

    def stream(hh):
        def scores(blk):
            rows = pl.ds(pl.multiple_of(blk * tq, tq), tq)
            return _nt_dot(kcat_ref[hh, rows, :], qcat[hh])
        return scores, lambda blk: vt_ref[hh, blk], near_add

    _chunk_causal_flash(i, [stream(0), stream(1)], acc_ref, s_ref)

    g = g_ref[0].astype(F32)
    o_ref[0, :, 0:D_V] = (_normalised(acc_ref[0]).T * _silu(g[:, 0:D_V])).astype(o_ref.dtype)
    o_ref[0, :, D_V:] = (_normalised(acc_ref[1]).T * _silu(g[:, D_V:])).astype(o_ref.dtype)


def _mla_attention(q, kv, kpe, proj, gate_block0, cos, sin_signed, gains):
    b, s, _ = q.shape
    heads = D_HEADS
    pairs = heads // 2
    tq = ATTN_TQ
    pw = 2 * LANES
    nxt = lambda i: jnp.minimum(i + 1, s // tq - 1)
    return pl.pallas_call(
        functools.partial(_mla_attn_kernel, tq=tq),
        grid=(b, pairs, s // tq),
        in_specs=[
            pl.BlockSpec((1, tq, pw), lambda bi, p, i: (bi, 0, p)),
            pl.BlockSpec((1, tq, LANES), lambda bi, p, i: (bi, 0, 2 * pairs + p)),
            pl.BlockSpec((1, tq, pw), lambda bi, p, i: (bi, nxt(i), p)),
            pl.BlockSpec((1, tq, LANES), lambda bi, p, i: (bi, nxt(i), 2 * pairs + p)),
            pl.BlockSpec((1, s, pw), lambda bi, p, i: (bi, 0, p)),
            pl.BlockSpec((1, s, LANES), lambda bi, p, i: (bi, 0, 0)),
            pl.BlockSpec((1, s, pw), lambda bi, p, i: (bi, 0, pairs + p)),
            pl.BlockSpec((1, tq, pw), lambda bi, p, i: (bi, i, gate_block0 + p)),
            pl.BlockSpec((s, LANES), lambda bi, p, i: (0, 0)),
            pl.BlockSpec((s, LANES), lambda bi, p, i: (0, 0)),
            pl.BlockSpec((4, LANES), lambda bi, p, i: (0, 0)),
        ],
        out_specs=pl.BlockSpec((1, tq, pw), lambda bi, p, i: (bi, i, p)),
        out_shape=jax.ShapeDtypeStruct((b, s, heads * D_V), BF16),
        scratch_shapes=[pltpu.VMEM((2, s, pw), BF16),
                        pltpu.VMEM((2, s // tq, V_ROWS, tq), BF16),
                        pltpu.VMEM((2, 2, tq, pw), BF16),
                        pltpu.VMEM((2, V_ROWS, tq), F32),
                        pltpu.VMEM((2, 2, tq, tq), F32)],
        compiler_params=_cparams("parallel", "parallel", "arbitrary"),
        name="mla_attention",
    )(q, q, q, q, kv, kpe, kv, proj, cos, sin_signed, gains)


MM_TM = 1024
MM_TN = 1024
RES_TM = 512


def _layer_diff(x2, b, s, norm_g, w_in, qk_g, lam_vecs, subln_g, w_out, bias_tiles,
                layer_idx):
    proj = _norm_matmul(x2, norm_g, w_in, out_dtype=BF16, tm=MM_TM, tn=MM_TN)
    y = _diff_attention(proj.reshape(b, s, -1), lam_vecs, qk_g, subln_g,
                        bias_tiles, layer_idx)
    return _matmul_residual(y.reshape(b * s, -1), w_out.astype(BF16), x2,
                            tm=RES_TM)


def _layer_gla(x2, b, s, norm_g, w_in, w_gate, gate_bias, out_g, w_out):
    rank, hdk = w_gate.shape
    n_main = w_in.shape[1] - rank
    w_lr = jnp.pad(w_in[:, n_main:], ((0, 0), (0, LANES - rank))).astype(BF16)
    w_gate_t = jnp.pad(w_gate, ((0, LANES - rank), (0, 0))).T.astype(BF16)
    proj, lr = _norm_matmul(x2, norm_g, w_in, out_dtype=BF16, tm=MM_TM, tn=MM_TN,
                            n_cols=n_main, w_side=w_lr)
    y = _gla(proj.reshape(b, s, -1), lr.reshape(b, s, LANES), w_gate_t, gate_bias,
             out_g, ts=512)
    return _matmul_residual(y.reshape(b * s, -1), w_out.astype(BF16), x2,
                            tm=RES_TM)


def _layer_rglru(x2, b, s, norm_g, w_in, conv_w, conv_b, w_rg, b_rg, w_ig, b_ig,
                 lam, w_out):
    width = w_in.shape[1] // 2
    u, gate = _norm_matmul(x2, norm_g, w_in.astype(BF16), out_dtype=BF16,
                           tm=MM_TM, tn=MM_TN, first=(width, F32))
    y = _rglru(u.reshape(b, s, width), gate.reshape(b, s, width), 0, conv_w, conv_b,
               w_rg, b_rg, w_ig, b_ig, lam, ts=256)
    return _matmul_residual(y.reshape(b * s, -1), w_out.astype(BF16), x2,
                            tm=RES_TM)


def _rope_tables(s):
    half = D_ROPE // 2
    inv = ROPE_THETA ** (-jnp.arange(half, dtype=F32) / half)
    ang = jnp.arange(s, dtype=F32)[:, None] * inv[None, :]
    cos, sin = jnp.cos(ang), jnp.sin(ang)
    return (jnp.concatenate([cos, cos, cos, cos], axis=-1),
            jnp.concatenate([-sin, sin, -sin, sin], axis=-1))


def _layer_mla(x2, b, s, norm_g, w_in, q_lat_g, kv_lat_g, w_uq, w_ukv, qk_g, w_out):
    q_rank, kv_rank = q_lat_g.shape[0], kv_lat_g.shape[0]
    heads = D_HEADS
    lat = q_rank + kv_rank
    w_main = jnp.concatenate([w_in[:, :lat], w_in[:, lat + D_ROPE:]], axis=1)
    w_kpe = w_in[:, lat:lat + D_ROPE]
    w_kpe = jnp.concatenate([w_kpe, w_kpe], axis=1).astype(BF16)
    uq = w_uq.reshape(q_rank, heads, D_NOPE + D_ROPE)
    w_uq_p = jnp.concatenate([uq[:, :, :D_NOPE].reshape(q_rank, -1),
                              uq[:, :, D_NOPE:].reshape(q_rank, -1)], axis=1).astype(BF16)
    ukv = w_ukv.reshape(kv_rank, heads, D_NOPE + D_V)
    w_ukv_p = jnp.concatenate([ukv[:, :, :D_NOPE].reshape(kv_rank, -1),
                               ukv[:, :, D_NOPE:].reshape(kv_rank, -1)], axis=1).astype(BF16)
    dup = lambda v: jnp.concatenate([v, v])
    gains = jnp.stack([qk_g[0, :D_NOPE], dup(qk_g[0, D_NOPE:]),
                       qk_g[1, :D_NOPE], dup(qk_g[1, D_NOPE:])]).astype(F32)
    cos, sin_signed = _rope_tables(s)

    proj, kpe = _norm_matmul(x2, norm_g, w_main, out_dtype=BF16, tm=MM_TM, tn=MM_TN,
                             w_side=w_kpe)
    q = _norm_matmul(proj, q_lat_g, w_uq_p, out_dtype=BF16, tm=MM_TM,
                     tn=w_uq_p.shape[1], col_block=0)
    kv = _norm_matmul(proj, kv_lat_g, w_ukv_p, out_dtype=BF16, tm=MM_TM,
                      tn=w_ukv_p.shape[1], col_block=1)
    gate_block0 = lat // (2 * LANES)
    y = _mla_attention(q.reshape(b, s, -1), kv.reshape(b, s, -1),
                       kpe.reshape(b, s, LANES), proj.reshape(b, s, -1),
                       gate_block0, cos, sin_signed, gains)
    return _matmul_residual(y.reshape(b * s, -1), w_out.astype(BF16), x2,
                            tm=RES_TM)


def kernel(x, norm_g, rel_bias, a_w_in, a_qk_g, a_lambda, a_subln_g, a_w_out, b_w_in, b_w_gate, b_gate_bias, b_out_g, b_w_out, c_w_in, c_conv_w, c_conv_b, c_w_rgate, c_b_rgate, c_w_igate, c_b_igate, c_lambda, c_w_out, d_w_in, d_q_lat_g, d_kv_lat_g, d_w_uq, d_w_ukv, d_qk_g, d_w_out):
    b, s, d = x.shape
    depth = norm_g.shape[0]
    x2 = x.reshape(b * s, d)
    bias_tiles = _t5_tiles(rel_bias, ATTN_TQ)
    for i in range(depth):
        m, j = i % 4, i // 4
        if m == 0:
            x2 = _layer_diff(x2, b, s, norm_g[i], a_w_in[j], a_qk_g[j], a_lambda[j],
                             a_subln_g[j], a_w_out[j], bias_tiles, i)
        elif m == 1:
            x2 = _layer_gla(x2, b, s, norm_g[i], b_w_in[j], b_w_gate[j],
                            b_gate_bias[j], b_out_g[j], b_w_out[j])
        elif m == 2:
            x2 = _layer_rglru(x2, b, s, norm_g[i], c_w_in[j], c_conv_w[j], c_conv_b[j],
                              c_w_rgate[j], c_b_rgate[j], c_w_igate[j], c_b_igate[j],
                              c_lambda[j], c_w_out[j])
        else:
            x2 = _layer_mla(x2, b, s, norm_g[i], d_w_in[j], d_q_lat_g[j],
                            d_kv_lat_g[j], d_w_uq[j], d_w_ukv[j], d_qk_g[j], d_w_out[j])
    return x2.reshape(b, s, d)
```

```python
import functools
import math

import jax
import jax.numpy as jnp
from jax import lax
from jax.experimental import pallas as pl
from jax.experimental.pallas import tpu as pltpu

F32 = jnp.float32
BF16 = jnp.bfloat16

EPS = 1e-6
NEG_INF = -1e30
LOG2E = math.log2(math.e)
CHUNK = 64
LANES = 128
SUBLANES = 8
BF16_ROWS = 16
V7X_VMEM_BYTES = 64 * 1024 * 1024
VMEM_LIMIT = V7X_VMEM_BYTES - 8 * 1024 * 1024

A_HEADS = 16
A_HEAD_DIM = 64
T5_BUCKETS = 32
B_HEADS = 4
B_GATE_TAU = 16.0
C_BLOCKS = 8
C_CONV = 4
C_C = 8.0
D_HEADS = 16
D_NOPE = 128
D_ROPE = 64
D_V = 128
ROPE_THETA = 10000.0

ATTN_TQ = 512
V_ROWS = LANES + BF16_ROWS
FAR_UNROLL = 4


def _cparams(*semantics):
    return pltpu.CompilerParams(dimension_semantics=semantics,
                                vmem_limit_bytes=VMEM_LIMIT)


def _sigmoid(x):
    return 0.5 * jnp.tanh(0.5 * x) + 0.5


def _silu(g):
    return g * _sigmoid(g)


def _softplus(x):
    return jnp.maximum(x, 0.0) + jnp.log(1.0 + jnp.exp(-jnp.abs(x)))


_EXPM1_SERIES_BOUND = 0.1
_EXPM1_SERIES_TERMS = 5


def _one_minus_exp(x, exp_x):
    poly = jnp.full_like(x, 1.0 / math.factorial(_EXPM1_SERIES_TERMS))
    for n in range(_EXPM1_SERIES_TERMS - 1, 0, -1):
        poly = poly * x + 1.0 / math.factorial(n)
    return jnp.where(x > -_EXPM1_SERIES_BOUND, -x * poly, 1.0 - exp_x)


def _nt_dot(a, b):
    return lax.dot_general(a, b, (((1,), (1,)), ((), ())),
                           preferred_element_type=F32)


def _norm_matmul_kernel(*refs, has_side, n_first):
    refs = list(refs)
    x_ref, g_ref, w_ref = refs[:3]
    del refs[:3]
    ws_ref = refs.pop(0) if has_side else None
    o_ref = refs.pop(0)
    rest_ref = refs.pop(0) if n_first is not None else None
    side_ref = refs.pop(0) if has_side else None
    h_ref = refs.pop(0)
    j = pl.program_id(1)

    @pl.when(j == 0)
    def _():
        x = x_ref[...].astype(F32)
        ms = jnp.mean(x * x, axis=-1, keepdims=True)
        h_ref[...] = (x * lax.rsqrt(ms + EPS) * g_ref[...]).astype(BF16)
        if has_side:
            side_ref[...] = jnp.dot(h_ref[...], ws_ref[...],
                                    preferred_element_type=F32).astype(side_ref.dtype)

    w = w_ref[...]
    if w.dtype != BF16:
        w = w.astype(BF16)
    res = jnp.dot(h_ref[...], w, preferred_element_type=F32)
    if n_first is None:
        o_ref[...] = res.astype(o_ref.dtype)
    else:
        @pl.when(j < n_first)
        def _():
            o_ref[...] = res.astype(o_ref.dtype)

        @pl.when(j >= n_first)
        def _():
            rest_ref[...] = res.astype(rest_ref.dtype)


def _norm_matmul(x, gain, w, *, out_dtype, tm, tn, col_block=0, n_cols=None,
                 w_side=None, first=None):
    m = x.shape[0]
    k = w.shape[0]
    n = w.shape[1] if n_cols is None else n_cols
    n_first = None if first is None else first[0] // tn
    in_specs = [pl.BlockSpec((tm, k), lambda i, j: (i, col_block)),
                pl.BlockSpec((1, k), lambda i, j: (0, 0)),
                pl.BlockSpec((k, tn), lambda i, j: (0, j))]
    args = [x, gain.reshape(1, k).astype(F32), w]
    if w_side is not None:
        in_specs.append(pl.BlockSpec((k, LANES), lambda i, j: (0, 0)))
        args.append(w_side)
    if first is None:
        out_specs = [pl.BlockSpec((tm, tn), lambda i, j: (i, j))]
        out_shape = [jax.ShapeDtypeStruct((m, n), out_dtype)]
    else:
        out_specs = [pl.BlockSpec((tm, tn), lambda i, j: (i, jnp.minimum(j, n_first - 1))),
                     pl.BlockSpec((tm, tn), lambda i, j: (i, jnp.maximum(j - n_first, 0)))]
        out_shape = [jax.ShapeDtypeStruct((m, first[0]), first[1]),
                     jax.ShapeDtypeStruct((m, n - first[0]), out_dtype)]
    if w_side is not None:
        out_specs.append(pl.BlockSpec((tm, LANES), lambda i, j: (i, 0)))
        out_shape.append(jax.ShapeDtypeStruct((m, LANES), BF16))
    outs = pl.pallas_call(
        functools.partial(_norm_matmul_kernel, has_side=w_side is not None,
                          n_first=n_first),
        grid=(m // tm, n // tn),
        in_specs=in_specs,
        out_specs=out_specs,
        out_shape=out_shape,
        scratch_shapes=[pltpu.VMEM((tm, k), BF16)],
        compiler_params=_cparams("parallel", "arbitrary"),
        name="norm_matmul",
    )(*args)
    return outs[0] if len(outs) == 1 else tuple(outs)


def _matmul_residual_kernel(y_ref, w_ref, x_ref, o_ref):
    o_ref[...] = x_ref[...] + jnp.dot(y_ref[...], w_ref[...],
                                      preferred_element_type=F32)


def _matmul_residual(y, w, x, *, tm):
    m, k = y.shape
    n = w.shape[1]
    return pl.pallas_call(
        _matmul_residual_kernel,
        grid=(m // tm,),
        in_specs=[pl.BlockSpec((tm, k), lambda i: (i, 0)),
                  pl.BlockSpec((k, n), lambda i: (0, 0)),
                  pl.BlockSpec((tm, n), lambda i: (i, 0))],
        out_specs=pl.BlockSpec((tm, n), lambda i: (i, 0)),
        out_shape=jax.ShapeDtypeStruct((m, n), F32),
        compiler_params=_cparams("parallel"),
        name="matmul_residual",
    )(y, w, x)


def _group_rmsnorm(x, gain, group):
    r = lax.broadcasted_iota(jnp.int32, (LANES, LANES), 0)
    c = lax.broadcasted_iota(jnp.int32, (LANES, LANES), 1)
    ones = jnp.where(r // group == c // group, 1.0, 0.0).astype(BF16)
    x2 = x * x
    hi = x2.astype(BF16)
    lo = (x2 - hi.astype(F32)).astype(BF16)
    ss = (jnp.dot(hi, ones, preferred_element_type=F32)
          + jnp.dot(lo, ones, preferred_element_type=F32))
    return x * lax.rsqrt(ss * (1.0 / group) + EPS) * gain


def _half_rmsnorm(x, gain, lo_mask):
    x2 = x * x
    lo = jnp.sum(jnp.where(lo_mask, x2, 0.0), axis=-1, keepdims=True)
    hi = jnp.sum(jnp.where(lo_mask, 0.0, x2), axis=-1, keepdims=True)
    ms = jnp.where(lo_mask, lo, hi) * (1.0 / 64)
    return x * lax.rsqrt(ms + EPS) * gain


def _store_vt_block(vt_ref, lead, v_rows):
    tq = v_rows.shape[0]
    r = lax.broadcasted_iota(jnp.int32, (LANES, LANES), 0)
    c = lax.broadcasted_iota(jnp.int32, (LANES, LANES), 1)
    eye = jnp.where(r == c, 1.0, 0.0).astype(BF16)
    vt_ref[(*lead, slice(0, LANES), slice(None))] = _nt_dot(eye, v_rows).astype(BF16)
    vt_ref[(*lead, slice(LANES, V_ROWS), slice(None))] = jnp.ones((BF16_ROWS, tq), BF16)


def _flash_init(s, vt):
    m = jnp.max(s, axis=0, keepdims=True)
    p = jnp.exp2(s - m).astype(BF16)
    return m, jnp.dot(vt, p, preferred_element_type=F32)


def _flash_update(s, vt, m, acc):
    m_new = jnp.maximum(m, jnp.max(s, axis=0, keepdims=True))
    alpha = jnp.exp2(m - m_new)
    p = jnp.exp2(s - m_new).astype(BF16)
    return m_new, alpha * acc + jnp.dot(vt, p, preferred_element_type=F32)


def _chunk_causal_flash(i, streams, acc_ref, s_ref):
    first = jnp.maximum(i - 1, 0)
    n_far = first

    def issue(blk, slot, add=None):
        for si, (scores, _, near_add) in enumerate(streams):
            s = scores(blk)
            s_ref[si, slot] = s if add is None else s + near_add(add)

    def update(blk, slot, ms):
        out = []
        for si, (_, values, _) in enumerate(streams):
            m, acc_ref[si] = _flash_update(s_ref[si, slot], values(blk), ms[si],
                                           acc_ref[si])
            out.append(m)
        return tuple(out)

    def group(t0, count, prefetch_next, ms):
        for k in range(count):
            if k + 1 < count or prefetch_next:
                issue(t0 + k + 1, (k + 1) % 2)
            ms = update(t0 + k, k % 2, ms)
        return ms

    issue(first, 0, add=0)
    issue(first + 1, 1, add=1)
    ms = []
    for si, (_, values, _) in enumerate(streams):
        m, acc_ref[si] = _flash_init(s_ref[si, 0], values(first))
        ms.append(m)
    issue(0, 0)
    ms = update(first + 1, 1, tuple(ms))

    full = jnp.maximum(n_far - 1, 0) // FAR_UNROLL
    rest = n_far - FAR_UNROLL * full
    ms = lax.fori_loop(0, full,
                       lambda g, ms: group(g * FAR_UNROLL, FAR_UNROLL, True, ms), ms)
    for count in range(1, FAR_UNROLL + 1):
        ms = lax.fori_loop(0, (rest == count).astype(jnp.int32),
                           lambda _, ms, count=count: group(n_far - count, count, False, ms),
                           ms)


def _normalised(acc):
    return acc[0:LANES] * (1.0 / acc[LANES:LANES + 1])


_T5_LARGE_THRESHOLDS = (12, 16, 23, 32, 46, 64, 91)


def _t5_tiles_kernel(rb_ref, o_ref, *, tq):
    h = pl.program_id(0)
    variant = pl.program_id(1)
    j = lax.broadcasted_iota(jnp.int32, (1, tq), 1)
    rel_row = jnp.where(j < tq // 2, -j, tq - j)
    n = jnp.abs(rel_row)
    large = jnp.full_like(n, 8)
    for t in _T5_LARGE_THRESHOLDS:
        large = large + (n >= t).astype(jnp.int32)
    bucket = jnp.where(rel_row > 0, T5_BUCKETS // 2, 0) + jnp.where(n < 8, n, large)
    far = rb_ref[T5_BUCKETS // 2 - 1, h]
    row = jnp.zeros((1, tq), F32)
    for b in range(T5_BUCKETS):
        row = jnp.where(bucket == b, (rb_ref[b, h] - far) * LOG2E, row)
    tile = pltpu.roll(jnp.broadcast_to(row, (2 * tq, tq)), 0, 1, stride=1, stride_axis=0)
    c = lax.broadcasted_iota(jnp.int32, (2 * tq, tq), 0)
    r = lax.broadcasted_iota(jnp.int32, (2 * tq, tq), 1)
    key_off = c - tq * variant
    bias = jnp.where(key_off - r > -_T5_LARGE_THRESHOLDS[-1], tile, 0.0)
    visible = (lax.shift_right_arithmetic(key_off, 6)
               <= lax.shift_right_arithmetic(r, 6))
    o_ref[0, 0] = jnp.where(visible, bias, NEG_INF)


def _t5_tiles(rel_bias, tq):
    nb, nh = rel_bias.shape
    return pl.pallas_call(
        functools.partial(_t5_tiles_kernel, tq=tq),
        grid=(nh, 2),
        in_specs=[pl.BlockSpec(memory_space=pltpu.SMEM)],
        out_specs=pl.BlockSpec((1, 1, 2 * tq, tq), lambda h, v: (h, v, 0, 0)),
        out_shape=jax.ShapeDtypeStruct((nh, 2, 2 * tq, tq), F32),
        compiler_params=_cparams("parallel", "arbitrary"),
        name="t5_tiles",
    )(rel_bias.astype(F32))


def _diff_attn_kernel(lamv_ref, q_ref, k_ref, v_ref, g_ref, bias_ref, qkg_ref,
                      sub_ref, o_ref, kn_ref, vt_ref, acc_ref, s_ref, *, tq, lam_init):
    i = pl.program_id(2)
    seq = k_ref.shape[1]
    lane = lax.broadcasted_iota(jnp.int32, (1, LANES), 1)
    lo_mask = lane < A_HEAD_DIM

    @pl.when(i == 0)
    def _():
        def body(c, carry):
            rows = pl.ds(pl.multiple_of(c * tq, tq), tq)
            kn_ref[rows, :] = _group_rmsnorm(
                k_ref[0, rows, :].astype(F32), qkg_ref[1:2, :], A_HEAD_DIM).astype(BF16)
            _store_vt_block(vt_ref, (c,), v_ref[0, rows, :])
            return carry
        lax.fori_loop(0, seq // tq, body, 0, unroll=2)

    lf = lamv_ref[...].astype(F32)
    lam = (jnp.exp(jnp.sum(lf[0:1] * lf[1:2], axis=-1, keepdims=True))
           - jnp.exp(jnp.sum(lf[2:3] * lf[3:4], axis=-1, keepdims=True)) + lam_init)

    qn = (_half_rmsnorm(q_ref[0].astype(F32), qkg_ref[0:1, :], lo_mask)
          * (A_HEAD_DIM ** -0.5 * LOG2E))
    q01 = jnp.concatenate([jnp.where(lo_mask, qn, 0.0).astype(BF16),
                           jnp.where(lo_mask, 0.0, qn).astype(BF16)], axis=0)

    def scores(blk):
        return _nt_dot(kn_ref[pl.ds(pl.multiple_of(blk * tq, tq), tq), :], q01)

    def near_add(j):
        bias_t = bias_ref[0, 0, j * tq:(j + 1) * tq, :]
        return jnp.concatenate([bias_t, bias_t], axis=1)

    _chunk_causal_flash(i, [(scores, lambda blk: vt_ref[blk], near_add)], acc_ref,
                        s_ref)

    o01 = _normalised(acc_ref[0])
    o_t = o01[:, 0:tq] - lam * o01[:, tq:]
    ms_t = jnp.mean(o_t * o_t, axis=0, keepdims=True)
    o_t = o_t * lax.rsqrt(ms_t + EPS) * sub_ref[...] * (1.0 - lam_init)
    o_ref[0] = (o_t.T * _silu(g_ref[0].astype(F32))).astype(o_ref.dtype)


def _diff_attention(proj, lam_vecs, qk_g, subln_g, bias_tiles, layer_idx):
    b, s, four_w = proj.shape
    width = four_w // 4
    hblocks = width // LANES
    tq = ATTN_TQ
    lam_init = 0.8 - 0.6 * math.exp(-0.3 * layer_idx)
    qkg = jnp.concatenate([qk_g, qk_g], axis=-1).astype(F32)
    return pl.pallas_call(
        functools.partial(_diff_attn_kernel, tq=tq, lam_init=lam_init),
        grid=(b, hblocks, s // tq),
        in_specs=[
            pl.BlockSpec((4, A_HEAD_DIM), lambda bi, h, i: (0, 0)),
            pl.BlockSpec((1, tq, LANES), lambda bi, h, i: (bi, i, h)),
            pl.BlockSpec((1, s, LANES), lambda bi, h, i: (bi, 0, hblocks + h)),
            pl.BlockSpec((1, s, LANES), lambda bi, h, i: (bi, 0, 2 * hblocks + h)),
            pl.BlockSpec((1, tq, LANES), lambda bi, h, i: (bi, i, 3 * hblocks + h)),
            pl.BlockSpec((1, 1, 2 * tq, tq),
                         lambda bi, h, i: (h, jnp.minimum(i, 1), 0, 0)),
            pl.BlockSpec((2, LANES), lambda bi, h, i: (0, 0)),
            pl.BlockSpec((LANES, 1), lambda bi, h, i: (0, 0)),
        ],
        out_specs=pl.BlockSpec((1, tq, LANES), lambda bi, h, i: (bi, i, h)),
        out_shape=jax.ShapeDtypeStruct((b, s, width), BF16),
        scratch_shapes=[pltpu.VMEM((s, LANES), BF16),
                        pltpu.VMEM((s // tq, V_ROWS, tq), BF16),
                        pltpu.VMEM((1, V_ROWS, 2 * tq), F32),
                        pltpu.VMEM((1, 2, tq, 2 * tq), F32)],
        compiler_params=_cparams("parallel", "parallel", "arbitrary"),
        name="diff_attention",
    )(lam_vecs.astype(F32), proj, proj, proj, proj, bias_tiles, qkg,
      subln_g.reshape(LANES, 1).astype(F32))


def _gla_kernel(q_ref, k_ref, v_ref, g_ref, lr_ref, wgt_ref, gb_ref, og_ref,
                cm_ref, o_ref, state_ref, *, ts):
    @pl.when(pl.program_id(2) == 0)
    def _():
        state_ref[...] = jnp.zeros_like(state_ref)

    dk = q_ref.shape[2]
    z_t = _nt_dot(wgt_ref[...], lr_ref[0]) + gb_ref[...]
    la_t = -_softplus(-z_t) * (1.0 / B_GATE_TAU)
    la_hi = la_t.astype(BF16)
    la_lo = (la_t - la_hi.astype(F32)).astype(BF16)

    span = cm_ref.shape[0]
    rest_t = jnp.concatenate(
        [jnp.dot(la_hi[:, s0:s0 + span], cm_ref[...], preferred_element_type=F32)
         + jnp.dot(la_lo[:, s0:s0 + span], cm_ref[...], preferred_element_type=F32)
         for s0 in range(0, ts, span)], axis=1)
    kd_t = (k_ref[0].astype(F32).T * jnp.exp(rest_t)).astype(BF16)

    lane = lax.broadcasted_iota(jnp.int32, (1, LANES), 1)
    first_chunk = lane < CHUNK
    state = state_ref[...]
    for c in range(ts // CHUNK):
        group = slice((c // 2) * LANES, (c // 2 + 1) * LANES)
        keep = first_chunk if c % 2 == 0 else jnp.logical_not(first_chunk)
        kd_c = jnp.where(keep, kd_t[:, group], jnp.zeros((), BF16))
        t0 = slice(c * CHUNK, c * CHUNK + 1)
        state = (jnp.exp(rest_t[:, t0] + la_t[:, t0]) * state
                 + jnp.dot(kd_c, v_ref[0, group, :], preferred_element_type=F32))
        rows = slice(c * CHUNK, (c + 1) * CHUNK)
        q = (q_ref[0, rows, :].astype(F32) * (dk ** -0.5)).astype(BF16)
        o = jnp.dot(q, state.astype(BF16), preferred_element_type=F32)
        ms = jnp.mean(o * o, axis=-1, keepdims=True)
        o = o * lax.rsqrt(ms + EPS) * og_ref[...]
        o_ref[0, rows, :] = (o * _silu(g_ref[0, rows, :].astype(F32))).astype(o_ref.dtype)
    state_ref[...] = state


def _gla(proj, lr, w_gate_t, gate_bias, out_g, *, ts):
    b, s, n = proj.shape
    heads = B_HEADS
    dk = n // (6 * heads)
    dv = 2 * dk
    kq = (heads * dk) // dk
    kv = (2 * heads * dk) // dv
    span = 2 * LANES
    t_in = jnp.arange(span)[:, None]
    t_out = jnp.arange(span)[None, :]
    chunk_masks = (((t_in // CHUNK) == (t_out // CHUNK)) & (t_in > t_out)).astype(BF16)
    return pl.pallas_call(
        functools.partial(_gla_kernel, ts=ts),
        grid=(b, heads, s // ts),
        in_specs=[
            pl.BlockSpec((1, ts, dk), lambda bi, h, i: (bi, i, h)),
            pl.BlockSpec((1, ts, dk), lambda bi, h, i: (bi, i, kq + h)),
            pl.BlockSpec((1, ts, dv), lambda bi, h, i: (bi, i, kv + h)),
            pl.BlockSpec((1, ts, dv), lambda bi, h, i: (bi, i, kv + heads + h)),
            pl.BlockSpec((1, ts, LANES), lambda bi, h, i: (bi, i, 0)),
            pl.BlockSpec((dk, LANES), lambda bi, h, i: (h, 0)),
            pl.BlockSpec((dk, 1), lambda bi, h, i: (h, 0)),
            pl.BlockSpec((1, dv), lambda bi, h, i: (0, 0)),
            pl.BlockSpec((span, span), lambda bi, h, i: (0, 0)),
        ],
        out_specs=pl.BlockSpec((1, ts, dv), lambda bi, h, i: (bi, i, h)),
        out_shape=jax.ShapeDtypeStruct((b, s, heads * dv), BF16),
        scratch_shapes=[pltpu.VMEM((dk, dv), F32)],
        compiler_params=_cparams("parallel", "parallel", "arbitrary"),
        name="gla",
    )(proj, proj, proj, proj, lr, w_gate_t, gate_bias.reshape(-1, 1).astype(F32),
      out_g.reshape(1, dv).astype(F32), chunk_masks)


def _rglru_kernel(u_ref, g_ref, cw_ref, cb_ref, wr_ref, br_ref, wi_ref, bi_ref,
                  lam_ref, o_ref, ubuf_ref, xc_ref, a_ref, b_ref, h_ref, *, ts):
    width = u_ref.shape[2]
    bd = width // C_BLOCKS

    @pl.when(pl.program_id(1) == 0)
    def _():
        ubuf_ref[0:SUBLANES, :] = jnp.zeros((SUBLANES, width), F32)
        h_ref[...] = jnp.zeros_like(h_ref)

    ubuf_ref[SUBLANES:SUBLANES + ts, :] = u_ref[0]
    ue = ubuf_ref[...]
    xc = ue * cw_ref[0:1, :]
    for t in range(1, C_CONV):
        xc = pltpu.roll(xc, 1, 0) + ue * cw_ref[t:t + 1, :]
    xc_ref[...] = xc[SUBLANES:, :] + cb_ref[...]
    ubuf_ref[0:SUBLANES, :] = ubuf_ref[ts:ts + SUBLANES, :]

    rate = -C_C * _softplus(-lam_ref[...])
    for n in range(C_BLOCKS):
        cols = slice(n * bd, (n + 1) * bd)
        xb = xc_ref[:, cols]
        xb16 = xb.astype(BF16)
        r = _sigmoid(jnp.dot(xb16, wr_ref[n], preferred_element_type=F32)
                     + br_ref[:, cols])
        gate_i = _sigmoid(jnp.dot(xb16, wi_ref[n], preferred_element_type=F32)
                          + bi_ref[:, cols])
        log_a = r * rate[:, cols]
        a = jnp.exp(log_a)
        a_ref[:, cols] = a
        b_ref[:, cols] = jnp.sqrt(_one_minus_exp(2.0 * log_a, a * a)) * (gate_i * xb)

    row = lax.broadcasted_iota(jnp.int32, (SUBLANES, width), 0)

    def tile(t, h_prev):
        rows = pl.ds(pl.multiple_of(t * SUBLANES, SUBLANES), SUBLANES)
        a = a_ref[rows, :]
        b = b_ref[rows, :]
        d = 1
        while d < SUBLANES:
            a_up = pltpu.roll(a, d, 0)
            b_up = pltpu.roll(b, d, 0)
            keep = row >= d
            b = jnp.where(keep, a * b_up + b, b)
            a = jnp.where(keep, a * a_up, a)
            d *= 2
        h = b + a * h_prev
        b_ref[rows, :] = h
        return h[SUBLANES - 1:SUBLANES, :]

    h_ref[...] = lax.fori_loop(0, ts // SUBLANES, tile, h_ref[...])
    o_ref[0] = (b_ref[...] * _silu(g_ref[0].astype(F32))).astype(o_ref.dtype)


def _rglru(u, gate, gate_block, conv_w, conv_b, w_rg, b_rg, w_ig, b_ig, lam, *, ts):
    b, s, width = u.shape
    bd = width // C_BLOCKS
    row = lambda a: a.reshape(1, width).astype(F32)
    const2 = lambda bi, i: (0, 0)
    return pl.pallas_call(
        functools.partial(_rglru_kernel, ts=ts),
        grid=(b, s // ts),
        in_specs=[
            pl.BlockSpec((1, ts, width), lambda bi, i: (bi, i, 0)),
            pl.BlockSpec((1, ts, width), lambda bi, i: (bi, i, gate_block)),
            pl.BlockSpec((C_CONV, width), const2),
            pl.BlockSpec((1, width), const2),
            pl.BlockSpec((C_BLOCKS, bd, bd), lambda bi, i: (0, 0, 0)),
            pl.BlockSpec((1, width), const2),
            pl.BlockSpec((C_BLOCKS, bd, bd), lambda bi, i: (0, 0, 0)),
            pl.BlockSpec((1, width), const2),
            pl.BlockSpec((1, width), const2),
        ],
        out_specs=pl.BlockSpec((1, ts, width), lambda bi, i: (bi, i, 0)),
        out_shape=jax.ShapeDtypeStruct((b, s, width), BF16),
        scratch_shapes=[pltpu.VMEM((ts + SUBLANES, width), F32),
                        pltpu.VMEM((ts, width), F32),
                        pltpu.VMEM((ts, width), F32),
                        pltpu.VMEM((ts, width), F32),
                        pltpu.VMEM((1, width), F32)],
        compiler_params=_cparams("parallel", "arbitrary"),
        name="rglru",
    )(u, gate, conv_w.astype(F32), row(conv_b), w_rg.astype(BF16), row(b_rg),
      w_ig.astype(BF16), row(b_ig), row(lam))


def _swap_rope_halves(x):
    lane = lax.broadcasted_iota(jnp.int32, (1, LANES), 1)
    first = (lane % D_ROPE) < (D_ROPE // 2)
    return jnp.where(first, pltpu.roll(x, LANES - D_ROPE // 2, 1),
                     pltpu.roll(x, D_ROPE // 2, 1))


def _rope(x, cos, sin_signed):
    return x * cos + _swap_rope_halves(x) * sin_signed


def _mla_attn_kernel(qn_ref, qp_ref, kn_ref, kp_ref, v_ref, g_ref, cos_ref,
                     sin_ref, gains_ref, o_ref, kcat_ref, vt_ref, acc_ref, s_ref,
                     *, tq):
    i = pl.program_id(2)
    seq = kn_ref.shape[1]
    lane = lax.broadcasted_iota(jnp.int32, (1, LANES), 1)
    lo_mask = lane < D_ROPE
    g_q_nope, g_q_pe = gains_ref[0:1, :], gains_ref[1:2, :]
    g_k_nope, g_k_pe = gains_ref[2:3, :], gains_ref[3:4, :]

    @pl.when(i == 0)
    def _():
        def body(c, carry):
            rows = pl.ds(pl.multiple_of(c * tq, tq), tq)
            kp = _rope(_group_rmsnorm(kp_ref[0, rows, :].astype(F32), g_k_pe, D_ROPE),
                       cos_ref[rows, :], sin_ref[rows, :])
            kn = kn_ref[0, rows, :].astype(F32)
            kcat_ref[0, rows, 0:LANES] = _group_rmsnorm(
                kn[:, 0:LANES], g_k_nope, D_NOPE).astype(BF16)
            kcat_ref[0, rows, LANES:] = jnp.where(lo_mask, kp, 0.0).astype(BF16)
            kcat_ref[1, rows, 0:LANES] = _group_rmsnorm(
                kn[:, LANES:], g_k_nope, D_NOPE).astype(BF16)
            kcat_ref[1, rows, LANES:] = jnp.where(lo_mask, 0.0, kp).astype(BF16)
            _store_vt_block(vt_ref, (0, c), v_ref[0, rows, 0:D_V])
            _store_vt_block(vt_ref, (1, c), v_ref[0, rows, D_V:])
            return carry
        lax.fori_loop(0, seq // tq, body, 0, unroll=2)

    scale = (D_NOPE + D_ROPE) ** -0.5 * LOG2E
    qrows = pl.ds(pl.multiple_of(i * tq, tq), tq)
    qp = _rope(_group_rmsnorm(qp_ref[0].astype(F32), g_q_pe, D_ROPE),
               cos_ref[qrows, :], sin_ref[qrows, :]) * scale
    qn = qn_ref[0].astype(F32)
    qcat = (
        jnp.concatenate([(_group_rmsnorm(qn[:, 0:LANES], g_q_nope, D_NOPE)
                          * scale).astype(BF16),
                         jnp.where(lo_mask, qp, 0.0).astype(BF16)], axis=-1),
        jnp.concatenate([(_group_rmsnorm(qn[:, LANES:], g_q_nope, D_NOPE)
                          * scale).astype(BF16),
                         jnp.where(lo_mask, 0.0, qp).astype(BF16)], axis=-1),
    )

    def near_add(j):
        c = lax.broadcasted_iota(jnp.int32, (tq, tq), 0) + tq * (j - jnp.minimum(i, 1))
        r = lax.broadcasted_iota(jnp.int32, (tq, tq), 1)
        visible = (lax.shift_right_arithmetic(c, 6) <= lax.shift_right_arithmetic(r, 6))
        return jnp.where(visible, 0.0, NEG_INF).astype(F32)

    def stream(hh):
        def scores(blk):
            rows = pl.ds(pl.multiple_of(blk * tq, tq), tq)
            return _nt_dot(kcat_ref[hh, rows, :], qcat[hh])
        return scores, lambda blk: vt_ref[hh, blk], near_add

    _chunk_causal_flash(i, [stream(0), stream(1)], acc_ref, s_ref)

    g = g_ref[0].astype(F32)
    o_ref[0, :, 0:D_V] = (_normalised(acc_ref[0]).T * _silu(g[:, 0:D_V])).astype(o_ref.dtype)
    o_ref[0, :, D_V:] = (_normalised(acc_ref[1]).T * _silu(g[:, D_V:])).astype(o_ref.dtype)


def _mla_attention(q, kv, kpe, proj, gate_block0, cos, sin_signed, gains):
    b, s, _ = q.shape
    heads = D_HEADS
    pairs = heads // 2
    tq = ATTN_TQ
    pw = 2 * LANES
    return pl.pallas_call(
        functools.partial(_mla_attn_kernel, tq=tq),
        grid=(b, pairs, s // tq),
        in_specs=[
            pl.BlockSpec((1, tq, pw), lambda bi, p, i: (bi, i, p)),
            pl.BlockSpec((1, tq, LANES), lambda bi, p, i: (bi, i, 2 * pairs + p)),
            pl.BlockSpec((1, s, pw), lambda bi, p, i: (bi, 0, p)),
            pl.BlockSpec((1, s, LANES), lambda bi, p, i: (bi, 0, 0)),
            pl.BlockSpec((1, s, pw), lambda bi, p, i: (bi, 0, pairs + p)),
            pl.BlockSpec((1, tq, pw), lambda bi, p, i: (bi, i, gate_block0 + p)),
            pl.BlockSpec((s, LANES), lambda bi, p, i: (0, 0)),
            pl.BlockSpec((s, LANES), lambda bi, p, i: (0, 0)),
            pl.BlockSpec((4, LANES), lambda bi, p, i: (0, 0)),
        ],
        out_specs=pl.BlockSpec((1, tq, pw), lambda bi, p, i: (bi, i, p)),
        out_shape=jax.ShapeDtypeStruct((b, s, heads * D_V), BF16),
        scratch_shapes=[pltpu.VMEM((2, s, pw), BF16),
                        pltpu.VMEM((2, s // tq, V_ROWS, tq), BF16),
                        pltpu.VMEM((2, V_ROWS, tq), F32),
                        pltpu.VMEM((2, 2, tq, tq), F32)],
        compiler_params=_cparams("parallel", "parallel", "arbitrary"),
        name="mla_attention",
    )(q, q, kv, kpe, kv, proj, cos, sin_signed, gains)


MM_TM = 1024
MM_TN = 1024
RES_TM = 512


def _layer_diff(x2, b, s, norm_g, w_in, qk_g, lam_vecs, subln_g, w_out, bias_tiles,
                layer_idx):
    proj = _norm_matmul(x2, norm_g, w_in, out_dtype=BF16, tm=MM_TM, tn=MM_TN)
    y = _diff_attention(proj.reshape(b, s, -1), lam_vecs, qk_g, subln_g,
                        bias_tiles, layer_idx)
    return _matmul_residual(y.reshape(b * s, -1), w_out.astype(BF16), x2,
                            tm=RES_TM)


def _layer_gla(x2, b, s, norm_g, w_in, w_gate, gate_bias, out_g, w_out):
    rank, hdk = w_gate.shape
    n_main = w_in.shape[1] - rank
    w_lr = jnp.pad(w_in[:, n_main:], ((0, 0), (0, LANES - rank))).astype(BF16)
    w_gate_t = jnp.pad(w_gate, ((0, LANES - rank), (0, 0))).T.astype(BF16)
    proj, lr = _norm_matmul(x2, norm_g, w_in, out_dtype=BF16, tm=MM_TM, tn=MM_TN,
                            n_cols=n_main, w_side=w_lr)
    y = _gla(proj.reshape(b, s, -1), lr.reshape(b, s, LANES), w_gate_t, gate_bias,
             out_g, ts=512)
    return _matmul_residual(y.reshape(b * s, -1), w_out.astype(BF16), x2,
                            tm=RES_TM)


def _layer_rglru(x2, b, s, norm_g, w_in, conv_w, conv_b, w_rg, b_rg, w_ig, b_ig,
                 lam, w_out):
    width = w_in.shape[1] // 2
    u, gate = _norm_matmul(x2, norm_g, w_in.astype(BF16), out_dtype=BF16,
                           tm=MM_TM, tn=MM_TN, first=(width, F32))
    y = _rglru(u.reshape(b, s, width), gate.reshape(b, s, width), 0, conv_w, conv_b,
               w_rg, b_rg, w_ig, b_ig, lam, ts=256)
    return _matmul_residual(y.reshape(b * s, -1), w_out.astype(BF16), x2,
                            tm=RES_TM)


def _rope_tables(s):
    half = D_ROPE // 2
    inv = ROPE_THETA ** (-jnp.arange(half, dtype=F32) / half)
    ang = jnp.arange(s, dtype=F32)[:, None] * inv[None, :]
    cos, sin = jnp.cos(ang), jnp.sin(ang)
    return (jnp.concatenate([cos, cos, cos, cos], axis=-1),
            jnp.concatenate([-sin, sin, -sin, sin], axis=-1))


def _layer_mla(x2, b, s, norm_g, w_in, q_lat_g, kv_lat_g, w_uq, w_ukv, qk_g, w_out):
    q_rank, kv_rank = q_lat_g.shape[0], kv_lat_g.shape[0]
    heads = D_HEADS
    lat = q_rank + kv_rank
    w_main = jnp.concatenate([w_in[:, :lat], w_in[:, lat + D_ROPE:]], axis=1)
    w_kpe = w_in[:, lat:lat + D_ROPE]
    w_kpe = jnp.concatenate([w_kpe, w_kpe], axis=1).astype(BF16)
    uq = w_uq.reshape(q_rank, heads, D_NOPE + D_ROPE)
    w_uq_p = jnp.concatenate([uq[:, :, :D_NOPE].reshape(q_rank, -1),
                              uq[:, :, D_NOPE:].reshape(q_rank, -1)], axis=1).astype(BF16)
    ukv = w_ukv.reshape(kv_rank, heads, D_NOPE + D_V)
    w_ukv_p = jnp.concatenate([ukv[:, :, :D_NOPE].reshape(kv_rank, -1),
                               ukv[:, :, D_NOPE:].reshape(kv_rank, -1)], axis=1).astype(BF16)
    dup = lambda v: jnp.concatenate([v, v])
    gains = jnp.stack([qk_g[0, :D_NOPE], dup(qk_g[0, D_NOPE:]),
                       qk_g[1, :D_NOPE], dup(qk_g[1, D_NOPE:])]).astype(F32)
    cos, sin_signed = _rope_tables(s)

    proj, kpe = _norm_matmul(x2, norm_g, w_main, out_dtype=BF16, tm=MM_TM, tn=MM_TN,
                             w_side=w_kpe)
    q = _norm_matmul(proj, q_lat_g, w_uq_p, out_dtype=BF16, tm=MM_TM,
                     tn=w_uq_p.shape[1], col_block=0)
    kv = _norm_matmul(proj, kv_lat_g, w_ukv_p, out_dtype=BF16, tm=MM_TM,
                      tn=w_ukv_p.shape[1], col_block=1)
    gate_block0 = lat // (2 * LANES)
    y = _mla_attention(q.reshape(b, s, -1), kv.reshape(b, s, -1),
                       kpe.reshape(b, s, LANES), proj.reshape(b, s, -1),
                       gate_block0, cos, sin_signed, gains)
    return _matmul_residual(y.reshape(b * s, -1), w_out.astype(BF16), x2,
                            tm=RES_TM)


def kernel(x, norm_g, rel_bias, a_w_in, a_qk_g, a_lambda, a_subln_g, a_w_out, b_w_in, b_w_gate, b_gate_bias, b_out_g, b_w_out, c_w_in, c_conv_w, c_conv_b, c_w_rgate, c_b_rgate, c_w_igate, c_b_igate, c_lambda, c_w_out, d_w_in, d_q_lat_g, d_kv_lat_g, d_w_uq, d_w_ukv, d_qk_g, d_w_out):
    b, s, d = x.shape
    depth = norm_g.shape[0]
    x2 = x.reshape(b * s, d)
    bias_tiles = _t5_tiles(rel_bias, ATTN_TQ)
    for i in range(depth):
        m, j = i % 4, i // 4
        if m == 0:
            x2 = _layer_diff(x2, b, s, norm_g[i], a_w_in[j], a_qk_g[j], a_lambda[j],
                             a_subln_g[j], a_w_out[j], bias_tiles, i)
        elif m == 1:
            x2 = _layer_gla(x2, b, s, norm_g[i], b_w_in[j], b_w_gate[j],
                            b_gate_bias[j], b_out_g[j], b_w_out[j])
        elif m == 2:
            x2 = _layer_rglru(x2, b, s, norm_g[i], c_w_in[j], c_conv_w[j], c_conv_b[j],
                              c_w_rgate[j], c_b_rgate[j], c_w_igate[j], c_b_igate[j],
                              c_lambda[j], c_w_out[j])
        else:
            x2 = _layer_mla(x2, b, s, norm_g[i], d_w_in[j], d_q_lat_g[j],
                            d_kv_lat_g[j], d_w_uq[j], d_w_ukv[j], d_qk_g[j], d_w_out[j])
    return x2.reshape(b, s, d)
```

```python
import functools
import math

import jax
import jax.numpy as jnp
from jax import lax
from jax.experimental import pallas as pl
from jax.experimental.pallas import tpu as pltpu

F32 = jnp.float32
BF16 = jnp.bfloat16

EPS = 1e-6
NEG_INF = -1e30
LOG2E = math.log2(math.e)
CHUNK = 64
LANES = 128
SUBLANES = 8
BF16_ROWS = 16
V7X_VMEM_BYTES = 64 * 1024 * 1024
VMEM_LIMIT = V7X_VMEM_BYTES - 8 * 1024 * 1024

A_HEADS = 16
A_HEAD_DIM = 64
T5_BUCKETS = 32
B_HEADS = 4
B_GATE_TAU = 16.0
C_BLOCKS = 8
C_CONV = 4
C_C = 8.0
D_HEADS = 16
D_NOPE = 128
D_ROPE = 64
D_V = 128
ROPE_THETA = 10000.0

ATTN_TQ = 512
V_ROWS = LANES + BF16_ROWS
FAR_UNROLL = 4


def _cparams(*semantics):
    return pltpu.CompilerParams(dimension_semantics=semantics,
                                vmem_limit_bytes=VMEM_LIMIT)


def _sigmoid(x):
    return 0.5 * jnp.tanh(0.5 * x) + 0.5


def _silu(g):
    h = 0.5 * g
    return h * jnp.tanh(h) + h


def _softplus(x):
    return jnp.maximum(x, 0.0) + jnp.log(1.0 + jnp.exp(-jnp.abs(x)))


def _one_minus_exp2(y):
    t = jnp.tanh(y)
    return (-2.0 * t) / (1.0 - t)


def _nt_dot(a, b):
    return lax.dot_general(a, b, (((1,), (1,)), ((), ())),
                           preferred_element_type=F32)


def _norm_matmul_kernel(*refs, has_side, n_first):
    refs = list(refs)
    x_ref, g_ref, w_ref = refs[:3]
    del refs[:3]
    ws_ref = refs.pop(0) if has_side else None
    o_ref = refs.pop(0)
    rest_ref = refs.pop(0) if n_first is not None else None
    side_ref = refs.pop(0) if has_side else None
    h_ref = refs.pop(0)
    j = pl.program_id(1)

    @pl.when(j == 0)
    def _():
        x = x_ref[...].astype(F32)
        ms = jnp.mean(x * x, axis=-1, keepdims=True)
        h_ref[...] = (x * lax.rsqrt(ms + EPS) * g_ref[...]).astype(BF16)
        if has_side:
            side_ref[...] = jnp.dot(h_ref[...], ws_ref[...],
                                    preferred_element_type=F32).astype(side_ref.dtype)

    w = w_ref[...]
    if w.dtype != BF16:
        w = w.astype(BF16)
    res = jnp.dot(h_ref[...], w, preferred_element_type=F32)
    if n_first is None:
        o_ref[...] = res.astype(o_ref.dtype)
    else:
        @pl.when(j < n_first)
        def _():
            o_ref[...] = res.astype(o_ref.dtype)

        @pl.when(j >= n_first)
        def _():
            rest_ref[...] = res.astype(rest_ref.dtype)


def _norm_matmul(x, gain, w, *, out_dtype, tm, tn, col_block=0, n_cols=None,
                 w_side=None, first=None):
    m = x.shape[0]
    k = w.shape[0]
    n = w.shape[1] if n_cols is None else n_cols
    n_first = None if first is None else first[0] // tn
    in_specs = [pl.BlockSpec((tm, k), lambda i, j: (i, col_block)),
                pl.BlockSpec((1, k), lambda i, j: (0, 0)),
                pl.BlockSpec((k, tn), lambda i, j: (0, j))]
    args = [x, gain.reshape(1, k).astype(F32), w]
    if w_side is not None:
        in_specs.append(pl.BlockSpec((k, LANES), lambda i, j: (0, 0)))
        args.append(w_side)
    if first is None:
        out_specs = [pl.BlockSpec((tm, tn), lambda i, j: (i, j))]
        out_shape = [jax.ShapeDtypeStruct((m, n), out_dtype)]
    else:
        out_specs = [pl.BlockSpec((tm, tn), lambda i, j: (i, jnp.minimum(j, n_first - 1))),
                     pl.BlockSpec((tm, tn), lambda i, j: (i, jnp.maximum(j - n_first, 0)))]
        out_shape = [jax.ShapeDtypeStruct((m, first[0]), first[1]),
                     jax.ShapeDtypeStruct((m, n - first[0]), out_dtype)]
    if w_side is not None:
        out_specs.append(pl.BlockSpec((tm, LANES), lambda i, j: (i, 0)))
        out_shape.append(jax.ShapeDtypeStruct((m, LANES), BF16))
    outs = pl.pallas_call(
        functools.partial(_norm_matmul_kernel, has_side=w_side is not None,
                          n_first=n_first),
        grid=(m // tm, n // tn),
        in_specs=in_specs,
        out_specs=out_specs,
        out_shape=out_shape,
        scratch_shapes=[pltpu.VMEM((tm, k), BF16)],
        compiler_params=_cparams("parallel", "arbitrary"),
        name="norm_matmul",
    )(*args)
    return outs[0] if len(outs) == 1 else tuple(outs)


def _matmul_residual_kernel(y_ref, w_ref, x_ref, o_ref):
    o_ref[...] = x_ref[...] + jnp.dot(y_ref[...], w_ref[...],
                                      preferred_element_type=F32)


def _matmul_residual(y, w, x, *, tm):
    m, k = y.shape
    n = w.shape[1]
    return pl.pallas_call(
        _matmul_residual_kernel,
        grid=(m // tm,),
        in_specs=[pl.BlockSpec((tm, k), lambda i: (i, 0)),
                  pl.BlockSpec((k, n), lambda i: (0, 0)),
                  pl.BlockSpec((tm, n), lambda i: (i, 0))],
        out_specs=pl.BlockSpec((tm, n), lambda i: (i, 0)),
        out_shape=jax.ShapeDtypeStruct((m, n), F32),
        compiler_params=_cparams("parallel"),
        name="matmul_residual",
    )(y, w, x)


def _group_rmsnorm(x, gain, group):
    r = lax.broadcasted_iota(jnp.int32, (LANES, LANES), 0)
    c = lax.broadcasted_iota(jnp.int32, (LANES, LANES), 1)
    ones = jnp.where(r // group == c // group, 1.0, 0.0).astype(BF16)
    x2 = x * x
    hi = x2.astype(BF16)
    lo = (x2 - hi.astype(F32)).astype(BF16)
    ss = (jnp.dot(hi, ones, preferred_element_type=F32)
          + jnp.dot(lo, ones, preferred_element_type=F32))
    return x * lax.rsqrt(ss * (1.0 / group) + EPS) * gain


def _half_rmsnorm(x, gain, lo_mask):
    x2 = x * x
    lo = jnp.sum(jnp.where(lo_mask, x2, 0.0), axis=-1, keepdims=True)
    hi = jnp.sum(jnp.where(lo_mask, 0.0, x2), axis=-1, keepdims=True)
    ms = jnp.where(lo_mask, lo, hi) * (1.0 / 64)
    return x * lax.rsqrt(ms + EPS) * gain


def _store_vt_block(vt_ref, lead, v_rows):
    tq = v_rows.shape[0]
    r = lax.broadcasted_iota(jnp.int32, (LANES, LANES), 0)
    c = lax.broadcasted_iota(jnp.int32, (LANES, LANES), 1)
    eye = jnp.where(r == c, 1.0, 0.0).astype(BF16)
    vt_ref[(*lead, slice(0, LANES), slice(None))] = _nt_dot(eye, v_rows).astype(BF16)
    vt_ref[(*lead, slice(LANES, V_ROWS), slice(None))] = jnp.ones((BF16_ROWS, tq), BF16)


def _flash_init(s, vt):
    m = jnp.max(s, axis=0, keepdims=True)
    p = jnp.exp2(s - m).astype(BF16)
    return m, jnp.dot(vt, p, preferred_element_type=F32)


def _flash_update(s, vt, m, acc):
    m_new = jnp.maximum(m, jnp.max(s, axis=0, keepdims=True))
    alpha = jnp.exp2(m - m_new)
    p = jnp.exp2(s - m_new).astype(BF16)
    return m_new, alpha * acc + jnp.dot(vt, p, preferred_element_type=F32)


def _chunk_causal_flash(i, streams, acc_ref, s_ref):
    first = jnp.maximum(i - 1, 0)
    n_far = first

    def issue(blk, slot, add=None):
        for si, (scores, _, near_add) in enumerate(streams):
            s = scores(blk)
            s_ref[si, slot] = s if add is None else s + near_add(add)

    def update(blk, slot, ms):
        out = []
        for si, (_, values, _) in enumerate(streams):
            m, acc_ref[si] = _flash_update(s_ref[si, slot], values(blk), ms[si],
                                           acc_ref[si])
            out.append(m)
        return tuple(out)

    def group(t0, count, prefetch_next, ms):
        for k in range(count):
            if k + 1 < count or prefetch_next:
                issue(t0 + k + 1, (k + 1) % 2)
            ms = update(t0 + k, k % 2, ms)
        return ms

    issue(first, 0, add=0)
    issue(first + 1, 1, add=1)
    ms = []
    for si, (_, values, _) in enumerate(streams):
        m, acc_ref[si] = _flash_init(s_ref[si, 0], values(first))
        ms.append(m)
    issue(0, 0)
    ms = update(first + 1, 1, tuple(ms))

    full = jnp.maximum(n_far - 1, 0) // FAR_UNROLL
    rest = n_far - FAR_UNROLL * full
    ms = lax.fori_loop(0, full,
                       lambda g, ms: group(g * FAR_UNROLL, FAR_UNROLL, True, ms), ms)
    for count in range(1, FAR_UNROLL + 1):
        ms = lax.fori_loop(0, (rest == count).astype(jnp.int32),
                           lambda _, ms, count=count: group(n_far - count, count, False, ms),
                           ms)


def _normalised(acc):
    return acc[0:LANES] * (1.0 / acc[LANES:LANES + 1])


_T5_LARGE_THRESHOLDS = (12, 16, 23, 32, 46, 64, 91)


def _t5_tiles_kernel(rb_ref, o_ref, *, tq):
    h = pl.program_id(0)
    variant = pl.program_id(1)
    j = lax.broadcasted_iota(jnp.int32, (1, tq), 1)
    rel_row = jnp.where(j < tq // 2, -j, tq - j)
    n = jnp.abs(rel_row)
    large = jnp.full_like(n, 8)
    for t in _T5_LARGE_THRESHOLDS:
        large = large + (n >= t).astype(jnp.int32)
    bucket = jnp.where(rel_row > 0, T5_BUCKETS // 2, 0) + jnp.where(n < 8, n, large)
    far = rb_ref[T5_BUCKETS // 2 - 1, h]
    row = jnp.zeros((1, tq), F32)
    for b in range(T5_BUCKETS):
        row = jnp.where(bucket == b, (rb_ref[b, h] - far) * LOG2E, row)
    tile = pltpu.roll(jnp.broadcast_to(row, (2 * tq, tq)), 0, 1, stride=1, stride_axis=0)
    c = lax.broadcasted_iota(jnp.int32, (2 * tq, tq), 0)
    r = lax.broadcasted_iota(jnp.int32, (2 * tq, tq), 1)
    key_off = c - tq * variant
    bias = jnp.where(key_off - r > -_T5_LARGE_THRESHOLDS[-1], tile, 0.0)
    visible = (lax.shift_right_arithmetic(key_off, 6)
               <= lax.shift_right_arithmetic(r, 6))
    o_ref[0, 0] = jnp.where(visible, bias, NEG_INF)


def _t5_tiles(rel_bias, tq):
    nb, nh = rel_bias.shape
    return pl.pallas_call(
        functools.partial(_t5_tiles_kernel, tq=tq),
        grid=(nh, 2),
        in_specs=[pl.BlockSpec(memory_space=pltpu.SMEM)],
        out_specs=pl.BlockSpec((1, 1, 2 * tq, tq), lambda h, v: (h, v, 0, 0)),
        out_shape=jax.ShapeDtypeStruct((nh, 2, 2 * tq, tq), F32),
        compiler_params=_cparams("parallel", "arbitrary"),
        name="t5_tiles",
    )(rel_bias.astype(F32))


def _diff_attn_kernel(lamv_ref, q_ref, k_ref, v_ref, g_ref, bias_ref, qkg_ref,
                      sub_ref, o_ref, kn_ref, vt_ref, acc_ref, s_ref, *, tq, lam_init):
    i = pl.program_id(2)
    seq = k_ref.shape[1]
    lane = lax.broadcasted_iota(jnp.int32, (1, LANES), 1)
    lo_mask = lane < A_HEAD_DIM

    @pl.when(i == 0)
    def _():
        def body(c, carry):
            rows = pl.ds(pl.multiple_of(c * tq, tq), tq)
            kn_ref[rows, :] = _group_rmsnorm(
                k_ref[0, rows, :].astype(F32), qkg_ref[1:2, :], A_HEAD_DIM).astype(BF16)
            _store_vt_block(vt_ref, (c,), v_ref[0, rows, :])
            return carry
        lax.fori_loop(0, seq // tq, body, 0, unroll=2)

    lf = lamv_ref[...].astype(F32)
    lam = (jnp.exp(jnp.sum(lf[0:1] * lf[1:2], axis=-1, keepdims=True))
           - jnp.exp(jnp.sum(lf[2:3] * lf[3:4], axis=-1, keepdims=True)) + lam_init)

    qn = (_half_rmsnorm(q_ref[0].astype(F32), qkg_ref[0:1, :], lo_mask)
          * (A_HEAD_DIM ** -0.5 * LOG2E))
    q01 = jnp.concatenate([jnp.where(lo_mask, qn, 0.0).astype(BF16),
                           jnp.where(lo_mask, 0.0, qn).astype(BF16)], axis=0)

    def scores(blk):
        return _nt_dot(kn_ref[pl.ds(pl.multiple_of(blk * tq, tq), tq), :], q01)

    def near_add(j):
        bias_t = bias_ref[0, 0, j * tq:(j + 1) * tq, :]
        return jnp.concatenate([bias_t, bias_t], axis=1)

    _chunk_causal_flash(i, [(scores, lambda blk: vt_ref[blk], near_add)], acc_ref,
                        s_ref)

    o01 = _normalised(acc_ref[0])
    o_t = o01[:, 0:tq] - lam * o01[:, tq:]
    ms_t = jnp.mean(o_t * o_t, axis=0, keepdims=True)
    o_t = o_t * lax.rsqrt(ms_t + EPS) * sub_ref[...] * (1.0 - lam_init)
    o_ref[0] = (o_t.T * _silu(g_ref[0].astype(F32))).astype(o_ref.dtype)


def _diff_attention(proj, lam_vecs, qk_g, subln_g, bias_tiles, layer_idx):
    b, s, four_w = proj.shape
    width = four_w // 4
    hblocks = width // LANES
    tq = ATTN_TQ
    lam_init = 0.8 - 0.6 * math.exp(-0.3 * layer_idx)
    qkg = jnp.concatenate([qk_g, qk_g], axis=-1).astype(F32)
    return pl.pallas_call(
        functools.partial(_diff_attn_kernel, tq=tq, lam_init=lam_init),
        grid=(b, hblocks, s // tq),
        in_specs=[
            pl.BlockSpec((4, A_HEAD_DIM), lambda bi, h, i: (0, 0)),
            pl.BlockSpec((1, tq, LANES), lambda bi, h, i: (bi, i, h)),
            pl.BlockSpec((1, s, LANES), lambda bi, h, i: (bi, 0, hblocks + h)),
            pl.BlockSpec((1, s, LANES), lambda bi, h, i: (bi, 0, 2 * hblocks + h)),
            pl.BlockSpec((1, tq, LANES), lambda bi, h, i: (bi, i, 3 * hblocks + h)),
            pl.BlockSpec((1, 1, 2 * tq, tq),
                         lambda bi, h, i: (h, jnp.minimum(i, 1), 0, 0)),
            pl.BlockSpec((2, LANES), lambda bi, h, i: (0, 0)),
            pl.BlockSpec((LANES, 1), lambda bi, h, i: (0, 0)),
        ],
        out_specs=pl.BlockSpec((1, tq, LANES), lambda bi, h, i: (bi, i, h)),
        out_shape=jax.ShapeDtypeStruct((b, s, width), BF16),
        scratch_shapes=[pltpu.VMEM((s, LANES), BF16),
                        pltpu.VMEM((s // tq, V_ROWS, tq), BF16),
                        pltpu.VMEM((1, V_ROWS, 2 * tq), F32),
                        pltpu.VMEM((1, 2, tq, 2 * tq), F32)],
        compiler_params=_cparams("parallel", "parallel", "arbitrary"),
        name="diff_attention",
    )(lam_vecs.astype(F32), proj, proj, proj, proj, bias_tiles, qkg,
      subln_g.reshape(LANES, 1).astype(F32))


def _gla_kernel(q_ref, k_ref, v_ref, g_ref, lr_ref, wgt_ref, gb_ref, og_ref,
                cm_ref, o_ref, state_ref, *, ts, dk):
    @pl.when(pl.program_id(2) == 0)
    def _():
        state_ref[...] = jnp.zeros_like(state_ref)

    heads = q_ref.shape[2] // dk
    dv = v_ref.shape[2] // heads
    span = cm_ref.shape[0]
    la_t, rest_t, kd_t = [], [], []
    for hh in range(heads):
        hk = slice(hh * dk, (hh + 1) * dk)
        z = _nt_dot(wgt_ref[hk, :], lr_ref[0]) + gb_ref[hk, :]
        la = -_softplus(-z) * (1.0 / B_GATE_TAU)
        la_hi = la.astype(BF16)
        la_lo = (la - la_hi.astype(F32)).astype(BF16)
        rest = jnp.concatenate(
            [jnp.dot(la_hi[:, s0:s0 + span], cm_ref[...], preferred_element_type=F32)
             + jnp.dot(la_lo[:, s0:s0 + span], cm_ref[...], preferred_element_type=F32)
             for s0 in range(0, ts, span)], axis=1)
        la_t.append(la)
        rest_t.append(rest)
        kd_t.append((k_ref[0, :, hk].astype(F32).T * jnp.exp(rest)).astype(BF16))

    lane = lax.broadcasted_iota(jnp.int32, (1, LANES), 1)
    first_chunk = lane < CHUNK
    states = [state_ref[hh] for hh in range(heads)]
    for c in range(ts // CHUNK):
        group = slice((c // 2) * LANES, (c // 2 + 1) * LANES)
        keep = first_chunk if c % 2 == 0 else jnp.logical_not(first_chunk)
        t0 = slice(c * CHUNK, c * CHUNK + 1)
        rows = slice(c * CHUNK, (c + 1) * CHUNK)
        for hh in range(heads):
            hk = slice(hh * dk, (hh + 1) * dk)
            hv = slice(hh * dv, (hh + 1) * dv)
            kd_c = jnp.where(keep, kd_t[hh][:, group], jnp.zeros((), BF16))
            states[hh] = (jnp.exp(rest_t[hh][:, t0] + la_t[hh][:, t0]) * states[hh]
                          + jnp.dot(kd_c, v_ref[0, group, hv], preferred_element_type=F32))
            q = (q_ref[0, rows, hk].astype(F32) * (dk ** -0.5)).astype(BF16)
            o = jnp.dot(q, states[hh].astype(BF16), preferred_element_type=F32)
            ms = jnp.mean(o * o, axis=-1, keepdims=True)
            o = o * lax.rsqrt(ms + EPS) * og_ref[...]
            o_ref[0, rows, hv] = (o * _silu(g_ref[0, rows, hv].astype(F32))).astype(o_ref.dtype)
    for hh in range(heads):
        state_ref[hh] = states[hh]


GLA_HEADS_PER_STEP = 2


def _gla(proj, lr, w_gate_t, gate_bias, out_g, *, ts):
    b, s, n = proj.shape
    heads = B_HEADS
    dk = n // (6 * heads)
    dv = 2 * dk
    hp = GLA_HEADS_PER_STEP
    steps = heads // hp
    wk, wv = hp * dk, hp * dv
    k0 = (heads * dk) // wk
    v0 = (2 * heads * dk) // wv
    g0 = v0 + steps
    span = 2 * LANES
    t_in = jnp.arange(span)[:, None]
    t_out = jnp.arange(span)[None, :]
    chunk_masks = (((t_in // CHUNK) == (t_out // CHUNK)) & (t_in > t_out)).astype(BF16)
    return pl.pallas_call(
        functools.partial(_gla_kernel, ts=ts, dk=dk),
        grid=(b, steps, s // ts),
        in_specs=[
            pl.BlockSpec((1, ts, wk), lambda bi, h, i: (bi, i, h)),
            pl.BlockSpec((1, ts, wk), lambda bi, h, i: (bi, i, k0 + h)),
            pl.BlockSpec((1, ts, wv), lambda bi, h, i: (bi, i, v0 + h)),
            pl.BlockSpec((1, ts, wv), lambda bi, h, i: (bi, i, g0 + h)),
            pl.BlockSpec((1, ts, LANES), lambda bi, h, i: (bi, i, 0)),
            pl.BlockSpec((wk, LANES), lambda bi, h, i: (h, 0)),
            pl.BlockSpec((wk, 1), lambda bi, h, i: (h, 0)),
            pl.BlockSpec((1, dv), lambda bi, h, i: (0, 0)),
            pl.BlockSpec((span, span), lambda bi, h, i: (0, 0)),
        ],
        out_specs=pl.BlockSpec((1, ts, wv), lambda bi, h, i: (bi, i, h)),
        out_shape=jax.ShapeDtypeStruct((b, s, heads * dv), BF16),
        scratch_shapes=[pltpu.VMEM((hp, dk, dv), F32)],
        compiler_params=_cparams("parallel", "parallel", "arbitrary"),
        name="gla",
    )(proj, proj, proj, proj, lr, w_gate_t, gate_bias.reshape(-1, 1).astype(F32),
      out_g.reshape(1, dv).astype(F32), chunk_masks)


def _rglru_kernel(u_ref, g_ref, cw_ref, cb_ref, wr_ref, br_ref, wi_ref, bi_ref,
                  lam_ref, o_ref, ubuf_ref, xc_ref, a_ref, b_ref, h_ref, *, ts):
    width = u_ref.shape[2]
    bd = width // C_BLOCKS

    @pl.when(pl.program_id(1) == 0)
    def _():
        ubuf_ref[0:SUBLANES, :] = jnp.zeros((SUBLANES, width), F32)
        h_ref[...] = jnp.zeros_like(h_ref)

    ubuf_ref[SUBLANES:SUBLANES + ts, :] = u_ref[0]
    ue = ubuf_ref[...]
    xc = ue * cw_ref[0:1, :]
    for t in range(1, C_CONV):
        xc = pltpu.roll(xc, 1, 0) + ue * cw_ref[t:t + 1, :]
    xc_ref[...] = xc[SUBLANES:, :] + cb_ref[...]
    ubuf_ref[0:SUBLANES, :] = ubuf_ref[ts:ts + SUBLANES, :]

    rate = -C_C * _softplus(-lam_ref[...])
    for n in range(C_BLOCKS):
        cols = slice(n * bd, (n + 1) * bd)
        xb = xc_ref[:, cols]
        xb16 = xb.astype(BF16)
        r = _sigmoid(jnp.dot(xb16, wr_ref[n], preferred_element_type=F32)
                     + br_ref[:, cols])
        gate_i = _sigmoid(jnp.dot(xb16, wi_ref[n], preferred_element_type=F32)
                          + bi_ref[:, cols])
        log_a = r * rate[:, cols]
        a_ref[:, cols] = jnp.exp(log_a)
        b_ref[:, cols] = jnp.sqrt(_one_minus_exp2(log_a)) * (gate_i * xb)

    row = lax.broadcasted_iota(jnp.int32, (SUBLANES, width), 0)

    def tile(t, h_prev):
        rows = pl.ds(pl.multiple_of(t * SUBLANES, SUBLANES), SUBLANES)
        a = a_ref[rows, :]
        b = b_ref[rows, :]
        d = 1
        while d < SUBLANES:
            a_up = pltpu.roll(a, d, 0)
            b_up = pltpu.roll(b, d, 0)
            keep = row >= d
            b = jnp.where(keep, a * b_up + b, b)
            a = jnp.where(keep, a * a_up, a)
            d *= 2
        h = b + a * h_prev
        b_ref[rows, :] = h
        return h[SUBLANES - 1:SUBLANES, :]

    h_ref[...] = lax.fori_loop(0, ts // SUBLANES, tile, h_ref[...])
    o_ref[0] = (b_ref[...] * _silu(g_ref[0].astype(F32))).astype(o_ref.dtype)


def _rglru(u, gate, gate_block, conv_w, conv_b, w_rg, b_rg, w_ig, b_ig, lam, *, ts):
    b, s, width = u.shape
    bd = width // C_BLOCKS
    row = lambda a: a.reshape(1, width).astype(F32)
    const2 = lambda bi, i: (0, 0)
    return pl.pallas_call(
        functools.partial(_rglru_kernel, ts=ts),
        grid=(b, s // ts),
        in_specs=[
            pl.BlockSpec((1, ts, width), lambda bi, i: (bi, i, 0)),
            pl.BlockSpec((1, ts, width), lambda bi, i: (bi, i, gate_block)),
            pl.BlockSpec((C_CONV, width), const2),
            pl.BlockSpec((1, width), const2),
            pl.BlockSpec((C_BLOCKS, bd, bd), lambda bi, i: (0, 0, 0)),
            pl.BlockSpec((1, width), const2),
            pl.BlockSpec((C_BLOCKS, bd, bd), lambda bi, i: (0, 0, 0)),
            pl.BlockSpec((1, width), const2),
            pl.BlockSpec((1, width), const2),
        ],
        out_specs=pl.BlockSpec((1, ts, width), lambda bi, i: (bi, i, 0)),
        out_shape=jax.ShapeDtypeStruct((b, s, width), BF16),
        scratch_shapes=[pltpu.VMEM((ts + SUBLANES, width), F32),
                        pltpu.VMEM((ts, width), F32),
                        pltpu.VMEM((ts, width), F32),
                        pltpu.VMEM((ts, width), F32),
                        pltpu.VMEM((1, width), F32)],
        compiler_params=_cparams("parallel", "arbitrary"),
        name="rglru",
    )(u, gate, conv_w.astype(F32), row(conv_b), w_rg.astype(BF16), row(b_rg),
      w_ig.astype(BF16), row(b_ig), row(lam))


def _swap_rope_halves(x):
    lane = lax.broadcasted_iota(jnp.int32, (1, LANES), 1)
    first = (lane % D_ROPE) < (D_ROPE // 2)
    return jnp.where(first, pltpu.roll(x, LANES - D_ROPE // 2, 1),
                     pltpu.roll(x, D_ROPE // 2, 1))


def _rope(x, cos, sin_signed):
    return x * cos + _swap_rope_halves(x) * sin_signed


def _mla_attn_kernel(qn_ref, qp_ref, kn_ref, kp_ref, v_ref, g_ref, cos_ref,
                     sin_ref, gains_ref, o_ref, kcat_ref, vt_ref, acc_ref, s_ref,
                     *, tq):
    i = pl.program_id(2)
    seq = kn_ref.shape[1]
    lane = lax.broadcasted_iota(jnp.int32, (1, LANES), 1)
    lo_mask = lane < D_ROPE
    g_q_nope, g_q_pe = gains_ref[0:1, :], gains_ref[1:2, :]
    g_k_nope, g_k_pe = gains_ref[2:3, :], gains_ref[3:4, :]

    @pl.when(i == 0)
    def _():
        def body(c, carry):
            rows = pl.ds(pl.multiple_of(c * tq, tq), tq)
            kp = _rope(_group_rmsnorm(kp_ref[0, rows, :].astype(F32), g_k_pe, D_ROPE),
                       cos_ref[rows, :], sin_ref[rows, :])
            kn = kn_ref[0, rows, :].astype(F32)
            kcat_ref[0, rows, 0:LANES] = _group_rmsnorm(
                kn[:, 0:LANES], g_k_nope, D_NOPE).astype(BF16)
            kcat_ref[0, rows, LANES:] = jnp.where(lo_mask, kp, 0.0).astype(BF16)
            kcat_ref[1, rows, 0:LANES] = _group_rmsnorm(
                kn[:, LANES:], g_k_nope, D_NOPE).astype(BF16)
            kcat_ref[1, rows, LANES:] = jnp.where(lo_mask, 0.0, kp).astype(BF16)
            _store_vt_block(vt_ref, (0, c), v_ref[0, rows, 0:D_V])
            _store_vt_block(vt_ref, (1, c), v_ref[0, rows, D_V:])
            return carry
        lax.fori_loop(0, seq // tq, body, 0, unroll=2)

    scale = (D_NOPE + D_ROPE) ** -0.5 * LOG2E
    qrows = pl.ds(pl.multiple_of(i * tq, tq), tq)
    qp = _rope(_group_rmsnorm(qp_ref[0].astype(F32), g_q_pe, D_ROPE),
               cos_ref[qrows, :], sin_ref[qrows, :]) * scale
    qn = qn_ref[0].astype(F32)
    qcat = (
        jnp.concatenate([(_group_rmsnorm(qn[:, 0:LANES], g_q_nope, D_NOPE)
                          * scale).astype(BF16),
                         jnp.where(lo_mask, qp, 0.0).astype(BF16)], axis=-1),
        jnp.concatenate([(_group_rmsnorm(qn[:, LANES:], g_q_nope, D_NOPE)
                          * scale).astype(BF16),
                         jnp.where(lo_mask, 0.0, qp).astype(BF16)], axis=-1),
    )

    def near_add(j):
        c = lax.broadcasted_iota(jnp.int32, (tq, tq), 0) + tq * (j - jnp.minimum(i, 1))
        r = lax.broadcasted_iota(jnp.int32, (tq, tq), 1)
        visible = (lax.shift_right_arithmetic(c, 6) <= lax.shift_right_arithmetic(r, 6))
        return jnp.where(visible, 0.0, NEG_INF).astype(F32)

    def stream(hh):
        def scores(blk):
            rows = pl.ds(pl.multiple_of(blk * tq, tq), tq)
            return _nt_dot(kcat_ref[hh, rows, :], qcat[hh])
        return scores, lambda blk: vt_ref[hh, blk], near_add

    _chunk_causal_flash(i, [stream(0), stream(1)], acc_ref, s_ref)

    g = g_ref[0].astype(F32)
    o_ref[0, :, 0:D_V] = (_normalised(acc_ref[0]).T * _silu(g[:, 0:D_V])).astype(o_ref.dtype)
    o_ref[0, :, D_V:] = (_normalised(acc_ref[1]).T * _silu(g[:, D_V:])).astype(o_ref.dtype)


def _mla_attention(q, kv, kpe, proj, gate_block0, cos, sin_signed, gains):
    b, s, _ = q.shape
    heads = D_HEADS
    pairs = heads // 2
    tq = ATTN_TQ
    pw = 2 * LANES
    return pl.pallas_call(
        functools.partial(_mla_attn_kernel, tq=tq),
        grid=(b, pairs, s // tq),
        in_specs=[
            pl.BlockSpec((1, tq, pw), lambda bi, p, i: (bi, i, p)),
            pl.BlockSpec((1, tq, LANES), lambda bi, p, i: (bi, i, 2 * pairs + p)),
            pl.BlockSpec((1, s, pw), lambda bi, p, i: (bi, 0, p)),
            pl.BlockSpec((1, s, LANES), lambda bi, p, i: (bi, 0, 0)),
            pl.BlockSpec((1, s, pw), lambda bi, p, i: (bi, 0, pairs + p)),
            pl.BlockSpec((1, tq, pw), lambda bi, p, i: (bi, i, gate_block0 + p)),
            pl.BlockSpec((s, LANES), lambda bi, p, i: (0, 0)),
            pl.BlockSpec((s, LANES), lambda bi, p, i: (0, 0)),
            pl.BlockSpec((4, LANES), lambda bi, p, i: (0, 0)),
        ],
        out_specs=pl.BlockSpec((1, tq, pw), lambda bi, p, i: (bi, i, p)),
        out_shape=jax.ShapeDtypeStruct((b, s, heads * D_V), BF16),
        scratch_shapes=[pltpu.VMEM((2, s, pw), BF16),
                        pltpu.VMEM((2, s // tq, V_ROWS, tq), BF16),
                        pltpu.VMEM((2, V_ROWS, tq), F32),
                        pltpu.VMEM((2, 2, tq, tq), F32)],
        compiler_params=_cparams("parallel", "parallel", "arbitrary"),
        name="mla_attention",
    )(q, q, kv, kpe, kv, proj, cos, sin_signed, gains)


MM_TM = 1024
MM_TN = 1024
RES_TM = 512


def _layer_diff(x2, b, s, norm_g, w_in, qk_g, lam_vecs, subln_g, w_out, bias_tiles,
                layer_idx):
    proj = _norm_matmul(x2, norm_g, w_in, out_dtype=BF16, tm=MM_TM, tn=MM_TN)
    y = _diff_attention(proj.reshape(b, s, -1), lam_vecs, qk_g, subln_g,
                        bias_tiles, layer_idx)
    return _matmul_residual(y.reshape(b * s, -1), w_out.astype(BF16), x2,
                            tm=RES_TM)


def _layer_gla(x2, b, s, norm_g, w_in, w_gate, gate_bias, out_g, w_out):
    rank, hdk = w_gate.shape
    n_main = w_in.shape[1] - rank
    w_lr = jnp.pad(w_in[:, n_main:], ((0, 0), (0, LANES - rank))).astype(BF16)
    w_gate_t = jnp.pad(w_gate, ((0, LANES - rank), (0, 0))).T.astype(BF16)
    proj, lr = _norm_matmul(x2, norm_g, w_in[:, :n_main], out_dtype=BF16, tm=MM_TM,
                            tn=MM_TN, w_side=w_lr)
    y = _gla(proj.reshape(b, s, -1), lr.reshape(b, s, LANES), w_gate_t, gate_bias,
             out_g, ts=512)
    return _matmul_residual(y.reshape(b * s, -1), w_out.astype(BF16), x2,
                            tm=RES_TM)


def _layer_rglru(x2, b, s, norm_g, w_in, conv_w, conv_b, w_rg, b_rg, w_ig, b_ig,
                 lam, w_out):
    width = w_in.shape[1] // 2
    u, gate = _norm_matmul(x2, norm_g, w_in.astype(BF16), out_dtype=BF16,
                           tm=MM_TM, tn=MM_TN, first=(width, F32))
    y = _rglru(u.reshape(b, s, width), gate.reshape(b, s, width), 0, conv_w, conv_b,
               w_rg, b_rg, w_ig, b_ig, lam, ts=256)
    return _matmul_residual(y.reshape(b * s, -1), w_out.astype(BF16), x2,
                            tm=RES_TM)


def _rope_tables(s):
    half = D_ROPE // 2
    inv = ROPE_THETA ** (-jnp.arange(half, dtype=F32) / half)
    ang = jnp.arange(s, dtype=F32)[:, None] * inv[None, :]
    cos, sin = jnp.cos(ang), jnp.sin(ang)
    return (jnp.concatenate([cos, cos, cos, cos], axis=-1),
            jnp.concatenate([-sin, sin, -sin, sin], axis=-1))


def _layer_mla(x2, b, s, norm_g, w_in, q_lat_g, kv_lat_g, w_uq, w_ukv, qk_g, w_out):
    q_rank, kv_rank = q_lat_g.shape[0], kv_lat_g.shape[0]
    heads = D_HEADS
    lat = q_rank + kv_rank
    w_main = jnp.concatenate([w_in[:, :lat], w_in[:, lat + D_ROPE:]], axis=1)
    w_kpe = w_in[:, lat:lat + D_ROPE]
    w_kpe = jnp.concatenate([w_kpe, w_kpe], axis=1).astype(BF16)
    uq = w_uq.reshape(q_rank, heads, D_NOPE + D_ROPE)
    w_uq_p = jnp.concatenate([uq[:, :, :D_NOPE].reshape(q_rank, -1),
                              uq[:, :, D_NOPE:].reshape(q_rank, -1)], axis=1).astype(BF16)
    ukv = w_ukv.reshape(kv_rank, heads, D_NOPE + D_V)
    w_ukv_p = jnp.concatenate([ukv[:, :, :D_NOPE].reshape(kv_rank, -1),
                               ukv[:, :, D_NOPE:].reshape(kv_rank, -1)], axis=1).astype(BF16)
    dup = lambda v: jnp.concatenate([v, v])
    gains = jnp.stack([qk_g[0, :D_NOPE], dup(qk_g[0, D_NOPE:]),
                       qk_g[1, :D_NOPE], dup(qk_g[1, D_NOPE:])]).astype(F32)
    cos, sin_signed = _rope_tables(s)

    proj, kpe = _norm_matmul(x2, norm_g, w_main, out_dtype=BF16, tm=MM_TM, tn=MM_TN,
                             w_side=w_kpe)
    q = _norm_matmul(proj, q_lat_g, w_uq_p, out_dtype=BF16, tm=MM_TM,
                     tn=w_uq_p.shape[1], col_block=0)
    kv = _norm_matmul(proj, kv_lat_g, w_ukv_p, out_dtype=BF16, tm=MM_TM,
                      tn=w_ukv_p.shape[1], col_block=1)
    gate_block0 = lat // (2 * LANES)
    y = _mla_attention(q.reshape(b, s, -1), kv.reshape(b, s, -1),
                       kpe.reshape(b, s, LANES), proj.reshape(b, s, -1),
                       gate_block0, cos, sin_signed, gains)
    return _matmul_residual(y.reshape(b * s, -1), w_out.astype(BF16), x2,
                            tm=RES_TM)


def kernel(x, norm_g, rel_bias, a_w_in, a_qk_g, a_lambda, a_subln_g, a_w_out, b_w_in, b_w_gate, b_gate_bias, b_out_g, b_w_out, c_w_in, c_conv_w, c_conv_b, c_w_rgate, c_b_rgate, c_w_igate, c_b_igate, c_lambda, c_w_out, d_w_in, d_q_lat_g, d_kv_lat_g, d_w_uq, d_w_ukv, d_qk_g, d_w_out):
    b, s, d = x.shape
    depth = norm_g.shape[0]
    x2 = x.reshape(b * s, d)
    bias_tiles = _t5_tiles(rel_bias, ATTN_TQ)
    for i in range(depth):
        m, j = i % 4, i // 4
        if m == 0:
            x2 = _layer_diff(x2, b, s, norm_g[i], a_w_in[j], a_qk_g[j], a_lambda[j],
                             a_subln_g[j], a_w_out[j], bias_tiles, i)
        elif m == 1:
            x2 = _layer_gla(x2, b, s, norm_g[i], b_w_in[j], b_w_gate[j],
                            b_gate_bias[j], b_out_g[j], b_w_out[j])
        elif m == 2:
            x2 = _layer_rglru(x2, b, s, norm_g[i], c_w_in[j], c_conv_w[j], c_conv_b[j],
                              c_w_rgate[j], c_b_rgate[j], c_w_igate[j], c_b_igate[j],
                              c_lambda[j], c_w_out[j])
        else:
            x2 = _layer_mla(x2, b, s, norm_g[i], d_w_in[j], d_q_lat_g[j],
                            d_kv_lat_g[j], d_w_uq[j], d_w_ukv[j], d_qk_g[j], d_w_out[j])
    return x2.reshape(b, s, d)
```

```python
import functools
import math

import jax
import jax.numpy as jnp
from jax import lax
from jax.experimental import pallas as pl
from jax.experimental.pallas import tpu as pltpu

F32 = jnp.float32
BF16 = jnp.bfloat16

EPS = 1e-6
NEG_INF = -1e30
LOG2E = math.log2(math.e)
CHUNK = 64
LANES = 128
SUBLANES = 8
BF16_ROWS = 16
V7X_VMEM_BYTES = 64 * 1024 * 1024
COMPILER_SCRATCH_BYTES = 8 * 1024 * 1024
VMEM_LIMIT = V7X_VMEM_BYTES - COMPILER_SCRATCH_BYTES

A_HEADS = 16
A_HEAD_DIM = 64
T5_BUCKETS = 32
B_HEADS = 4
B_GATE_TAU = 16.0
C_BLOCKS = 8
C_CONV = 4
C_C = 8.0
D_HEADS = 16
D_NOPE = 128
D_ROPE = 64
D_V = 128
ROPE_THETA = 10000.0

ATTN_TQ = 512
V_ROWS = LANES + BF16_ROWS
FAR_UNROLL = 4


def _cparams(*semantics):
    return pltpu.CompilerParams(dimension_semantics=semantics,
                                vmem_limit_bytes=VMEM_LIMIT)


def _sigmoid(x):
    return 0.5 * jnp.tanh(0.5 * x) + 0.5


def _silu(g):
    h = 0.5 * g
    return h * jnp.tanh(h) + h


def _softplus(x):
    return jnp.maximum(x, 0.0) + jnp.log(1.0 + jnp.exp(-jnp.abs(x)))


def _one_minus_exp2(y):
    t = jnp.tanh(y)
    return (-2.0 * t) / (1.0 - t)


def _nt_dot(a, b):
    return lax.dot_general(a, b, (((1,), (1,)), ((), ())),
                           preferred_element_type=F32)


def _norm_matmul_kernel(*refs, has_side, n_first):
    refs = list(refs)
    x_ref, g_ref, w_ref = refs[:3]
    del refs[:3]
    ws_ref = refs.pop(0) if has_side else None
    o_ref = refs.pop(0)
    rest_ref = refs.pop(0) if n_first is not None else None
    side_ref = refs.pop(0) if has_side else None
    h_ref = refs.pop(0)
    j = pl.program_id(1)

    @pl.when(j == 0)
    def _():
        x = x_ref[...].astype(F32)
        ms = jnp.mean(x * x, axis=-1, keepdims=True)
        h_ref[...] = (x * lax.rsqrt(ms + EPS) * g_ref[...]).astype(BF16)
        if has_side:
            side_ref[...] = jnp.dot(h_ref[...], ws_ref[...],
                                    preferred_element_type=F32).astype(side_ref.dtype)

    w = w_ref[0] if len(w_ref.shape) == 3 else w_ref[...]
    if w.dtype != BF16:
        w = w.astype(BF16)
    res = jnp.dot(h_ref[...], w, preferred_element_type=F32)
    if n_first is None:
        o_ref[...] = res.astype(o_ref.dtype)
    else:
        @pl.when(j < n_first)
        def _():
            o_ref[...] = res.astype(o_ref.dtype)

        @pl.when(j >= n_first)
        def _():
            rest_ref[...] = res.astype(rest_ref.dtype)


def _norm_matmul(x, gain, w, *, out_dtype, tm, tn, col_block=0, n_cols=None,
                 w_side=None, first=None, layer=0):
    m = x.shape[0]
    k = w.shape[-2]
    n = w.shape[-1] if n_cols is None else n_cols
    n_first = None if first is None else first[0] // tn
    if w.ndim == 3:
        w_spec = pl.BlockSpec((1, k, tn), lambda i, j: (layer, 0, j))
    else:
        w_spec = pl.BlockSpec((k, tn), lambda i, j: (0, j))
    in_specs = [pl.BlockSpec((tm, k), lambda i, j: (i, col_block)),
                pl.BlockSpec((1, k), lambda i, j: (0, 0)),
                w_spec]
    args = [x, gain.reshape(1, k).astype(F32), w]
    if w_side is not None:
        in_specs.append(pl.BlockSpec((k, LANES), lambda i, j: (0, 0)))
        args.append(w_side)
    if first is None:
        out_specs = [pl.BlockSpec((tm, tn), lambda i, j: (i, j))]
        out_shape = [jax.ShapeDtypeStruct((m, n), out_dtype)]
    else:
        out_specs = [pl.BlockSpec((tm, tn), lambda i, j: (i, jnp.minimum(j, n_first - 1))),
                     pl.BlockSpec((tm, tn), lambda i, j: (i, jnp.maximum(j - n_first, 0)))]
        out_shape = [jax.ShapeDtypeStruct((m, first[0]), first[1]),
                     jax.ShapeDtypeStruct((m, n - first[0]), out_dtype)]
    if w_side is not None:
        out_specs.append(pl.BlockSpec((tm, LANES), lambda i, j: (i, 0)))
        out_shape.append(jax.ShapeDtypeStruct((m, LANES), BF16))
    outs = pl.pallas_call(
        functools.partial(_norm_matmul_kernel, has_side=w_side is not None,
                          n_first=n_first),
        grid=(m // tm, n // tn),
        in_specs=in_specs,
        out_specs=out_specs,
        out_shape=out_shape,
        scratch_shapes=[pltpu.VMEM((tm, k), BF16)],
        compiler_params=_cparams("parallel", "arbitrary"),
        name="norm_matmul",
    )(*args)
    return outs[0] if len(outs) == 1 else tuple(outs)


def _matmul_residual_kernel(y_ref, w_ref, x_ref, o_ref):
    o_ref[...] = x_ref[...] + jnp.dot(y_ref[...], w_ref[...],
                                      preferred_element_type=F32)


def _matmul_residual(y, w, x, *, tm):
    m, k = y.shape
    n = w.shape[1]
    return pl.pallas_call(
        _matmul_residual_kernel,
        grid=(m // tm,),
        in_specs=[pl.BlockSpec((tm, k), lambda i: (i, 0)),
                  pl.BlockSpec((k, n), lambda i: (0, 0)),
                  pl.BlockSpec((tm, n), lambda i: (i, 0))],
        out_specs=pl.BlockSpec((tm, n), lambda i: (i, 0)),
        out_shape=jax.ShapeDtypeStruct((m, n), F32),
        compiler_params=_cparams("parallel"),
        name="matmul_residual",
    )(y, w, x)


def _group_rmsnorm(x, gain, group):
    r = lax.broadcasted_iota(jnp.int32, (LANES, LANES), 0)
    c = lax.broadcasted_iota(jnp.int32, (LANES, LANES), 1)
    ones = jnp.where(r // group == c // group, 1.0, 0.0).astype(BF16)
    x2 = x * x
    hi = x2.astype(BF16)
    lo = (x2 - hi.astype(F32)).astype(BF16)
    ss = (jnp.dot(hi, ones, preferred_element_type=F32)
          + jnp.dot(lo, ones, preferred_element_type=F32))
    return x * lax.rsqrt(ss * (1.0 / group) + EPS) * gain


def _half_rmsnorm(x, gain, lo_mask):
    x2 = x * x
    lo = jnp.sum(jnp.where(lo_mask, x2, 0.0), axis=-1, keepdims=True)
    hi = jnp.sum(jnp.where(lo_mask, 0.0, x2), axis=-1, keepdims=True)
    ms = jnp.where(lo_mask, lo, hi) * (1.0 / 64)
    return x * lax.rsqrt(ms + EPS) * gain


def _store_vt_block(vt_ref, lead, v_rows):
    tq = v_rows.shape[0]
    r = lax.broadcasted_iota(jnp.int32, (LANES, LANES), 0)
    c = lax.broadcasted_iota(jnp.int32, (LANES, LANES), 1)
    eye = jnp.where(r == c, 1.0, 0.0).astype(BF16)
    vt_ref[(*lead, slice(0, LANES), slice(None))] = _nt_dot(eye, v_rows).astype(BF16)
    vt_ref[(*lead, slice(LANES, V_ROWS), slice(None))] = jnp.ones((BF16_ROWS, tq), BF16)


def _flash_init(s, vt):
    m = jnp.max(s, axis=0, keepdims=True)
    p = jnp.exp2(s - m).astype(BF16)
    return m, jnp.dot(vt, p, preferred_element_type=F32)


def _flash_update(s, vt, m, acc):
    m_new = jnp.maximum(m, jnp.max(s, axis=0, keepdims=True))
    alpha = jnp.exp2(m - m_new)
    p = jnp.exp2(s - m_new).astype(BF16)
    return m_new, alpha * acc + jnp.dot(vt, p, preferred_element_type=F32)


def _chunk_causal_flash(i, streams, acc_ref, s_ref):
    first = jnp.maximum(i - 1, 0)
    n_far = first

    def issue(blk, slot, add=None):
        for si, (scores, _, near_add) in enumerate(streams):
            s = scores(blk)
            s_ref[si, slot] = s if add is None else s + near_add(add)

    def update(blk, slot, ms):
        out = []
        for si, (_, values, _) in enumerate(streams):
            m, acc_ref[si] = _flash_update(s_ref[si, slot], values(blk), ms[si],
                                           acc_ref[si])
            out.append(m)
        return tuple(out)

    def group(t0, count, prefetch_next, ms):
        for k in range(count):
            if k + 1 < count or prefetch_next:
                issue(t0 + k + 1, (k + 1) % 2)
            ms = update(t0 + k, k % 2, ms)
        return ms

    issue(first, 0, add=0)
    issue(first + 1, 1, add=1)
    ms = []
    for si, (_, values, _) in enumerate(streams):
        m, acc_ref[si] = _flash_init(s_ref[si, 0], values(first))
        ms.append(m)
    issue(0, 0)
    ms = update(first + 1, 1, tuple(ms))

    full = jnp.maximum(n_far - 1, 0) // FAR_UNROLL
    rest = n_far - FAR_UNROLL * full
    ms = lax.fori_loop(0, full,
                       lambda g, ms: group(g * FAR_UNROLL, FAR_UNROLL, True, ms), ms)
    for count in range(1, FAR_UNROLL + 1):
        ms = lax.fori_loop(0, (rest == count).astype(jnp.int32),
                           lambda _, ms, count=count: group(n_far - count, count, False, ms),
                           ms)


def _normalised(acc):
    return acc[0:LANES] * (1.0 / acc[LANES:LANES + 1])


_T5_LARGE_THRESHOLDS = (12, 16, 23, 32, 46, 64, 91)


def _t5_tiles_kernel(rb_ref, o_ref, *, tq):
    h = pl.program_id(0)
    variant = pl.program_id(1)
    j = lax.broadcasted_iota(jnp.int32, (1, tq), 1)
    rel_row = jnp.where(j < tq // 2, -j, tq - j)
    n = jnp.abs(rel_row)
    large = jnp.full_like(n, 8)
    for t in _T5_LARGE_THRESHOLDS:
        large = large + (n >= t).astype(jnp.int32)
    bucket = jnp.where(rel_row > 0, T5_BUCKETS // 2, 0) + jnp.where(n < 8, n, large)
    far = rb_ref[T5_BUCKETS // 2 - 1, h]
    row = jnp.zeros((1, tq), F32)
    for b in range(T5_BUCKETS):
        row = jnp.where(bucket == b, (rb_ref[b, h] - far) * LOG2E, row)
    tile = pltpu.roll(jnp.broadcast_to(row, (2 * tq, tq)), 0, 1, stride=1, stride_axis=0)
    c = lax.broadcasted_iota(jnp.int32, (2 * tq, tq), 0)
    r = lax.broadcasted_iota(jnp.int32, (2 * tq, tq), 1)
    key_off = c - tq * variant
    bias = jnp.where(key_off - r > -_T5_LARGE_THRESHOLDS[-1], tile, 0.0)
    visible = (lax.shift_right_arithmetic(key_off, 6)
               <= lax.shift_right_arithmetic(r, 6))
    o_ref[0, 0] = jnp.where(visible, bias, NEG_INF)


def _t5_tiles(rel_bias, tq):
    nb, nh = rel_bias.shape
    return pl.pallas_call(
        functools.partial(_t5_tiles_kernel, tq=tq),
        grid=(nh, 2),
        in_specs=[pl.BlockSpec(memory_space=pltpu.SMEM)],
        out_specs=pl.BlockSpec((1, 1, 2 * tq, tq), lambda h, v: (h, v, 0, 0)),
        out_shape=jax.ShapeDtypeStruct((nh, 2, 2 * tq, tq), F32),
        compiler_params=_cparams("parallel", "arbitrary"),
        name="t5_tiles",
    )(rel_bias.astype(F32))


def _diff_attn_kernel(lamv_ref, q_ref, k_ref, v_ref, g_ref, bias_ref, qkg_ref,
                      sub_ref, o_ref, kn_ref, vt_ref, acc_ref, s_ref, *, tq, lam_init):
    i = pl.program_id(2)
    seq = k_ref.shape[1]
    lane = lax.broadcasted_iota(jnp.int32, (1, LANES), 1)
    lo_mask = lane < A_HEAD_DIM

    @pl.when(i == 0)
    def _():
        def body(c, carry):
            rows = pl.ds(pl.multiple_of(c * tq, tq), tq)
            kn_ref[rows, :] = _group_rmsnorm(
                k_ref[0, rows, :].astype(F32), qkg_ref[1:2, :], A_HEAD_DIM).astype(BF16)
            _store_vt_block(vt_ref, (c,), v_ref[0, rows, :])
            return carry
        lax.fori_loop(0, seq // tq, body, 0, unroll=2)

    lf = lamv_ref[...].astype(F32)
    lam = (jnp.exp(jnp.sum(lf[0:1] * lf[1:2], axis=-1, keepdims=True))
           - jnp.exp(jnp.sum(lf[2:3] * lf[3:4], axis=-1, keepdims=True)) + lam_init)

    qn = (_half_rmsnorm(q_ref[0].astype(F32), qkg_ref[0:1, :], lo_mask)
          * (A_HEAD_DIM ** -0.5 * LOG2E))
    q01 = jnp.concatenate([jnp.where(lo_mask, qn, 0.0).astype(BF16),
                           jnp.where(lo_mask, 0.0, qn).astype(BF16)], axis=0)

    def scores(blk):
        return _nt_dot(kn_ref[pl.ds(pl.multiple_of(blk * tq, tq), tq), :], q01)

    def near_add(j):
        bias_t = bias_ref[0, 0, j * tq:(j + 1) * tq, :]
        return jnp.concatenate([bias_t, bias_t], axis=1)

    _chunk_causal_flash(i, [(scores, lambda blk: vt_ref[blk], near_add)], acc_ref,
                        s_ref)

    o01 = _normalised(acc_ref[0])
    o_t = o01[:, 0:tq] - lam * o01[:, tq:]
    ms_t = jnp.mean(o_t * o_t, axis=0, keepdims=True)
    o_t = o_t * lax.rsqrt(ms_t + EPS) * sub_ref[...] * (1.0 - lam_init)
    o_ref[0] = (o_t.T * _silu(g_ref[0].astype(F32))).astype(o_ref.dtype)


def _diff_attention(proj, lam_vecs, qk_g, subln_g, bias_tiles, layer_idx):
    b, s, four_w = proj.shape
    width = four_w // 4
    hblocks = width // LANES
    tq = ATTN_TQ
    lam_init = 0.8 - 0.6 * math.exp(-0.3 * layer_idx)
    qkg = jnp.concatenate([qk_g, qk_g], axis=-1).astype(F32)
    return pl.pallas_call(
        functools.partial(_diff_attn_kernel, tq=tq, lam_init=lam_init),
        grid=(b, hblocks, s // tq),
        in_specs=[
            pl.BlockSpec((4, A_HEAD_DIM), lambda bi, h, i: (0, 0)),
            pl.BlockSpec((1, tq, LANES), lambda bi, h, i: (bi, i, h)),
            pl.BlockSpec((1, s, LANES), lambda bi, h, i: (bi, 0, hblocks + h)),
            pl.BlockSpec((1, s, LANES), lambda bi, h, i: (bi, 0, 2 * hblocks + h)),
            pl.BlockSpec((1, tq, LANES), lambda bi, h, i: (bi, i, 3 * hblocks + h)),
            pl.BlockSpec((1, 1, 2 * tq, tq),
                         lambda bi, h, i: (h, jnp.minimum(i, 1), 0, 0)),
            pl.BlockSpec((2, LANES), lambda bi, h, i: (0, 0)),
            pl.BlockSpec((LANES, 1), lambda bi, h, i: (0, 0)),
        ],
        out_specs=pl.BlockSpec((1, tq, LANES), lambda bi, h, i: (bi, i, h)),
        out_shape=jax.ShapeDtypeStruct((b, s, width), BF16),
        scratch_shapes=[pltpu.VMEM((s, LANES), BF16),
                        pltpu.VMEM((s // tq, V_ROWS, tq), BF16),
                        pltpu.VMEM((1, V_ROWS, 2 * tq), F32),
                        pltpu.VMEM((1, 2, tq, 2 * tq), F32)],
        compiler_params=_cparams("parallel", "parallel", "arbitrary"),
        name="diff_attention",
    )(lam_vecs.astype(F32), proj, proj, proj, proj, bias_tiles, qkg,
      subln_g.reshape(LANES, 1).astype(F32))


def _gla_kernel(q_ref, k_ref, v_ref, g_ref, lr_ref, wgt_ref, gb_ref, og_ref,
                cm_ref, o_ref, state_ref, *, ts, dk):
    @pl.when(pl.program_id(2) == 0)
    def _():
        state_ref[...] = jnp.zeros_like(state_ref)

    heads = q_ref.shape[2] // dk
    dv = v_ref.shape[2] // heads
    span = cm_ref.shape[0]
    la_t, rest_t, kd_t = [], [], []
    for hh in range(heads):
        hk = slice(hh * dk, (hh + 1) * dk)
        z = _nt_dot(wgt_ref[hk, :], lr_ref[0]) + gb_ref[hk, :]
        la = -_softplus(-z) * (1.0 / B_GATE_TAU)
        la_hi = la.astype(BF16)
        la_lo = (la - la_hi.astype(F32)).astype(BF16)
        rest = jnp.concatenate(
            [jnp.dot(la_hi[:, s0:s0 + span], cm_ref[...], preferred_element_type=F32)
             + jnp.dot(la_lo[:, s0:s0 + span], cm_ref[...], preferred_element_type=F32)
             for s0 in range(0, ts, span)], axis=1)
        la_t.append(la)
        rest_t.append(rest)
        kd_t.append((k_ref[0, :, hk].astype(F32).T * jnp.exp(rest)).astype(BF16))

    lane = lax.broadcasted_iota(jnp.int32, (1, LANES), 1)
    first_chunk = lane < CHUNK
    states = [state_ref[hh] for hh in range(heads)]
    for c in range(ts // CHUNK):
        group = slice((c // 2) * LANES, (c // 2 + 1) * LANES)
        keep = first_chunk if c % 2 == 0 else jnp.logical_not(first_chunk)
        t0 = slice(c * CHUNK, c * CHUNK + 1)
        rows = slice(c * CHUNK, (c + 1) * CHUNK)
        for hh in range(heads):
            hk = slice(hh * dk, (hh + 1) * dk)
            hv = slice(hh * dv, (hh + 1) * dv)
            kd_c = jnp.where(keep, kd_t[hh][:, group], jnp.zeros((), BF16))
            states[hh] = (jnp.exp(rest_t[hh][:, t0] + la_t[hh][:, t0]) * states[hh]
                          + jnp.dot(kd_c, v_ref[0, group, hv], preferred_element_type=F32))
            q = (q_ref[0, rows, hk].astype(F32) * (dk ** -0.5)).astype(BF16)
            o = jnp.dot(q, states[hh].astype(BF16), preferred_element_type=F32)
            ms = jnp.mean(o * o, axis=-1, keepdims=True)
            o = o * lax.rsqrt(ms + EPS) * og_ref[...]
            o_ref[0, rows, hv] = (o * _silu(g_ref[0, rows, hv].astype(F32))).astype(o_ref.dtype)
    for hh in range(heads):
        state_ref[hh] = states[hh]


GLA_HEADS_PER_STEP = 2


def _gla(proj, lr, w_gate_t, gate_bias, out_g, *, ts):
    b, s, n = proj.shape
    heads = B_HEADS
    dk = n // (6 * heads)
    dv = 2 * dk
    hp = GLA_HEADS_PER_STEP
    steps = heads // hp
    wk, wv = hp * dk, hp * dv
    k0 = (heads * dk) // wk
    v0 = (2 * heads * dk) // wv
    g0 = v0 + steps
    span = 2 * LANES
    t_in = jnp.arange(span)[:, None]
    t_out = jnp.arange(span)[None, :]
    chunk_masks = (((t_in // CHUNK) == (t_out // CHUNK)) & (t_in > t_out)).astype(BF16)
    return pl.pallas_call(
        functools.partial(_gla_kernel, ts=ts, dk=dk),
        grid=(b, steps, s // ts),
        in_specs=[
            pl.BlockSpec((1, ts, wk), lambda bi, h, i: (bi, i, h)),
            pl.BlockSpec((1, ts, wk), lambda bi, h, i: (bi, i, k0 + h)),
            pl.BlockSpec((1, ts, wv), lambda bi, h, i: (bi, i, v0 + h)),
            pl.BlockSpec((1, ts, wv), lambda bi, h, i: (bi, i, g0 + h)),
            pl.BlockSpec((1, ts, LANES), lambda bi, h, i: (bi, i, 0)),
            pl.BlockSpec((wk, LANES), lambda bi, h, i: (h, 0)),
            pl.BlockSpec((wk, 1), lambda bi, h, i: (h, 0)),
            pl.BlockSpec((1, dv), lambda bi, h, i: (0, 0)),
            pl.BlockSpec((span, span), lambda bi, h, i: (0, 0)),
        ],
        out_specs=pl.BlockSpec((1, ts, wv), lambda bi, h, i: (bi, i, h)),
        out_shape=jax.ShapeDtypeStruct((b, s, heads * dv), BF16),
        scratch_shapes=[pltpu.VMEM((hp, dk, dv), F32)],
        compiler_params=_cparams("parallel", "parallel", "arbitrary"),
        name="gla",
    )(proj, proj, proj, proj, lr, w_gate_t, gate_bias.reshape(-1, 1).astype(F32),
      out_g.reshape(1, dv).astype(F32), chunk_masks)


def _rglru_kernel(u_ref, g_ref, cw_ref, cb_ref, wr_ref, br_ref, wi_ref, bi_ref,
                  lam_ref, o_ref, ubuf_ref, xc_ref, a_ref, b_ref, h_ref, *, ts):
    width = u_ref.shape[2]
    bd = width // C_BLOCKS

    @pl.when(pl.program_id(1) == 0)
    def _():
        ubuf_ref[0:SUBLANES, :] = jnp.zeros((SUBLANES, width), F32)
        h_ref[...] = jnp.zeros_like(h_ref)

    ubuf_ref[SUBLANES:SUBLANES + ts, :] = u_ref[0]
    ue = ubuf_ref[...]
    xc = ue * cw_ref[0:1, :]
    for t in range(1, C_CONV):
        xc = pltpu.roll(xc, 1, 0) + ue * cw_ref[t:t + 1, :]
    xc_ref[...] = xc[SUBLANES:, :] + cb_ref[...]
    ubuf_ref[0:SUBLANES, :] = ubuf_ref[ts:ts + SUBLANES, :]

    rate = -C_C * _softplus(-lam_ref[...])
    for n in range(C_BLOCKS):
        cols = slice(n * bd, (n + 1) * bd)
        xb = xc_ref[:, cols]
        xb16 = xb.astype(BF16)
        r = _sigmoid(jnp.dot(xb16, wr_ref[n], preferred_element_type=F32)
                     + br_ref[:, cols])
        gate_i = _sigmoid(jnp.dot(xb16, wi_ref[n], preferred_element_type=F32)
                          + bi_ref[:, cols])
        log_a = r * rate[:, cols]
        a_ref[:, cols] = jnp.exp(log_a)
        b_ref[:, cols] = jnp.sqrt(_one_minus_exp2(log_a)) * (gate_i * xb)

    row = lax.broadcasted_iota(jnp.int32, (SUBLANES, width), 0)

    def tile(t, h_prev):
        rows = pl.ds(pl.multiple_of(t * SUBLANES, SUBLANES), SUBLANES)
        a = a_ref[rows, :]
        b = b_ref[rows, :]
        d = 1
        while d < SUBLANES:
            a_up = pltpu.roll(a, d, 0)
            b_up = pltpu.roll(b, d, 0)
            keep = row >= d
            b = jnp.where(keep, a * b_up + b, b)
            a = jnp.where(keep, a * a_up, a)
            d *= 2
        h = b + a * h_prev
        b_ref[rows, :] = h
        return h[SUBLANES - 1:SUBLANES, :]

    h_ref[...] = lax.fori_loop(0, ts // SUBLANES, tile, h_ref[...])
    o_ref[0] = (b_ref[...] * _silu(g_ref[0].astype(F32))).astype(o_ref.dtype)


def _rglru(u, gate, gate_block, conv_w, conv_b, w_rg, b_rg, w_ig, b_ig, lam, *, ts):
    b, s, width = u.shape
    bd = width // C_BLOCKS
    row = lambda a: a.reshape(1, width).astype(F32)
    const2 = lambda bi, i: (0, 0)
    return pl.pallas_call(
        functools.partial(_rglru_kernel, ts=ts),
        grid=(b, s // ts),
        in_specs=[
            pl.BlockSpec((1, ts, width), lambda bi, i: (bi, i, 0)),
            pl.BlockSpec((1, ts, width), lambda bi, i: (bi, i, gate_block)),
            pl.BlockSpec((C_CONV, width), const2),
            pl.BlockSpec((1, width), const2),
            pl.BlockSpec((C_BLOCKS, bd, bd), lambda bi, i: (0, 0, 0)),
            pl.BlockSpec((1, width), const2),
            pl.BlockSpec((C_BLOCKS, bd, bd), lambda bi, i: (0, 0, 0)),
            pl.BlockSpec((1, width), const2),
            pl.BlockSpec((1, width), const2),
        ],
        out_specs=pl.BlockSpec((1, ts, width), lambda bi, i: (bi, i, 0)),
        out_shape=jax.ShapeDtypeStruct((b, s, width), BF16),
        scratch_shapes=[pltpu.VMEM((ts + SUBLANES, width), F32),
                        pltpu.VMEM((ts, width), F32),
                        pltpu.VMEM((ts, width), F32),
                        pltpu.VMEM((ts, width), F32),
                        pltpu.VMEM((1, width), F32)],
        compiler_params=_cparams("parallel", "arbitrary"),
        name="rglru",
    )(u, gate, conv_w.astype(F32), row(conv_b), w_rg.astype(BF16), row(b_rg),
      w_ig.astype(BF16), row(b_ig), row(lam))


def _swap_rope_halves(x):
    lane = lax.broadcasted_iota(jnp.int32, (1, LANES), 1)
    first = (lane % D_ROPE) < (D_ROPE // 2)
    return jnp.where(first, pltpu.roll(x, LANES - D_ROPE // 2, 1),
                     pltpu.roll(x, D_ROPE // 2, 1))


def _rope(x, cos, sin_signed):
    return x * cos + _swap_rope_halves(x) * sin_signed


def _mla_attn_kernel(qn_ref, qp_ref, kn_ref, kp_ref, v_ref, g_ref, cos_ref,
                     sin_ref, gains_ref, o_ref, kcat_ref, vt_ref, acc_ref, s_ref,
                     *, tq):
    i = pl.program_id(2)
    seq = kn_ref.shape[1]
    lane = lax.broadcasted_iota(jnp.int32, (1, LANES), 1)
    lo_mask = lane < D_ROPE
    g_q_nope, g_q_pe = gains_ref[0:1, :], gains_ref[1:2, :]
    g_k_nope, g_k_pe = gains_ref[2:3, :], gains_ref[3:4, :]

    @pl.when(i == 0)
    def _():
        def body(c, carry):
            rows = pl.ds(pl.multiple_of(c * tq, tq), tq)
            kp = _rope(_group_rmsnorm(kp_ref[0, rows, :].astype(F32), g_k_pe, D_ROPE),
                       cos_ref[rows, :], sin_ref[rows, :])
            kn = kn_ref[0, rows, :].astype(F32)
            kcat_ref[0, rows, 0:LANES] = _group_rmsnorm(
                kn[:, 0:LANES], g_k_nope, D_NOPE).astype(BF16)
            kcat_ref[0, rows, LANES:] = jnp.where(lo_mask, kp, 0.0).astype(BF16)
            kcat_ref[1, rows, 0:LANES] = _group_rmsnorm(
                kn[:, LANES:], g_k_nope, D_NOPE).astype(BF16)
            kcat_ref[1, rows, LANES:] = jnp.where(lo_mask, 0.0, kp).astype(BF16)
            _store_vt_block(vt_ref, (0, c), v_ref[0, rows, 0:D_V])
            _store_vt_block(vt_ref, (1, c), v_ref[0, rows, D_V:])
            return carry
        lax.fori_loop(0, seq // tq, body, 0, unroll=2)

    scale = (D_NOPE + D_ROPE) ** -0.5 * LOG2E
    qrows = pl.ds(pl.multiple_of(i * tq, tq), tq)
    qp = _rope(_group_rmsnorm(qp_ref[0].astype(F32), g_q_pe, D_ROPE),
               cos_ref[qrows, :], sin_ref[qrows, :]) * scale
    qn = qn_ref[0].astype(F32)
    qcat = (
        jnp.concatenate([(_group_rmsnorm(qn[:, 0:LANES], g_q_nope, D_NOPE)
                          * scale).astype(BF16),
                         jnp.where(lo_mask, qp, 0.0).astype(BF16)], axis=-1),
        jnp.concatenate([(_group_rmsnorm(qn[:, LANES:], g_q_nope, D_NOPE)
                          * scale).astype(BF16),
                         jnp.where(lo_mask, 0.0, qp).astype(BF16)], axis=-1),
    )

    def near_add(j):
        c = lax.broadcasted_iota(jnp.int32, (tq, tq), 0) + tq * (j - jnp.minimum(i, 1))
        r = lax.broadcasted_iota(jnp.int32, (tq, tq), 1)
        visible = (lax.shift_right_arithmetic(c, 6) <= lax.shift_right_arithmetic(r, 6))
        return jnp.where(visible, 0.0, NEG_INF).astype(F32)

    def stream(hh):
        def scores(blk):
            rows = pl.ds(pl.multiple_of(blk * tq, tq), tq)
            return _nt_dot(kcat_ref[hh, rows, :], qcat[hh])
        return scores, lambda blk: vt_ref[hh, blk], near_add

    _chunk_causal_flash(i, [stream(0), stream(1)], acc_ref, s_ref)

    g = g_ref[0].astype(F32)
    o_ref[0, :, 0:D_V] = (_normalised(acc_ref[0]).T * _silu(g[:, 0:D_V])).astype(o_ref.dtype)
    o_ref[0, :, D_V:] = (_normalised(acc_ref[1]).T * _silu(g[:, D_V:])).astype(o_ref.dtype)


def _mla_attention(q, kv, kpe, proj, gate_block0, cos, sin_signed, gains):
    b, s, _ = q.shape
    heads = D_HEADS
    pairs = heads // 2
    tq = ATTN_TQ
    pw = 2 * LANES
    return pl.pallas_call(
        functools.partial(_mla_attn_kernel, tq=tq),
        grid=(b, pairs, s // tq),
        in_specs=[
            pl.BlockSpec((1, tq, pw), lambda bi, p, i: (bi, i, p)),
            pl.BlockSpec((1, tq, LANES), lambda bi, p, i: (bi, i, 2 * pairs + p)),
            pl.BlockSpec((1, s, pw), lambda bi, p, i: (bi, 0, p)),
            pl.BlockSpec((1, s, LANES), lambda bi, p, i: (bi, 0, 0)),
            pl.BlockSpec((1, s, pw), lambda bi, p, i: (bi, 0, pairs + p)),
            pl.BlockSpec((1, tq, pw), lambda bi, p, i: (bi, i, gate_block0 + p)),
            pl.BlockSpec((s, LANES), lambda bi, p, i: (0, 0)),
            pl.BlockSpec((s, LANES), lambda bi, p, i: (0, 0)),
            pl.BlockSpec((4, LANES), lambda bi, p, i: (0, 0)),
        ],
        out_specs=pl.BlockSpec((1, tq, pw), lambda bi, p, i: (bi, i, p)),
        out_shape=jax.ShapeDtypeStruct((b, s, heads * D_V), BF16),
        scratch_shapes=[pltpu.VMEM((2, s, pw), BF16),
                        pltpu.VMEM((2, s // tq, V_ROWS, tq), BF16),
                        pltpu.VMEM((2, V_ROWS, tq), F32),
                        pltpu.VMEM((2, 2, tq, tq), F32)],
        compiler_params=_cparams("parallel", "parallel", "arbitrary"),
        name="mla_attention",
    )(q, q, kv, kpe, kv, proj, cos, sin_signed, gains)


MM_TM = 1024
MM_TN = 1024
RES_TM = 512


def _layer_diff(x2, b, s, norm_g, w_in, qk_g, lam_vecs, subln_g, w_out, bias_tiles,
                layer_idx):
    proj = _norm_matmul(x2, norm_g, w_in, out_dtype=BF16, tm=MM_TM, tn=MM_TN)
    y = _diff_attention(proj.reshape(b, s, -1), lam_vecs, qk_g, subln_g,
                        bias_tiles, layer_idx)
    return _matmul_residual(y.reshape(b * s, -1), w_out.astype(BF16), x2,
                            tm=RES_TM)


def _layer_gla(x2, b, s, norm_g, w_in_all, layer, w_gate, gate_bias, out_g, w_out):
    w_in = w_in_all[layer]
    rank, hdk = w_gate.shape
    n_main = w_in.shape[1] - rank
    w_lr = jnp.pad(w_in[:, n_main:], ((0, 0), (0, LANES - rank))).astype(BF16)
    w_gate_t = jnp.pad(w_gate, ((0, LANES - rank), (0, 0))).T.astype(BF16)
    proj, lr = _norm_matmul(x2, norm_g, w_in_all, out_dtype=BF16, tm=MM_TM, tn=MM_TN,
                            n_cols=n_main, w_side=w_lr, layer=layer)
    y = _gla(proj.reshape(b, s, -1), lr.reshape(b, s, LANES), w_gate_t, gate_bias,
             out_g, ts=512)
    return _matmul_residual(y.reshape(b * s, -1), w_out.astype(BF16), x2,
                            tm=RES_TM)


def _layer_rglru(x2, b, s, norm_g, w_in, conv_w, conv_b, w_rg, b_rg, w_ig, b_ig,
                 lam, w_out):
    width = w_in.shape[1] // 2
    u, gate = _norm_matmul(x2, norm_g, w_in.astype(BF16), out_dtype=BF16,
                           tm=MM_TM, tn=MM_TN, first=(width, F32))
    y = _rglru(u.reshape(b, s, width), gate.reshape(b, s, width), 0, conv_w, conv_b,
               w_rg, b_rg, w_ig, b_ig, lam, ts=256)
    return _matmul_residual(y.reshape(b * s, -1), w_out.astype(BF16), x2,
                            tm=RES_TM)


def _rope_tables(s):
    half = D_ROPE // 2
    inv = ROPE_THETA ** (-jnp.arange(half, dtype=F32) / half)
    ang = jnp.arange(s, dtype=F32)[:, None] * inv[None, :]
    cos, sin = jnp.cos(ang), jnp.sin(ang)
    return (jnp.concatenate([cos, cos, cos, cos], axis=-1),
            jnp.concatenate([-sin, sin, -sin, sin], axis=-1))


def _layer_mla(x2, b, s, norm_g, w_in, q_lat_g, kv_lat_g, w_uq, w_ukv, qk_g, w_out):
    q_rank, kv_rank = q_lat_g.shape[0], kv_lat_g.shape[0]
    heads = D_HEADS
    lat = q_rank + kv_rank
    w_main = jnp.concatenate([w_in[:, :lat], w_in[:, lat + D_ROPE:]], axis=1)
    w_kpe = w_in[:, lat:lat + D_ROPE]
    w_kpe = jnp.concatenate([w_kpe, w_kpe], axis=1).astype(BF16)
    uq = w_uq.reshape(q_rank, heads, D_NOPE + D_ROPE)
    w_uq_p = jnp.concatenate([uq[:, :, :D_NOPE].reshape(q_rank, -1),
                              uq[:, :, D_NOPE:].reshape(q_rank, -1)], axis=1).astype(BF16)
    ukv = w_ukv.reshape(kv_rank, heads, D_NOPE + D_V)
    w_ukv_p = jnp.concatenate([ukv[:, :, :D_NOPE].reshape(kv_rank, -1),
                               ukv[:, :, D_NOPE:].reshape(kv_rank, -1)], axis=1).astype(BF16)
    dup = lambda v: jnp.concatenate([v, v])
    gains = jnp.stack([qk_g[0, :D_NOPE], dup(qk_g[0, D_NOPE:]),
                       qk_g[1, :D_NOPE], dup(qk_g[1, D_NOPE:])]).astype(F32)
    cos, sin_signed = _rope_tables(s)

    proj, kpe = _norm_matmul(x2, norm_g, w_main, out_dtype=BF16, tm=MM_TM, tn=MM_TN,
                             w_side=w_kpe)
    q = _norm_matmul(proj, q_lat_g, w_uq_p, out_dtype=BF16, tm=MM_TM,
                     tn=w_uq_p.shape[1], col_block=0)
    kv = _norm_matmul(proj, kv_lat_g, w_ukv_p, out_dtype=BF16, tm=MM_TM,
                      tn=w_ukv_p.shape[1], col_block=1)
    gate_block0 = lat // (2 * LANES)
    y = _mla_attention(q.reshape(b, s, -1), kv.reshape(b, s, -1),
                       kpe.reshape(b, s, LANES), proj.reshape(b, s, -1),
                       gate_block0, cos, sin_signed, gains)
    return _matmul_residual(y.reshape(b * s, -1), w_out.astype(BF16), x2,
                            tm=RES_TM)


def kernel(x, norm_g, rel_bias, a_w_in, a_qk_g, a_lambda, a_subln_g, a_w_out, b_w_in, b_w_gate, b_gate_bias, b_out_g, b_w_out, c_w_in, c_conv_w, c_conv_b, c_w_rgate, c_b_rgate, c_w_igate, c_b_igate, c_lambda, c_w_out, d_w_in, d_q_lat_g, d_kv_lat_g, d_w_uq, d_w_ukv, d_qk_g, d_w_out):
    b, s, d = x.shape
    depth = norm_g.shape[0]
    x2 = x.reshape(b * s, d)
    bias_tiles = _t5_tiles(rel_bias, ATTN_TQ)
    for i in range(depth):
        m, j = i % 4, i // 4
        if m == 0:
            x2 = _layer_diff(x2, b, s, norm_g[i], a_w_in[j], a_qk_g[j], a_lambda[j],
                             a_subln_g[j], a_w_out[j], bias_tiles, i)
        elif m == 1:
            x2 = _layer_gla(x2, b, s, norm_g[i], b_w_in, j, b_w_gate[j],
                            b_gate_bias[j], b_out_g[j], b_w_out[j])
        elif m == 2:
            x2 = _layer_rglru(x2, b, s, norm_g[i], c_w_in[j], c_conv_w[j], c_conv_b[j],
                              c_w_rgate[j], c_b_rgate[j], c_w_igate[j], c_b_igate[j],
                              c_lambda[j], c_w_out[j])
        else:
            x2 = _layer_mla(x2, b, s, norm_g[i], d_w_in[j], d_q_lat_g[j],
                            d_kv_lat_g[j], d_w_uq[j], d_w_ukv[j], d_qk_g[j], d_w_out[j])
    return x2.reshape(b, s, d)
```

```python
import functools
import math

import jax
import jax.numpy as jnp
from jax import lax
from jax.experimental import pallas as pl
from jax.experimental.pallas import tpu as pltpu

F32 = jnp.float32
BF16 = jnp.bfloat16

EPS = 1e-6
NEG_INF = -1e30
LOG2E = math.log2(math.e)
CHUNK = 64
LANES = 128
SUBLANES = 8
BF16_ROWS = 16
V7X_VMEM_BYTES = 64 * 1024 * 1024
COMPILER_SCRATCH_BYTES = 8 * 1024 * 1024
VMEM_LIMIT = V7X_VMEM_BYTES - COMPILER_SCRATCH_BYTES

A_HEADS = 16
A_HEAD_DIM = 64
T5_BUCKETS = 32
B_HEADS = 4
B_GATE_TAU = 16.0
C_BLOCKS = 8
C_CONV = 4
C_C = 8.0
D_HEADS = 16
D_NOPE = 128
D_ROPE = 64
D_V = 128
ROPE_THETA = 10000.0

ATTN_TQ = 512
V_ROWS = LANES + BF16_ROWS
FAR_UNROLL = 4


def _cparams(*semantics):
    return pltpu.CompilerParams(dimension_semantics=semantics,
                                vmem_limit_bytes=VMEM_LIMIT)


def _sigmoid(x):
    return 0.5 * jnp.tanh(0.5 * x) + 0.5


def _silu(g):
    h = 0.5 * g
    return h * jnp.tanh(h) + h


def _softplus(x):
    return jnp.maximum(x, 0.0) + jnp.log(1.0 + jnp.exp(-jnp.abs(x)))


def _one_minus_exp2(y):
    t = jnp.tanh(y)
    return (-2.0 * t) / (1.0 - t)


def _nt_dot(a, b):
    return lax.dot_general(a, b, (((1,), (1,)), ((), ())),
                           preferred_element_type=F32)


def _norm_matmul_kernel(*refs, has_side, n_first):
    refs = list(refs)
    x_ref, g_ref, w_ref = refs[:3]
    del refs[:3]
    ws_ref = refs.pop(0) if has_side else None
    o_ref = refs.pop(0)
    rest_ref = refs.pop(0) if n_first is not None else None
    side_ref = refs.pop(0) if has_side else None
    h_ref = refs.pop(0)
    j = pl.program_id(1)

    @pl.when(j == 0)
    def _():
        x = x_ref[...].astype(F32)
        ms = jnp.mean(x * x, axis=-1, keepdims=True)
        h_ref[...] = (x * lax.rsqrt(ms + EPS) * g_ref[...]).astype(BF16)
        if has_side:
            side_ref[...] = jnp.dot(h_ref[...], ws_ref[...],
                                    preferred_element_type=F32).astype(side_ref.dtype)

    w = w_ref[0] if len(w_ref.shape) == 3 else w_ref[...]
    if w.dtype != BF16:
        w = w.astype(BF16)
    res = jnp.dot(h_ref[...], w, preferred_element_type=F32)
    if n_first is None:
        o_ref[...] = res.astype(o_ref.dtype)
    else:
        @pl.when(j < n_first)
        def _():
            o_ref[...] = res.astype(o_ref.dtype)

        @pl.when(j >= n_first)
        def _():
            rest_ref[...] = res.astype(rest_ref.dtype)


def _norm_matmul(x, gain, w, *, out_dtype, tm, tn, col_block=0, n_cols=None,
                 w_side=None, first=None, layer=0):
    m = x.shape[0]
    k = w.shape[-2]
    n = w.shape[-1] if n_cols is None else n_cols
    n_first = None if first is None else first[0] // tn
    if w.ndim == 3:
        w_spec = pl.BlockSpec((1, k, tn), lambda i, j: (layer, 0, j))
    else:
        w_spec = pl.BlockSpec((k, tn), lambda i, j: (0, j))
    in_specs = [pl.BlockSpec((tm, k), lambda i, j: (i, col_block)),
                pl.BlockSpec((1, k), lambda i, j: (0, 0)),
                w_spec]
    args = [x, gain.reshape(1, k).astype(F32), w]
    if w_side is not None:
        in_specs.append(pl.BlockSpec((k, LANES), lambda i, j: (0, 0)))
        args.append(w_side)
    if first is None:
        out_specs = [pl.BlockSpec((tm, tn), lambda i, j: (i, j))]
        out_shape = [jax.ShapeDtypeStruct((m, n), out_dtype)]
    else:
        out_specs = [pl.BlockSpec((tm, tn), lambda i, j: (i, jnp.minimum(j, n_first - 1))),
                     pl.BlockSpec((tm, tn), lambda i, j: (i, jnp.maximum(j - n_first, 0)))]
        out_shape = [jax.ShapeDtypeStruct((m, first[0]), first[1]),
                     jax.ShapeDtypeStruct((m, n - first[0]), out_dtype)]
    if w_side is not None:
        out_specs.append(pl.BlockSpec((tm, LANES), lambda i, j: (i, 0)))
        out_shape.append(jax.ShapeDtypeStruct((m, LANES), BF16))
    outs = pl.pallas_call(
        functools.partial(_norm_matmul_kernel, has_side=w_side is not None,
                          n_first=n_first),
        grid=(m // tm, n // tn),
        in_specs=in_specs,
        out_specs=out_specs,
        out_shape=out_shape,
        scratch_shapes=[pltpu.VMEM((tm, k), BF16)],
        compiler_params=_cparams("parallel", "arbitrary"),
        name="norm_matmul",
    )(*args)
    return outs[0] if len(outs) == 1 else tuple(outs)


def _matmul_residual_kernel(y_ref, w_ref, x_ref, o_ref):
    o_ref[...] = x_ref[...] + jnp.dot(y_ref[...], w_ref[...],
                                      preferred_element_type=F32)


def _matmul_residual(y, w, x, *, tm):
    m, k = y.shape
    n = w.shape[1]
    return pl.pallas_call(
        _matmul_residual_kernel,
        grid=(m // tm,),
        in_specs=[pl.BlockSpec((tm, k), lambda i: (i, 0)),
                  pl.BlockSpec((k, n), lambda i: (0, 0)),
                  pl.BlockSpec((tm, n), lambda i: (i, 0))],
        out_specs=pl.BlockSpec((tm, n), lambda i: (i, 0)),
        out_shape=jax.ShapeDtypeStruct((m, n), F32),
        compiler_params=_cparams("parallel"),
        name="matmul_residual",
    )(y, w, x)


def _group_rmsnorm(x, gain, group):
    r = lax.broadcasted_iota(jnp.int32, (LANES, LANES), 0)
    c = lax.broadcasted_iota(jnp.int32, (LANES, LANES), 1)
    ones = jnp.where(r // group == c // group, 1.0, 0.0).astype(BF16)
    x2 = x * x
    hi = x2.astype(BF16)
    lo = (x2 - hi.astype(F32)).astype(BF16)
    ss = (jnp.dot(hi, ones, preferred_element_type=F32)
          + jnp.dot(lo, ones, preferred_element_type=F32))
    return x * lax.rsqrt(ss * (1.0 / group) + EPS) * gain


def _half_rmsnorm(x, gain, lo_mask):
    x2 = x * x
    lo = jnp.sum(jnp.where(lo_mask, x2, 0.0), axis=-1, keepdims=True)
    hi = jnp.sum(jnp.where(lo_mask, 0.0, x2), axis=-1, keepdims=True)
    ms = jnp.where(lo_mask, lo, hi) * (1.0 / 64)
    return x * lax.rsqrt(ms + EPS) * gain


def _store_vt_block(vt_ref, lead, v_rows):
    tq = v_rows.shape[0]
    r = lax.broadcasted_iota(jnp.int32, (LANES, LANES), 0)
    c = lax.broadcasted_iota(jnp.int32, (LANES, LANES), 1)
    eye = jnp.where(r == c, 1.0, 0.0).astype(BF16)
    vt_ref[(*lead, slice(0, LANES), slice(None))] = _nt_dot(eye, v_rows).astype(BF16)
    vt_ref[(*lead, slice(LANES, V_ROWS), slice(None))] = jnp.ones((BF16_ROWS, tq), BF16)


def _flash_init(s, vt):
    m = jnp.max(s, axis=0, keepdims=True)
    p = jnp.exp2(s - m).astype(BF16)
    return m, jnp.dot(vt, p, preferred_element_type=F32)


def _flash_update(s, vt, m, acc):
    m_new = jnp.maximum(m, jnp.max(s, axis=0, keepdims=True))
    alpha = jnp.exp2(m - m_new)
    p = jnp.exp2(s - m_new).astype(BF16)
    return m_new, alpha * acc + jnp.dot(vt, p, preferred_element_type=F32)


def _chunk_causal_flash(i, streams, acc_ref, s_ref):
    first = jnp.maximum(i - 1, 0)
    n_far = first

    def issue(blk, slot, add=None):
        for si, (scores, _, near_add) in enumerate(streams):
            s = scores(blk)
            s_ref[si, slot] = s if add is None else s + near_add(add)

    def update(blk, slot, ms):
        out = []
        for si, (_, values, _) in enumerate(streams):
            m, acc_ref[si] = _flash_update(s_ref[si, slot], values(blk), ms[si],
                                           acc_ref[si])
            out.append(m)
        return tuple(out)

    def group(t0, count, prefetch_next, ms):
        for k in range(count):
            if k + 1 < count or prefetch_next:
                issue(t0 + k + 1, (k + 1) % 2)
            ms = update(t0 + k, k % 2, ms)
        return ms

    def init(blk):
        ms = []
        for si, (_, values, _) in enumerate(streams):
            m, acc_ref[si] = _flash_init(s_ref[si, 0], values(blk))
            ms.append(m)
        return tuple(ms)

    def near_first():
        issue(0, 0, add=0)
        return init(0)

    def near_pair(prefetch_far):
        def run():
            issue(first, 0, add=0)
            issue(first + 1, 1, add=1)
            ms = init(first)
            if prefetch_far:
                issue(0, 0)
            return update(first + 1, 1, ms)
        return run

    ms = lax.cond(i == 0, near_first,
                  lambda: lax.cond(i == 1, near_pair(False), near_pair(True)))

    full = jnp.maximum(n_far - 1, 0) // FAR_UNROLL
    rest = n_far - FAR_UNROLL * full
    ms = lax.fori_loop(0, full,
                       lambda g, ms: group(g * FAR_UNROLL, FAR_UNROLL, True, ms), ms)
    for count in range(1, FAR_UNROLL + 1):
        ms = lax.fori_loop(0, (rest == count).astype(jnp.int32),
                           lambda _, ms, count=count: group(n_far - count, count, False, ms),
                           ms)


def _normalised(acc):
    return acc[0:LANES] * (1.0 / acc[LANES:LANES + 1])


_T5_LARGE_THRESHOLDS = (12, 16, 23, 32, 46, 64, 91)


def _t5_tiles_kernel(rb_ref, o_ref, *, tq):
    h = pl.program_id(0)
    variant = pl.program_id(1)
    j = lax.broadcasted_iota(jnp.int32, (1, tq), 1)
    rel_row = jnp.where(j < tq // 2, -j, tq - j)
    n = jnp.abs(rel_row)
    large = jnp.full_like(n, 8)
    for t in _T5_LARGE_THRESHOLDS:
        large = large + (n >= t).astype(jnp.int32)
    bucket = jnp.where(rel_row > 0, T5_BUCKETS // 2, 0) + jnp.where(n < 8, n, large)
    far = rb_ref[T5_BUCKETS // 2 - 1, h]
    row = jnp.zeros((1, tq), F32)
    for b in range(T5_BUCKETS):
        row = jnp.where(bucket == b, (rb_ref[b, h] - far) * LOG2E, row)
    tile = pltpu.roll(jnp.broadcast_to(row, (2 * tq, tq)), 0, 1, stride=1, stride_axis=0)
    c = lax.broadcasted_iota(jnp.int32, (2 * tq, tq), 0)
    r = lax.broadcasted_iota(jnp.int32, (2 * tq, tq), 1)
    key_off = c - tq * variant
    bias = jnp.where(key_off - r > -_T5_LARGE_THRESHOLDS[-1], tile, 0.0)
    visible = (lax.shift_right_arithmetic(key_off, 6)
               <= lax.shift_right_arithmetic(r, 6))
    o_ref[0, 0] = jnp.where(visible, bias, NEG_INF)


def _t5_tiles(rel_bias, tq):
    nb, nh = rel_bias.shape
    return pl.pallas_call(
        functools.partial(_t5_tiles_kernel, tq=tq),
        grid=(nh, 2),
        in_specs=[pl.BlockSpec(memory_space=pltpu.SMEM)],
        out_specs=pl.BlockSpec((1, 1, 2 * tq, tq), lambda h, v: (h, v, 0, 0)),
        out_shape=jax.ShapeDtypeStruct((nh, 2, 2 * tq, tq), F32),
        compiler_params=_cparams("parallel", "arbitrary"),
        name="t5_tiles",
    )(rel_bias.astype(F32))


def _diff_attn_kernel(lamv_ref, q_ref, k_ref, v_ref, g_ref, bias_ref, qkg_ref,
                      sub_ref, o_ref, kn_ref, vt_ref, acc_ref, s_ref, *, tq, lam_init):
    i = pl.program_id(2)
    seq = k_ref.shape[1]
    lane = lax.broadcasted_iota(jnp.int32, (1, LANES), 1)
    lo_mask = lane < A_HEAD_DIM

    @pl.when(i == 0)
    def _():
        def body(c, carry):
            rows = pl.ds(pl.multiple_of(c * tq, tq), tq)
            kn_ref[rows, :] = _group_rmsnorm(
                k_ref[0, rows, :].astype(F32), qkg_ref[1:2, :], A_HEAD_DIM).astype(BF16)
            _store_vt_block(vt_ref, (c,), v_ref[0, rows, :])
            return carry
        lax.fori_loop(0, seq // tq, body, 0, unroll=2)

    lf = lamv_ref[...].astype(F32)
    lam = (jnp.exp(jnp.sum(lf[0:1] * lf[1:2], axis=-1, keepdims=True))
           - jnp.exp(jnp.sum(lf[2:3] * lf[3:4], axis=-1, keepdims=True)) + lam_init)

    qn = (_half_rmsnorm(q_ref[0].astype(F32), qkg_ref[0:1, :], lo_mask)
          * (A_HEAD_DIM ** -0.5 * LOG2E))
    q01 = jnp.concatenate([jnp.where(lo_mask, qn, 0.0).astype(BF16),
                           jnp.where(lo_mask, 0.0, qn).astype(BF16)], axis=0)

    def scores(blk):
        return _nt_dot(kn_ref[pl.ds(pl.multiple_of(blk * tq, tq), tq), :], q01)

    def near_add(j):
        bias_t = bias_ref[0, 0, j * tq:(j + 1) * tq, :]
        return jnp.concatenate([bias_t, bias_t], axis=1)

    _chunk_causal_flash(i, [(scores, lambda blk: vt_ref[blk], near_add)], acc_ref,
                        s_ref)

    o01 = _normalised(acc_ref[0])
    o_t = o01[:, 0:tq] - lam * o01[:, tq:]
    ms_t = jnp.mean(o_t * o_t, axis=0, keepdims=True)
    o_t = o_t * lax.rsqrt(ms_t + EPS) * sub_ref[...] * (1.0 - lam_init)
    o_ref[0] = (o_t.T * _silu(g_ref[0].astype(F32))).astype(o_ref.dtype)


def _diff_attention(proj, lam_vecs, qk_g, subln_g, bias_tiles, layer_idx):
    b, s, four_w = proj.shape
    width = four_w // 4
    hblocks = width // LANES
    tq = ATTN_TQ
    lam_init = 0.8 - 0.6 * math.exp(-0.3 * layer_idx)
    qkg = jnp.concatenate([qk_g, qk_g], axis=-1).astype(F32)
    return pl.pallas_call(
        functools.partial(_diff_attn_kernel, tq=tq, lam_init=lam_init),
        grid=(b, hblocks, s // tq),
        in_specs=[
            pl.BlockSpec((4, A_HEAD_DIM), lambda bi, h, i: (0, 0)),
            pl.BlockSpec((1, tq, LANES), lambda bi, h, i: (bi, i, h)),
            pl.BlockSpec((1, s, LANES), lambda bi, h, i: (bi, 0, hblocks + h)),
            pl.BlockSpec((1, s, LANES), lambda bi, h, i: (bi, 0, 2 * hblocks + h)),
            pl.BlockSpec((1, tq, LANES), lambda bi, h, i: (bi, i, 3 * hblocks + h)),
            pl.BlockSpec((1, 1, 2 * tq, tq),
                         lambda bi, h, i: (h, jnp.minimum(i, 1), 0, 0)),
            pl.BlockSpec((2, LANES), lambda bi, h, i: (0, 0)),
            pl.BlockSpec((LANES, 1), lambda bi, h, i: (0, 0)),
        ],
        out_specs=pl.BlockSpec((1, tq, LANES), lambda bi, h, i: (bi, i, h)),
        out_shape=jax.ShapeDtypeStruct((b, s, width), BF16),
        scratch_shapes=[pltpu.VMEM((s, LANES), BF16),
                        pltpu.VMEM((s // tq, V_ROWS, tq), BF16),
                        pltpu.VMEM((1, V_ROWS, 2 * tq), F32),
                        pltpu.VMEM((1, 2, tq, 2 * tq), F32)],
        compiler_params=_cparams("parallel", "parallel", "arbitrary"),
        name="diff_attention",
    )(lam_vecs.astype(F32), proj, proj, proj, proj, bias_tiles, qkg,
      subln_g.reshape(LANES, 1).astype(F32))


def _gla_kernel(q_ref, k_ref, v_ref, g_ref, lr_ref, wgt_ref, gb_ref, og_ref,
                cm_ref, o_ref, state_ref, *, ts, dk):
    @pl.when(pl.program_id(2) == 0)
    def _():
        state_ref[...] = jnp.zeros_like(state_ref)

    heads = q_ref.shape[2] // dk
    dv = v_ref.shape[2] // heads
    span = cm_ref.shape[0]
    la_t, rest_t, kd_t = [], [], []
    for hh in range(heads):
        hk = slice(hh * dk, (hh + 1) * dk)
        z = _nt_dot(wgt_ref[hk, :], lr_ref[0]) + gb_ref[hk, :]
        la = -_softplus(-z) * (1.0 / B_GATE_TAU)
        la_hi = la.astype(BF16)
        la_lo = (la - la_hi.astype(F32)).astype(BF16)
        rest = jnp.concatenate(
            [jnp.dot(la_hi[:, s0:s0 + span], cm_ref[...], preferred_element_type=F32)
             + jnp.dot(la_lo[:, s0:s0 + span], cm_ref[...], preferred_element_type=F32)
             for s0 in range(0, ts, span)], axis=1)
        la_t.append(la)
        rest_t.append(rest)
        kd_t.append((k_ref[0, :, hk].astype(F32).T * jnp.exp(rest)).astype(BF16))

    lane = lax.broadcasted_iota(jnp.int32, (1, LANES), 1)
    first_chunk = lane < CHUNK
    states = [state_ref[hh] for hh in range(heads)]
    for c in range(ts // CHUNK):
        group = slice((c // 2) * LANES, (c // 2 + 1) * LANES)
        keep = first_chunk if c % 2 == 0 else jnp.logical_not(first_chunk)
        t0 = slice(c * CHUNK, c * CHUNK + 1)
        rows = slice(c * CHUNK, (c + 1) * CHUNK)
        for hh in range(heads):
            hk = slice(hh * dk, (hh + 1) * dk)
            hv = slice(hh * dv, (hh + 1) * dv)
            kd_c = jnp.where(keep, kd_t[hh][:, group], jnp.zeros((), BF16))
            states[hh] = (jnp.exp(rest_t[hh][:, t0] + la_t[hh][:, t0]) * states[hh]
                          + jnp.dot(kd_c, v_ref[0, group, hv], preferred_element_type=F32))
            q = (q_ref[0, rows, hk].astype(F32) * (dk ** -0.5)).astype(BF16)
            o = jnp.dot(q, states[hh].astype(BF16), preferred_element_type=F32)
            ms = jnp.mean(o * o, axis=-1, keepdims=True)
            o = o * lax.rsqrt(ms + EPS) * og_ref[...]
            o_ref[0, rows, hv] = (o * _silu(g_ref[0, rows, hv].astype(F32))).astype(o_ref.dtype)
    for hh in range(heads):
        state_ref[hh] = states[hh]


GLA_HEADS_PER_STEP = 2


def _gla(proj, lr, w_gate_t, gate_bias, out_g, *, ts):
    b, s, n = proj.shape
    heads = B_HEADS
    dk = n // (6 * heads)
    dv = 2 * dk
    hp = GLA_HEADS_PER_STEP
    steps = heads // hp
    wk, wv = hp * dk, hp * dv
    k0 = (heads * dk) // wk
    v0 = (2 * heads * dk) // wv
    g0 = v0 + steps
    span = 2 * LANES
    t_in = jnp.arange(span)[:, None]
    t_out = jnp.arange(span)[None, :]
    chunk_masks = (((t_in // CHUNK) == (t_out // CHUNK)) & (t_in > t_out)).astype(BF16)
    return pl.pallas_call(
        functools.partial(_gla_kernel, ts=ts, dk=dk),
        grid=(b, steps, s // ts),
        in_specs=[
            pl.BlockSpec((1, ts, wk), lambda bi, h, i: (bi, i, h)),
            pl.BlockSpec((1, ts, wk), lambda bi, h, i: (bi, i, k0 + h)),
            pl.BlockSpec((1, ts, wv), lambda bi, h, i: (bi, i, v0 + h)),
            pl.BlockSpec((1, ts, wv), lambda bi, h, i: (bi, i, g0 + h)),
            pl.BlockSpec((1, ts, LANES), lambda bi, h, i: (bi, i, 0)),
            pl.BlockSpec((wk, LANES), lambda bi, h, i: (h, 0)),
            pl.BlockSpec((wk, 1), lambda bi, h, i: (h, 0)),
            pl.BlockSpec((1, dv), lambda bi, h, i: (0, 0)),
            pl.BlockSpec((span, span), lambda bi, h, i: (0, 0)),
        ],
        out_specs=pl.BlockSpec((1, ts, wv), lambda bi, h, i: (bi, i, h)),
        out_shape=jax.ShapeDtypeStruct((b, s, heads * dv), BF16),
        scratch_shapes=[pltpu.VMEM((hp, dk, dv), F32)],
        compiler_params=_cparams("parallel", "parallel", "arbitrary"),
        name="gla",
    )(proj, proj, proj, proj, lr, w_gate_t, gate_bias.reshape(-1, 1).astype(F32),
      out_g.reshape(1, dv).astype(F32), chunk_masks)


def _rglru_kernel(u_ref, g_ref, cw_ref, cb_ref, wr_ref, br_ref, wi_ref, bi_ref,
                  lam_ref, o_ref, ubuf_ref, xc_ref, a_ref, b_ref, h_ref, *, ts):
    width = u_ref.shape[2]
    bd = width // C_BLOCKS

    @pl.when(pl.program_id(1) == 0)
    def _():
        ubuf_ref[0:SUBLANES, :] = jnp.zeros((SUBLANES, width), F32)
        h_ref[...] = jnp.zeros_like(h_ref)

    ubuf_ref[SUBLANES:SUBLANES + ts, :] = u_ref[0]
    ue = ubuf_ref[...]
    xc = ue * cw_ref[0:1, :]
    for t in range(1, C_CONV):
        xc = pltpu.roll(xc, 1, 0) + ue * cw_ref[t:t + 1, :]
    xc_ref[...] = xc[SUBLANES:, :] + cb_ref[...]
    ubuf_ref[0:SUBLANES, :] = ubuf_ref[ts:ts + SUBLANES, :]

    rate = -C_C * _softplus(-lam_ref[...])
    for n in range(C_BLOCKS):
        cols = slice(n * bd, (n + 1) * bd)
        xb = xc_ref[:, cols]
        xb16 = xb.astype(BF16)
        r = _sigmoid(jnp.dot(xb16, wr_ref[n], preferred_element_type=F32)
                     + br_ref[:, cols])
        gate_i = _sigmoid(jnp.dot(xb16, wi_ref[n], preferred_element_type=F32)
                          + bi_ref[:, cols])
        log_a = r * rate[:, cols]
        a_ref[:, cols] = jnp.exp(log_a)
        b_ref[:, cols] = jnp.sqrt(_one_minus_exp2(log_a)) * (gate_i * xb)

    row = lax.broadcasted_iota(jnp.int32, (SUBLANES, width), 0)

    def tile(t, h_prev):
        rows = pl.ds(pl.multiple_of(t * SUBLANES, SUBLANES), SUBLANES)
        a = a_ref[rows, :]
        b = b_ref[rows, :]
        d = 1
        while d < SUBLANES:
            a_up = pltpu.roll(a, d, 0)
            b_up = pltpu.roll(b, d, 0)
            keep = row >= d
            b = jnp.where(keep, a * b_up + b, b)
            a = jnp.where(keep, a * a_up, a)
            d *= 2
        h = b + a * h_prev
        b_ref[rows, :] = h
        return h[SUBLANES - 1:SUBLANES, :]

    h_ref[...] = lax.fori_loop(0, ts // SUBLANES, tile, h_ref[...])
    o_ref[0] = (b_ref[...] * _silu(g_ref[0].astype(F32))).astype(o_ref.dtype)


def _rglru(u, gate, gate_block, conv_w, conv_b, w_rg, b_rg, w_ig, b_ig, lam, *, ts):
    b, s, width = u.shape
    bd = width // C_BLOCKS
    row = lambda a: a.reshape(1, width).astype(F32)
    const2 = lambda bi, i: (0, 0)
    return pl.pallas_call(
        functools.partial(_rglru_kernel, ts=ts),
        grid=(b, s // ts),
        in_specs=[
            pl.BlockSpec((1, ts, width), lambda bi, i: (bi, i, 0)),
            pl.BlockSpec((1, ts, width), lambda bi, i: (bi, i, gate_block)),
            pl.BlockSpec((C_CONV, width), const2),
            pl.BlockSpec((1, width), const2),
            pl.BlockSpec((C_BLOCKS, bd, bd), lambda bi, i: (0, 0, 0)),
            pl.BlockSpec((1, width), const2),
            pl.BlockSpec((C_BLOCKS, bd, bd), lambda bi, i: (0, 0, 0)),
            pl.BlockSpec((1, width), const2),
            pl.BlockSpec((1, width), const2),
        ],
        out_specs=pl.BlockSpec((1, ts, width), lambda bi, i: (bi, i, 0)),
        out_shape=jax.ShapeDtypeStruct((b, s, width), BF16),
        scratch_shapes=[pltpu.VMEM((ts + SUBLANES, width), F32),
                        pltpu.VMEM((ts, width), F32),
                        pltpu.VMEM((ts, width), F32),
                        pltpu.VMEM((ts, width), F32),
                        pltpu.VMEM((1, width), F32)],
        compiler_params=_cparams("parallel", "arbitrary"),
        name="rglru",
    )(u, gate, conv_w.astype(F32), row(conv_b), w_rg.astype(BF16), row(b_rg),
      w_ig.astype(BF16), row(b_ig), row(lam))


def _swap_rope_halves(x):
    lane = lax.broadcasted_iota(jnp.int32, (1, LANES), 1)
    first = (lane % D_ROPE) < (D_ROPE // 2)
    return jnp.where(first, pltpu.roll(x, LANES - D_ROPE // 2, 1),
                     pltpu.roll(x, D_ROPE // 2, 1))


def _rope(x, cos, sin_signed):
    return x * cos + _swap_rope_halves(x) * sin_signed


def _mla_attn_kernel(qn_ref, qp_ref, kn_ref, kp_ref, v_ref, g_ref, cos_ref,
                     sin_ref, gains_ref, o_ref, kcat_ref, vt_ref, acc_ref, s_ref,
                     *, tq):
    i = pl.program_id(2)
    seq = kn_ref.shape[1]
    lane = lax.broadcasted_iota(jnp.int32, (1, LANES), 1)
    lo_mask = lane < D_ROPE
    g_q_nope, g_q_pe = gains_ref[0:1, :], gains_ref[1:2, :]
    g_k_nope, g_k_pe = gains_ref[2:3, :], gains_ref[3:4, :]

    @pl.when(i == 0)
    def _():
        def body(c, carry):
            rows = pl.ds(pl.multiple_of(c * tq, tq), tq)
            kp = _rope(_group_rmsnorm(kp_ref[0, rows, :].astype(F32), g_k_pe, D_ROPE),
                       cos_ref[rows, :], sin_ref[rows, :])
            kn = kn_ref[0, rows, :].astype(F32)
            kcat_ref[0, rows, 0:LANES] = _group_rmsnorm(
                kn[:, 0:LANES], g_k_nope, D_NOPE).astype(BF16)
            kcat_ref[0, rows, LANES:] = jnp.where(lo_mask, kp, 0.0).astype(BF16)
            kcat_ref[1, rows, 0:LANES] = _group_rmsnorm(
                kn[:, LANES:], g_k_nope, D_NOPE).astype(BF16)
            kcat_ref[1, rows, LANES:] = jnp.where(lo_mask, 0.0, kp).astype(BF16)
            _store_vt_block(vt_ref, (0, c), v_ref[0, rows, 0:D_V])
            _store_vt_block(vt_ref, (1, c), v_ref[0, rows, D_V:])
            return carry
        lax.fori_loop(0, seq // tq, body, 0, unroll=2)

    scale = (D_NOPE + D_ROPE) ** -0.5 * LOG2E
    qrows = pl.ds(pl.multiple_of(i * tq, tq), tq)
    qp = _rope(_group_rmsnorm(qp_ref[0].astype(F32), g_q_pe, D_ROPE),
               cos_ref[qrows, :], sin_ref[qrows, :]) * scale
    qn = qn_ref[0].astype(F32)
    qcat = (
        jnp.concatenate([(_group_rmsnorm(qn[:, 0:LANES], g_q_nope, D_NOPE)
                          * scale).astype(BF16),
                         jnp.where(lo_mask, qp, 0.0).astype(BF16)], axis=-1),
        jnp.concatenate([(_group_rmsnorm(qn[:, LANES:], g_q_nope, D_NOPE)
                          * scale).astype(BF16),
                         jnp.where(lo_mask, 0.0, qp).astype(BF16)], axis=-1),
    )

    def near_add(j):
        c = lax.broadcasted_iota(jnp.int32, (tq, tq), 0) + tq * (j - jnp.minimum(i, 1))
        r = lax.broadcasted_iota(jnp.int32, (tq, tq), 1)
        visible = (lax.shift_right_arithmetic(c, 6) <= lax.shift_right_arithmetic(r, 6))
        return jnp.where(visible, 0.0, NEG_INF).astype(F32)

    def stream(hh):
        def scores(blk):
            rows = pl.ds(pl.multiple_of(blk * tq, tq), tq)
            return _nt_dot(kcat_ref[hh, rows, :], qcat[hh])
        return scores, lambda blk: vt_ref[hh, blk], near_add

    _chunk_causal_flash(i, [stream(0), stream(1)], acc_ref, s_ref)

    g = g_ref[0].astype(F32)
    o_ref[0, :, 0:D_V] = (_normalised(acc_ref[0]).T * _silu(g[:, 0:D_V])).astype(o_ref.dtype)
    o_ref[0, :, D_V:] = (_normalised(acc_ref[1]).T * _silu(g[:, D_V:])).astype(o_ref.dtype)


def _mla_attention(q, kv, kpe, proj, gate_block0, cos, sin_signed, gains):
    b, s, _ = q.shape
    heads = D_HEADS
    pairs = heads // 2
    tq = ATTN_TQ
    pw = 2 * LANES
    return pl.pallas_call(
        functools.partial(_mla_attn_kernel, tq=tq),
        grid=(b, pairs, s // tq),
        in_specs=[
            pl.BlockSpec((1, tq, pw), lambda bi, p, i: (bi, i, p)),
            pl.BlockSpec((1, tq, LANES), lambda bi, p, i: (bi, i, 2 * pairs + p)),
            pl.BlockSpec((1, s, pw), lambda bi, p, i: (bi, 0, p)),
            pl.BlockSpec((1, s, LANES), lambda bi, p, i: (bi, 0, 0)),
            pl.BlockSpec((1, s, pw), lambda bi, p, i: (bi, 0, pairs + p)),
            pl.BlockSpec((1, tq, pw), lambda bi, p, i: (bi, i, gate_block0 + p)),
            pl.BlockSpec((s, LANES), lambda bi, p, i: (0, 0)),
            pl.BlockSpec((s, LANES), lambda bi, p, i: (0, 0)),
            pl.BlockSpec((4, LANES), lambda bi, p, i: (0, 0)),
        ],
        out_specs=pl.BlockSpec((1, tq, pw), lambda bi, p, i: (bi, i, p)),
        out_shape=jax.ShapeDtypeStruct((b, s, heads * D_V), BF16),
        scratch_shapes=[pltpu.VMEM((2, s, pw), BF16),
                        pltpu.VMEM((2, s // tq, V_ROWS, tq), BF16),
                        pltpu.VMEM((2, V_ROWS, tq), F32),
                        pltpu.VMEM((2, 2, tq, tq), F32)],
        compiler_params=_cparams("parallel", "parallel", "arbitrary"),
        name="mla_attention",
    )(q, q, kv, kpe, kv, proj, cos, sin_signed, gains)


MM_TM = 1024
MM_TN = 1024
RES_TM = 512


def _layer_diff(x2, b, s, norm_g, w_in, qk_g, lam_vecs, subln_g, w_out, bias_tiles,
                layer_idx):
    proj = _norm_matmul(x2, norm_g, w_in, out_dtype=BF16, tm=MM_TM, tn=MM_TN)
    y = _diff_attention(proj.reshape(b, s, -1), lam_vecs, qk_g, subln_g,
                        bias_tiles, layer_idx)
    return _matmul_residual(y.reshape(b * s, -1), w_out.astype(BF16), x2,
                            tm=RES_TM)


def _layer_gla(x2, b, s, norm_g, w_in_all, layer, w_gate, gate_bias, out_g, w_out):
    w_in = w_in_all[layer]
    rank, hdk = w_gate.shape
    n_main = w_in.shape[1] - rank
    w_lr = jnp.pad(w_in[:, n_main:], ((0, 0), (0, LANES - rank))).astype(BF16)
    w_gate_t = jnp.pad(w_gate, ((0, LANES - rank), (0, 0))).T.astype(BF16)
    proj, lr = _norm_matmul(x2, norm_g, w_in_all, out_dtype=BF16, tm=MM_TM, tn=MM_TN,
                            n_cols=n_main, w_side=w_lr, layer=layer)
    y = _gla(proj.reshape(b, s, -1), lr.reshape(b, s, LANES), w_gate_t, gate_bias,
             out_g, ts=512)
    return _matmul_residual(y.reshape(b * s, -1), w_out.astype(BF16), x2,
                            tm=RES_TM)


def _layer_rglru(x2, b, s, norm_g, w_in, conv_w, conv_b, w_rg, b_rg, w_ig, b_ig,
                 lam, w_out):
    width = w_in.shape[1] // 2
    u, gate = _norm_matmul(x2, norm_g, w_in.astype(BF16), out_dtype=BF16,
                           tm=MM_TM, tn=MM_TN, first=(width, F32))
    y = _rglru(u.reshape(b, s, width), gate.reshape(b, s, width), 0, conv_w, conv_b,
               w_rg, b_rg, w_ig, b_ig, lam, ts=256)
    return _matmul_residual(y.reshape(b * s, -1), w_out.astype(BF16), x2,
                            tm=RES_TM)


def _rope_tables(s):
    half = D_ROPE // 2
    inv = ROPE_THETA ** (-jnp.arange(half, dtype=F32) / half)
    ang = jnp.arange(s, dtype=F32)[:, None] * inv[None, :]
    cos, sin = jnp.cos(ang), jnp.sin(ang)
    return (jnp.concatenate([cos, cos, cos, cos], axis=-1),
            jnp.concatenate([-sin, sin, -sin, sin], axis=-1))


def _layer_mla(x2, b, s, norm_g, w_in, q_lat_g, kv_lat_g, w_uq, w_ukv, qk_g, w_out):
    q_rank, kv_rank = q_lat_g.shape[0], kv_lat_g.shape[0]
    heads = D_HEADS
    lat = q_rank + kv_rank
    w_main = jnp.concatenate([w_in[:, :lat], w_in[:, lat + D_ROPE:]], axis=1)
    w_kpe = w_in[:, lat:lat + D_ROPE]
    w_kpe = jnp.concatenate([w_kpe, w_kpe], axis=1).astype(BF16)
    uq = w_uq.reshape(q_rank, heads, D_NOPE + D_ROPE)
    w_uq_p = jnp.concatenate([uq[:, :, :D_NOPE].reshape(q_rank, -1),
                              uq[:, :, D_NOPE:].reshape(q_rank, -1)], axis=1).astype(BF16)
    ukv = w_ukv.reshape(kv_rank, heads, D_NOPE + D_V)
    w_ukv_p = jnp.concatenate([ukv[:, :, :D_NOPE].reshape(kv_rank, -1),
                               ukv[:, :, D_NOPE:].reshape(kv_rank, -1)], axis=1).astype(BF16)
    dup = lambda v: jnp.concatenate([v, v])
    gains = jnp.stack([qk_g[0, :D_NOPE], dup(qk_g[0, D_NOPE:]),
                       qk_g[1, :D_NOPE], dup(qk_g[1, D_NOPE:])]).astype(F32)
    cos, sin_signed = _rope_tables(s)

    proj, kpe = _norm_matmul(x2, norm_g, w_main, out_dtype=BF16, tm=MM_TM, tn=MM_TN,
                             w_side=w_kpe)
    q = _norm_matmul(proj, q_lat_g, w_uq_p, out_dtype=BF16, tm=MM_TM,
                     tn=w_uq_p.shape[1], col_block=0)
    kv = _norm_matmul(proj, kv_lat_g, w_ukv_p, out_dtype=BF16, tm=MM_TM,
                      tn=w_ukv_p.shape[1], col_block=1)
    gate_block0 = lat // (2 * LANES)
    y = _mla_attention(q.reshape(b, s, -1), kv.reshape(b, s, -1),
                       kpe.reshape(b, s, LANES), proj.reshape(b, s, -1),
                       gate_block0, cos, sin_signed, gains)
    return _matmul_residual(y.reshape(b * s, -1), w_out.astype(BF16), x2,
                            tm=RES_TM)


def kernel(x, norm_g, rel_bias, a_w_in, a_qk_g, a_lambda, a_subln_g, a_w_out, b_w_in, b_w_gate, b_gate_bias, b_out_g, b_w_out, c_w_in, c_conv_w, c_conv_b, c_w_rgate, c_b_rgate, c_w_igate, c_b_igate, c_lambda, c_w_out, d_w_in, d_q_lat_g, d_kv_lat_g, d_w_uq, d_w_ukv, d_qk_g, d_w_out):
    b, s, d = x.shape
    depth = norm_g.shape[0]
    x2 = x.reshape(b * s, d)
    bias_tiles = _t5_tiles(rel_bias, ATTN_TQ)
    for i in range(depth):
        m, j = i % 4, i // 4
        if m == 0:
            x2 = _layer_diff(x2, b, s, norm_g[i], a_w_in[j], a_qk_g[j], a_lambda[j],
                             a_subln_g[j], a_w_out[j], bias_tiles, i)
        elif m == 1:
            x2 = _layer_gla(x2, b, s, norm_g[i], b_w_in, j, b_w_gate[j],
                            b_gate_bias[j], b_out_g[j], b_w_out[j])
        elif m == 2:
            x2 = _layer_rglru(x2, b, s, norm_g[i], c_w_in[j], c_conv_w[j], c_conv_b[j],
                              c_w_rgate[j], c_b_rgate[j], c_w_igate[j], c_b_igate[j],
                              c_lambda[j], c_w_out[j])
        else:
            x2 = _layer_mla(x2, b, s, norm_g[i], d_w_in[j], d_q_lat_g[j],
                            d_kv_lat_g[j], d_w_uq[j], d_w_ukv[j], d_qk_g[j], d_w_out[j])
    return x2.reshape(b, s, d)
```

```python
import functools
import math

import jax
import jax.numpy as jnp
from jax import lax
from jax.experimental import pallas as pl
from jax.experimental.pallas import tpu as pltpu

F32 = jnp.float32
BF16 = jnp.bfloat16

EPS = 1e-6
NEG_INF = -1e30
LOG2E = math.log2(math.e)
CHUNK = 64
LANES = 128
SUBLANES = 8
BF16_ROWS = 16
V7X_VMEM_BYTES = 64 * 1024 * 1024
VMEM_LIMIT = V7X_VMEM_BYTES - 8 * 1024 * 1024

A_HEADS = 16
A_HEAD_DIM = 64
T5_BUCKETS = 32
B_HEADS = 4
B_GATE_TAU = 16.0
C_BLOCKS = 8
C_CONV = 4
C_C = 8.0
D_HEADS = 16
D_NOPE = 128
D_ROPE = 64
D_V = 128
ROPE_THETA = 10000.0

ATTN_TQ = 512
V_ROWS = LANES + BF16_ROWS
FAR_UNROLL = 6


def _cparams(*semantics):
    return pltpu.CompilerParams(dimension_semantics=semantics,
                                vmem_limit_bytes=VMEM_LIMIT)


def _sigmoid(x):
    return 0.5 * jnp.tanh(0.5 * x) + 0.5


def _silu(g):
    h = 0.5 * g
    return h * jnp.tanh(h) + h


def _softplus(x):
    return jnp.maximum(x, 0.0) + jnp.log(1.0 + jnp.exp(-jnp.abs(x)))


def _one_minus_exp2(y):
    t = jnp.tanh(y)
    return (-2.0 * t) / (1.0 - t)


def _nt_dot(a, b):
    return lax.dot_general(a, b, (((1,), (1,)), ((), ())),
                           preferred_element_type=F32)


def _norm_matmul_kernel(*refs, has_side, n_first):
    refs = list(refs)
    x_ref, g_ref, w_ref = refs[:3]
    del refs[:3]
    ws_ref = refs.pop(0) if has_side else None
    o_ref = refs.pop(0)
    rest_ref = refs.pop(0) if n_first is not None else None
    side_ref = refs.pop(0) if has_side else None
    h_ref = refs.pop(0)
    j = pl.program_id(1)

    @pl.when(j == 0)
    def _():
        x = x_ref[...].astype(F32)
        ms = jnp.mean(x * x, axis=-1, keepdims=True)
        h_ref[...] = (x * lax.rsqrt(ms + EPS) * g_ref[...]).astype(BF16)
        if has_side:
            side_ref[...] = jnp.dot(h_ref[...], ws_ref[...],
                                    preferred_element_type=F32).astype(side_ref.dtype)

    w = w_ref[...]
    if w.dtype != BF16:
        w = w.astype(BF16)
    res = jnp.dot(h_ref[...], w, preferred_element_type=F32)
    if n_first is None:
        o_ref[...] = res.astype(o_ref.dtype)
    else:
        @pl.when(j < n_first)
        def _():
            o_ref[...] = res.astype(o_ref.dtype)

        @pl.when(j >= n_first)
        def _():
            rest_ref[...] = res.astype(rest_ref.dtype)


def _norm_matmul(x, gain, w, *, out_dtype, tm, tn, col_block=0, n_cols=None,
                 w_side=None, first=None):
    m = x.shape[0]
    k = w.shape[0]
    n = w.shape[1] if n_cols is None else n_cols
    n_first = None if first is None else first[0] // tn
    in_specs = [pl.BlockSpec((tm, k), lambda i, j: (i, col_block)),
                pl.BlockSpec((1, k), lambda i, j: (0, 0)),
                pl.BlockSpec((k, tn), lambda i, j: (0, j))]
    args = [x, gain.reshape(1, k).astype(F32), w]
    if w_side is not None:
        in_specs.append(pl.BlockSpec((k, LANES), lambda i, j: (0, 0)))
        args.append(w_side)
    if first is None:
        out_specs = [pl.BlockSpec((tm, tn), lambda i, j: (i, j))]
        out_shape = [jax.ShapeDtypeStruct((m, n), out_dtype)]
    else:
        out_specs = [pl.BlockSpec((tm, tn), lambda i, j: (i, jnp.minimum(j, n_first - 1))),
                     pl.BlockSpec((tm, tn), lambda i, j: (i, jnp.maximum(j - n_first, 0)))]
        out_shape = [jax.ShapeDtypeStruct((m, first[0]), first[1]),
                     jax.ShapeDtypeStruct((m, n - first[0]), out_dtype)]
    if w_side is not None:
        out_specs.append(pl.BlockSpec((tm, LANES), lambda i, j: (i, 0)))
        out_shape.append(jax.ShapeDtypeStruct((m, LANES), BF16))
    outs = pl.pallas_call(
        functools.partial(_norm_matmul_kernel, has_side=w_side is not None,
                          n_first=n_first),
        grid=(m // tm, n // tn),
        in_specs=in_specs,
        out_specs=out_specs,
        out_shape=out_shape,
        scratch_shapes=[pltpu.VMEM((tm, k), BF16)],
        compiler_params=_cparams("parallel", "arbitrary"),
        name="norm_matmul",
    )(*args)
    return outs[0] if len(outs) == 1 else tuple(outs)


def _matmul_residual_kernel(y_ref, w_ref, x_ref, o_ref):
    o_ref[...] = x_ref[...] + jnp.dot(y_ref[...], w_ref[...],
                                      preferred_element_type=F32)


def _matmul_residual(y, w, x, *, tm):
    m, k = y.shape
    n = w.shape[1]
    return pl.pallas_call(
        _matmul_residual_kernel,
        grid=(m // tm,),
        in_specs=[pl.BlockSpec((tm, k), lambda i: (i, 0)),
                  pl.BlockSpec((k, n), lambda i: (0, 0)),
                  pl.BlockSpec((tm, n), lambda i: (i, 0))],
        out_specs=pl.BlockSpec((tm, n), lambda i: (i, 0)),
        out_shape=jax.ShapeDtypeStruct((m, n), F32),
        compiler_params=_cparams("parallel"),
        name="matmul_residual",
    )(y, w, x)


def _group_rmsnorm(x, gain, group):
    r = lax.broadcasted_iota(jnp.int32, (LANES, LANES), 0)
    c = lax.broadcasted_iota(jnp.int32, (LANES, LANES), 1)
    ones = jnp.where(r // group == c // group, 1.0, 0.0).astype(BF16)
    x2 = x * x
    hi = x2.astype(BF16)
    lo = (x2 - hi.astype(F32)).astype(BF16)
    ss = (jnp.dot(hi, ones, preferred_element_type=F32)
          + jnp.dot(lo, ones, preferred_element_type=F32))
    return x * lax.rsqrt(ss * (1.0 / group) + EPS) * gain


def _half_rmsnorm(x, gain, lo_mask):
    x2 = x * x
    lo = jnp.sum(jnp.where(lo_mask, x2, 0.0), axis=-1, keepdims=True)
    hi = jnp.sum(jnp.where(lo_mask, 0.0, x2), axis=-1, keepdims=True)
    ms = jnp.where(lo_mask, lo, hi) * (1.0 / 64)
    return x * lax.rsqrt(ms + EPS) * gain


def _store_vt_block(vt_ref, lead, v_rows):
    tq = v_rows.shape[0]
    r = lax.broadcasted_iota(jnp.int32, (LANES, LANES), 0)
    c = lax.broadcasted_iota(jnp.int32, (LANES, LANES), 1)
    eye = jnp.where(r == c, 1.0, 0.0).astype(BF16)
    vt_ref[(*lead, slice(0, LANES), slice(None))] = _nt_dot(eye, v_rows).astype(BF16)
    vt_ref[(*lead, slice(LANES, V_ROWS), slice(None))] = jnp.ones((BF16_ROWS, tq), BF16)


def _flash_init(s, vt):
    m = jnp.max(s, axis=0, keepdims=True)
    p = jnp.exp2(s - m).astype(BF16)
    return m, jnp.dot(vt, p, preferred_element_type=F32)


def _flash_update(s, vt, m, acc):
    m_new = jnp.maximum(m, jnp.max(s, axis=0, keepdims=True))
    alpha = jnp.exp2(m - m_new)
    p = jnp.exp2(s - m_new).astype(BF16)
    return m_new, alpha * acc + jnp.dot(vt, p, preferred_element_type=F32)


def _chunk_causal_flash(i, streams, acc_ref, s_ref):
    first = jnp.maximum(i - 1, 0)
    n_far = first

    def issue(blk, slot, add=None):
        for si, (scores, _, near_add) in enumerate(streams):
            s = scores(blk)
            s_ref[si, slot] = s if add is None else s + near_add(add)

    def update(blk, slot, ms):
        out = []
        for si, (_, values, _) in enumerate(streams):
            m, acc_ref[si] = _flash_update(s_ref[si, slot], values(blk), ms[si],
                                           acc_ref[si])
            out.append(m)
        return tuple(out)

    def group(t0, count, prefetch_next, ms):
        for k in range(count):
            if k + 1 < count or prefetch_next:
                issue(t0 + k + 1, (k + 1) % 2)
            ms = update(t0 + k, k % 2, ms)
        return ms

    issue(first, 0, add=0)
    issue(first + 1, 1, add=1)
    ms = []
    for si, (_, values, _) in enumerate(streams):
        m, acc_ref[si] = _flash_init(s_ref[si, 0], values(first))
        ms.append(m)
    issue(0, 0)
    ms = update(first + 1, 1, tuple(ms))

    full = jnp.maximum(n_far - 1, 0) // FAR_UNROLL
    rest = n_far - FAR_UNROLL * full
    ms = lax.fori_loop(0, full,
                       lambda g, ms: group(g * FAR_UNROLL, FAR_UNROLL, True, ms), ms)
    for count in range(1, FAR_UNROLL + 1):
        ms = lax.fori_loop(0, (rest == count).astype(jnp.int32),
                           lambda _, ms, count=count: group(n_far - count, count, False, ms),
                           ms)


def _normalised(acc):
    return acc[0:LANES] * (1.0 / acc[LANES:LANES + 1])


_T5_LARGE_THRESHOLDS = (12, 16, 23, 32, 46, 64, 91)


def _t5_tiles_kernel(rb_ref, o_ref, *, tq):
    h = pl.program_id(0)
    variant = pl.program_id(1)
    j = lax.broadcasted_iota(jnp.int32, (1, tq), 1)
    rel_row = jnp.where(j < tq // 2, -j, tq - j)
    n = jnp.abs(rel_row)
    large = jnp.full_like(n, 8)
    for t in _T5_LARGE_THRESHOLDS:
        large = large + (n >= t).astype(jnp.int32)
    bucket = jnp.where(rel_row > 0, T5_BUCKETS // 2, 0) + jnp.where(n < 8, n, large)
    far = rb_ref[T5_BUCKETS // 2 - 1, h]
    row = jnp.zeros((1, tq), F32)
    for b in range(T5_BUCKETS):
        row = jnp.where(bucket == b, (rb_ref[b, h] - far) * LOG2E, row)
    tile = pltpu.roll(jnp.broadcast_to(row, (2 * tq, tq)), 0, 1, stride=1, stride_axis=0)
    c = lax.broadcasted_iota(jnp.int32, (2 * tq, tq), 0)
    r = lax.broadcasted_iota(jnp.int32, (2 * tq, tq), 1)
    key_off = c - tq * variant
    bias = jnp.where(key_off - r > -_T5_LARGE_THRESHOLDS[-1], tile, 0.0)
    visible = (lax.shift_right_arithmetic(key_off, 6)
               <= lax.shift_right_arithmetic(r, 6))
    o_ref[0, 0] = jnp.where(visible, bias, NEG_INF)


def _t5_tiles(rel_bias, tq):
    nb, nh = rel_bias.shape
    return pl.pallas_call(
        functools.partial(_t5_tiles_kernel, tq=tq),
        grid=(nh, 2),
        in_specs=[pl.BlockSpec(memory_space=pltpu.SMEM)],
        out_specs=pl.BlockSpec((1, 1, 2 * tq, tq), lambda h, v: (h, v, 0, 0)),
        out_shape=jax.ShapeDtypeStruct((nh, 2, 2 * tq, tq), F32),
        compiler_params=_cparams("parallel", "arbitrary"),
        name="t5_tiles",
    )(rel_bias.astype(F32))


def _diff_attn_kernel(lamv_ref, q_ref, k_ref, v_ref, g_ref, bias_ref, qkg_ref,
                      sub_ref, o_ref, kn_ref, vt_ref, acc_ref, s_ref, *, tq, lam_init):
    i = pl.program_id(2)
    seq = k_ref.shape[1]
    lane = lax.broadcasted_iota(jnp.int32, (1, LANES), 1)
    lo_mask = lane < A_HEAD_DIM

    @pl.when(i == 0)
    def _():
        def body(c, carry):
            rows = pl.ds(pl.multiple_of(c * tq, tq), tq)
            kn_ref[rows, :] = _group_rmsnorm(
                k_ref[0, rows, :].astype(F32), qkg_ref[1:2, :], A_HEAD_DIM).astype(BF16)
            _store_vt_block(vt_ref, (c,), v_ref[0, rows, :])
            return carry
        lax.fori_loop(0, seq // tq, body, 0, unroll=2)

    lf = lamv_ref[...].astype(F32)
    lam = (jnp.exp(jnp.sum(lf[0:1] * lf[1:2], axis=-1, keepdims=True))
           - jnp.exp(jnp.sum(lf[2:3] * lf[3:4], axis=-1, keepdims=True)) + lam_init)

    qn = (_half_rmsnorm(q_ref[0].astype(F32), qkg_ref[0:1, :], lo_mask)
          * (A_HEAD_DIM ** -0.5 * LOG2E))
    q01 = jnp.concatenate([jnp.where(lo_mask, qn, 0.0).astype(BF16),
                           jnp.where(lo_mask, 0.0, qn).astype(BF16)], axis=0)

    def scores(blk):
        return _nt_dot(kn_ref[pl.ds(pl.multiple_of(blk * tq, tq), tq), :], q01)

    def near_add(j):
        bias_t = bias_ref[0, 0, j * tq:(j + 1) * tq, :]
        return jnp.concatenate([bias_t, bias_t], axis=1)

    _chunk_causal_flash(i, [(scores, lambda blk: vt_ref[blk], near_add)], acc_ref,
                        s_ref)

    o01 = _normalised(acc_ref[0])
    o_t = o01[:, 0:tq] - lam * o01[:, tq:]
    ms_t = jnp.mean(o_t * o_t, axis=0, keepdims=True)
    o_t = o_t * lax.rsqrt(ms_t + EPS) * sub_ref[...] * (1.0 - lam_init)
    o_ref[0] = (o_t.T * _silu(g_ref[0].astype(F32))).astype(o_ref.dtype)


def _diff_attention(proj, lam_vecs, qk_g, subln_g, bias_tiles, layer_idx):
    b, s, four_w = proj.shape
    width = four_w // 4
    hblocks = width // LANES
    tq = ATTN_TQ
    lam_init = 0.8 - 0.6 * math.exp(-0.3 * layer_idx)
    qkg = jnp.concatenate([qk_g, qk_g], axis=-1).astype(F32)
    return pl.pallas_call(
        functools.partial(_diff_attn_kernel, tq=tq, lam_init=lam_init),
        grid=(b, hblocks, s // tq),
        in_specs=[
            pl.BlockSpec((4, A_HEAD_DIM), lambda bi, h, i: (0, 0)),
            pl.BlockSpec((1, tq, LANES), lambda bi, h, i: (bi, i, h)),
            pl.BlockSpec((1, s, LANES), lambda bi, h, i: (bi, 0, hblocks + h)),
            pl.BlockSpec((1, s, LANES), lambda bi, h, i: (bi, 0, 2 * hblocks + h)),
            pl.BlockSpec((1, tq, LANES), lambda bi, h, i: (bi, i, 3 * hblocks + h)),
            pl.BlockSpec((1, 1, 2 * tq, tq),
                         lambda bi, h, i: (h, jnp.minimum(i, 1), 0, 0)),
            pl.BlockSpec((2, LANES), lambda bi, h, i: (0, 0)),
            pl.BlockSpec((LANES, 1), lambda bi, h, i: (0, 0)),
        ],
        out_specs=pl.BlockSpec((1, tq, LANES), lambda bi, h, i: (bi, i, h)),
        out_shape=jax.ShapeDtypeStruct((b, s, width), BF16),
        scratch_shapes=[pltpu.VMEM((s, LANES), BF16),
                        pltpu.VMEM((s // tq, V_ROWS, tq), BF16),
                        pltpu.VMEM((1, V_ROWS, 2 * tq), F32),
                        pltpu.VMEM((1, 2, tq, 2 * tq), F32)],
        compiler_params=_cparams("parallel", "parallel", "arbitrary"),
        name="diff_attention",
    )(lam_vecs.astype(F32), proj, proj, proj, proj, bias_tiles, qkg,
      subln_g.reshape(LANES, 1).astype(F32))


def _gla_kernel(q_ref, k_ref, v_ref, g_ref, lr_ref, wgt_ref, gb_ref, og_ref,
                cm_ref, o_ref, state_ref, *, ts, dk):
    @pl.when(pl.program_id(2) == 0)
    def _():
        state_ref[...] = jnp.zeros_like(state_ref)

    heads = q_ref.shape[2] // dk
    dv = v_ref.shape[2] // heads
    span = cm_ref.shape[0]
    la_t, rest_t, kd_t = [], [], []
    for hh in range(heads):
        hk = slice(hh * dk, (hh + 1) * dk)
        z = _nt_dot(wgt_ref[hk, :], lr_ref[0]) + gb_ref[hk, :]
        la = -_softplus(-z) * (1.0 / B_GATE_TAU)
        la_hi = la.astype(BF16)
        la_lo = (la - la_hi.astype(F32)).astype(BF16)
        rest = jnp.concatenate(
            [jnp.dot(la_hi[:, s0:s0 + span], cm_ref[...], preferred_element_type=F32)
             + jnp.dot(la_lo[:, s0:s0 + span], cm_ref[...], preferred_element_type=F32)
             for s0 in range(0, ts, span)], axis=1)
        la_t.append(la)
        rest_t.append(rest)
        kd_t.append((k_ref[0, :, hk].astype(F32).T * jnp.exp(rest)).astype(BF16))

    lane = lax.broadcasted_iota(jnp.int32, (1, LANES), 1)
    first_chunk = lane < CHUNK
    states = [state_ref[hh] for hh in range(heads)]
    for c in range(ts // CHUNK):
        group = slice((c // 2) * LANES, (c // 2 + 1) * LANES)
        keep = first_chunk if c % 2 == 0 else jnp.logical_not(first_chunk)
        t0 = slice(c * CHUNK, c * CHUNK + 1)
        rows = slice(c * CHUNK, (c + 1) * CHUNK)
        for hh in range(heads):
            hk = slice(hh * dk, (hh + 1) * dk)
            hv = slice(hh * dv, (hh + 1) * dv)
            kd_c = jnp.where(keep, kd_t[hh][:, group], jnp.zeros((), BF16))
            states[hh] = (jnp.exp(rest_t[hh][:, t0] + la_t[hh][:, t0]) * states[hh]
                          + jnp.dot(kd_c, v_ref[0, group, hv], preferred_element_type=F32))
            q = (q_ref[0, rows, hk].astype(F32) * (dk ** -0.5)).astype(BF16)
            o = jnp.dot(q, states[hh].astype(BF16), preferred_element_type=F32)
            ms = jnp.mean(o * o, axis=-1, keepdims=True)
            o = o * lax.rsqrt(ms + EPS) * og_ref[...]
            o_ref[0, rows, hv] = (o * _silu(g_ref[0, rows, hv].astype(F32))).astype(o_ref.dtype)
    for hh in range(heads):
        state_ref[hh] = states[hh]


GLA_HEADS_PER_STEP = 4


def _gla(proj, lr, w_gate_t, gate_bias, out_g, *, ts):
    b, s, n = proj.shape
    heads = B_HEADS
    dk = n // (6 * heads)
    dv = 2 * dk
    hp = GLA_HEADS_PER_STEP
    steps = heads // hp
    wk, wv = hp * dk, hp * dv
    k0 = (heads * dk) // wk
    v0 = (2 * heads * dk) // wv
    g0 = v0 + steps
    span = 2 * LANES
    t_in = jnp.arange(span)[:, None]
    t_out = jnp.arange(span)[None, :]
    chunk_masks = (((t_in // CHUNK) == (t_out // CHUNK)) & (t_in > t_out)).astype(BF16)
    return pl.pallas_call(
        functools.partial(_gla_kernel, ts=ts, dk=dk),
        grid=(b, steps, s // ts),
        in_specs=[
            pl.BlockSpec((1, ts, wk), lambda bi, h, i: (bi, i, h)),
            pl.BlockSpec((1, ts, wk), lambda bi, h, i: (bi, i, k0 + h)),
            pl.BlockSpec((1, ts, wv), lambda bi, h, i: (bi, i, v0 + h)),
            pl.BlockSpec((1, ts, wv), lambda bi, h, i: (bi, i, g0 + h)),
            pl.BlockSpec((1, ts, LANES), lambda bi, h, i: (bi, i, 0)),
            pl.BlockSpec((wk, LANES), lambda bi, h, i: (h, 0)),
            pl.BlockSpec((wk, 1), lambda bi, h, i: (h, 0)),
            pl.BlockSpec((1, dv), lambda bi, h, i: (0, 0)),
            pl.BlockSpec((span, span), lambda bi, h, i: (0, 0)),
        ],
        out_specs=pl.BlockSpec((1, ts, wv), lambda bi, h, i: (bi, i, h)),
        out_shape=jax.ShapeDtypeStruct((b, s, heads * dv), BF16),
        scratch_shapes=[pltpu.VMEM((hp, dk, dv), F32)],
        compiler_params=_cparams("parallel", "parallel", "arbitrary"),
        name="gla",
    )(proj, proj, proj, proj, lr, w_gate_t, gate_bias.reshape(-1, 1).astype(F32),
      out_g.reshape(1, dv).astype(F32), chunk_masks)


def _rglru_kernel(u_ref, g_ref, cw_ref, cb_ref, wr_ref, br_ref, wi_ref, bi_ref,
                  lam_ref, o_ref, ubuf_ref, xc_ref, a_ref, b_ref, h_ref, *, ts):
    width = u_ref.shape[2]
    bd = width // C_BLOCKS

    @pl.when(pl.program_id(1) == 0)
    def _():
        ubuf_ref[0:SUBLANES, :] = jnp.zeros((SUBLANES, width), F32)
        h_ref[...] = jnp.zeros_like(h_ref)

    ubuf_ref[SUBLANES:SUBLANES + ts, :] = u_ref[0]
    ue = ubuf_ref[...]
    xc = ue * cw_ref[0:1, :]
    for t in range(1, C_CONV):
        xc = pltpu.roll(xc, 1, 0) + ue * cw_ref[t:t + 1, :]
    xc_ref[...] = xc[SUBLANES:, :] + cb_ref[...]
    ubuf_ref[0:SUBLANES, :] = ubuf_ref[ts:ts + SUBLANES, :]

    rate = -C_C * _softplus(-lam_ref[...])
    for n in range(C_BLOCKS):
        cols = slice(n * bd, (n + 1) * bd)
        xb = xc_ref[:, cols]
        xb16 = xb.astype(BF16)
        r = _sigmoid(jnp.dot(xb16, wr_ref[n], preferred_element_type=F32)
                     + br_ref[:, cols])
        gate_i = _sigmoid(jnp.dot(xb16, wi_ref[n], preferred_element_type=F32)
                          + bi_ref[:, cols])
        log_a = r * rate[:, cols]
        a_ref[:, cols] = jnp.exp(log_a)
        b_ref[:, cols] = jnp.sqrt(_one_minus_exp2(log_a)) * (gate_i * xb)

    row = lax.broadcasted_iota(jnp.int32, (SUBLANES, width), 0)

    def tile(t, h_prev):
        rows = pl.ds(pl.multiple_of(t * SUBLANES, SUBLANES), SUBLANES)
        a = a_ref[rows, :]
        b = b_ref[rows, :]
        d = 1
        while d < SUBLANES:
            a_up = pltpu.roll(a, d, 0)
            b_up = pltpu.roll(b, d, 0)
            keep = row >= d
            b = jnp.where(keep, a * b_up + b, b)
            a = jnp.where(keep, a * a_up, a)
            d *= 2
        h = b + a * h_prev
        b_ref[rows, :] = h
        return h[SUBLANES - 1:SUBLANES, :]

    h_ref[...] = lax.fori_loop(0, ts // SUBLANES, tile, h_ref[...])
    o_ref[0] = (b_ref[...] * _silu(g_ref[0].astype(F32))).astype(o_ref.dtype)


def _rglru(u, gate, gate_block, conv_w, conv_b, w_rg, b_rg, w_ig, b_ig, lam, *, ts):
    b, s, width = u.shape
    bd = width // C_BLOCKS
    row = lambda a: a.reshape(1, width).astype(F32)
    const2 = lambda bi, i: (0, 0)
    return pl.pallas_call(
        functools.partial(_rglru_kernel, ts=ts),
        grid=(b, s // ts),
        in_specs=[
            pl.BlockSpec((1, ts, width), lambda bi, i: (bi, i, 0)),
            pl.BlockSpec((1, ts, width), lambda bi, i: (bi, i, gate_block)),
            pl.BlockSpec((C_CONV, width), const2),
            pl.BlockSpec((1, width), const2),
            pl.BlockSpec((C_BLOCKS, bd, bd), lambda bi, i: (0, 0, 0)),
            pl.BlockSpec((1, width), const2),
            pl.BlockSpec((C_BLOCKS, bd, bd), lambda bi, i: (0, 0, 0)),
            pl.BlockSpec((1, width), const2),
            pl.BlockSpec((1, width), const2),
        ],
        out_specs=pl.BlockSpec((1, ts, width), lambda bi, i: (bi, i, 0)),
        out_shape=jax.ShapeDtypeStruct((b, s, width), BF16),
        scratch_shapes=[pltpu.VMEM((ts + SUBLANES, width), F32),
                        pltpu.VMEM((ts, width), F32),
                        pltpu.VMEM((ts, width), F32),
                        pltpu.VMEM((ts, width), F32),
                        pltpu.VMEM((1, width), F32)],
        compiler_params=_cparams("parallel", "arbitrary"),
        name="rglru",
    )(u, gate, conv_w.astype(F32), row(conv_b), w_rg.astype(BF16), row(b_rg),
      w_ig.astype(BF16), row(b_ig), row(lam))


def _swap_rope_halves(x):
    lane = lax.broadcasted_iota(jnp.int32, (1, LANES), 1)
    first = (lane % D_ROPE) < (D_ROPE // 2)
    return jnp.where(first, pltpu.roll(x, LANES - D_ROPE // 2, 1),
                     pltpu.roll(x, D_ROPE // 2, 1))


def _rope(x, cos, sin_signed):
    return x * cos + _swap_rope_halves(x) * sin_signed


def _mla_attn_kernel(qn_ref, qp_ref, kn_ref, kp_ref, v_ref, g_ref, cos_ref,
                     sin_ref, gains_ref, o_ref, kcat_ref, vt_ref, acc_ref, s_ref,
                     *, tq):
    i = pl.program_id(2)
    seq = kn_ref.shape[1]
    lane = lax.broadcasted_iota(jnp.int32, (1, LANES), 1)
    lo_mask = lane < D_ROPE
    g_q_nope, g_q_pe = gains_ref[0:1, :], gains_ref[1:2, :]
    g_k_nope, g_k_pe = gains_ref[2:3, :], gains_ref[3:4, :]

    @pl.when(i == 0)
    def _():
        def body(c, carry):
            rows = pl.ds(pl.multiple_of(c * tq, tq), tq)
            kp = _rope(_group_rmsnorm(kp_ref[0, rows, :].astype(F32), g_k_pe, D_ROPE),
                       cos_ref[rows, :], sin_ref[rows, :])
            kn = kn_ref[0, rows, :].astype(F32)
            kcat_ref[0, rows, 0:LANES] = _group_rmsnorm(
                kn[:, 0:LANES], g_k_nope, D_NOPE).astype(BF16)
            kcat_ref[0, rows, LANES:] = jnp.where(lo_mask, kp, 0.0).astype(BF16)
            kcat_ref[1, rows, 0:LANES] = _group_rmsnorm(
                kn[:, LANES:], g_k_nope, D_NOPE).astype(BF16)
            kcat_ref[1, rows, LANES:] = jnp.where(lo_mask, 0.0, kp).astype(BF16)
            _store_vt_block(vt_ref, (0, c), v_ref[0, rows, 0:D_V])
            _store_vt_block(vt_ref, (1, c), v_ref[0, rows, D_V:])
            return carry
        lax.fori_loop(0, seq // tq, body, 0, unroll=2)

    scale = (D_NOPE + D_ROPE) ** -0.5 * LOG2E
    qrows = pl.ds(pl.multiple_of(i * tq, tq), tq)
    qp = _rope(_group_rmsnorm(qp_ref[0].astype(F32), g_q_pe, D_ROPE),
               cos_ref[qrows, :], sin_ref[qrows, :]) * scale
    qn = qn_ref[0].astype(F32)
    qcat = (
        jnp.concatenate([(_group_rmsnorm(qn[:, 0:LANES], g_q_nope, D_NOPE)
                          * scale).astype(BF16),
                         jnp.where(lo_mask, qp, 0.0).astype(BF16)], axis=-1),
        jnp.concatenate([(_group_rmsnorm(qn[:, LANES:], g_q_nope, D_NOPE)
                          * scale).astype(BF16),
                         jnp.where(lo_mask, 0.0, qp).astype(BF16)], axis=-1),
    )

    def near_add(j):
        c = lax.broadcasted_iota(jnp.int32, (tq, tq), 0) + tq * (j - jnp.minimum(i, 1))
        r = lax.broadcasted_iota(jnp.int32, (tq, tq), 1)
        visible = (lax.shift_right_arithmetic(c, 6) <= lax.shift_right_arithmetic(r, 6))
        return jnp.where(visible, 0.0, NEG_INF).astype(F32)

    def stream(hh):
        def scores(blk):
            rows = pl.ds(pl.multiple_of(blk * tq, tq), tq)
            return _nt_dot(kcat_ref[hh, rows, :], qcat[hh])
        return scores, lambda blk: vt_ref[hh, blk], near_add

    _chunk_causal_flash(i, [stream(0), stream(1)], acc_ref, s_ref)

    g = g_ref[0].astype(F32)
    o_ref[0, :, 0:D_V] = (_normalised(acc_ref[0]).T * _silu(g[:, 0:D_V])).astype(o_ref.dtype)
    o_ref[0, :, D_V:] = (_normalised(acc_ref[1]).T * _silu(g[:, D_V:])).astype(o_ref.dtype)


def _mla_attention(q, kv, kpe, proj, gate_block0, cos, sin_signed, gains):
    b, s, _ = q.shape
    heads = D_HEADS
    pairs = heads // 2
    tq = ATTN_TQ
    pw = 2 * LANES
    return pl.pallas_call(
        functools.partial(_mla_attn_kernel, tq=tq),
        grid=(b, pairs, s // tq),
        in_specs=[
            pl.BlockSpec((1, tq, pw), lambda bi, p, i: (bi, i, p)),
            pl.BlockSpec((1, tq, LANES), lambda bi, p, i: (bi, i, 2 * pairs + p)),
            pl.BlockSpec((1, s, pw), lambda bi, p, i: (bi, 0, p)),
            pl.BlockSpec((1, s, LANES), lambda bi, p, i: (bi, 0, 0)),
            pl.BlockSpec((1, s, pw), lambda bi, p, i: (bi, 0, pairs + p)),
            pl.BlockSpec((1, tq, pw), lambda bi, p, i: (bi, i, gate_block0 + p)),
            pl.BlockSpec((s, LANES), lambda bi, p, i: (0, 0)),
            pl.BlockSpec((s, LANES), lambda bi, p, i: (0, 0)),
            pl.BlockSpec((4, LANES), lambda bi, p, i: (0, 0)),
        ],
        out_specs=pl.BlockSpec((1, tq, pw), lambda bi, p, i: (bi, i, p)),
        out_shape=jax.ShapeDtypeStruct((b, s, heads * D_V), BF16),
        scratch_shapes=[pltpu.VMEM((2, s, pw), BF16),
                        pltpu.VMEM((2, s // tq, V_ROWS, tq), BF16),
                        pltpu.VMEM((2, V_ROWS, tq), F32),
                        pltpu.VMEM((2, 2, tq, tq), F32)],
        compiler_params=_cparams("parallel", "parallel", "arbitrary"),
        name="mla_attention",
    )(q, q, kv, kpe, kv, proj, cos, sin_signed, gains)


MM_TM = 1024
MM_TN = 1024
RES_TM = 512


def _layer_diff(x2, b, s, norm_g, w_in, qk_g, lam_vecs, subln_g, w_out, bias_tiles,
                layer_idx):
    proj = _norm_matmul(x2, norm_g, w_in, out_dtype=BF16, tm=MM_TM, tn=MM_TN)
    y = _diff_attention(proj.reshape(b, s, -1), lam_vecs, qk_g, subln_g,
                        bias_tiles, layer_idx)
    return _matmul_residual(y.reshape(b * s, -1), w_out.astype(BF16), x2,
                            tm=RES_TM)


def _layer_gla(x2, b, s, norm_g, w_in, w_gate, gate_bias, out_g, w_out):
    rank, hdk = w_gate.shape
    n_main = w_in.shape[1] - rank
    w_lr = jnp.pad(w_in[:, n_main:], ((0, 0), (0, LANES - rank))).astype(BF16)
    w_gate_t = jnp.pad(w_gate, ((0, LANES - rank), (0, 0))).T.astype(BF16)
    proj, lr = _norm_matmul(x2, norm_g, w_in, out_dtype=BF16, tm=MM_TM, tn=MM_TN,
                            n_cols=n_main, w_side=w_lr)
    y = _gla(proj.reshape(b, s, -1), lr.reshape(b, s, LANES), w_gate_t, gate_bias,
             out_g, ts=512)
    return _matmul_residual(y.reshape(b * s, -1), w_out.astype(BF16), x2,
                            tm=RES_TM)


def _layer_rglru(x2, b, s, norm_g, w_in, conv_w, conv_b, w_rg, b_rg, w_ig, b_ig,
                 lam, w_out):
    width = w_in.shape[1] // 2
    u, gate = _norm_matmul(x2, norm_g, w_in.astype(BF16), out_dtype=BF16,
                           tm=MM_TM, tn=MM_TN, first=(width, F32))
    y = _rglru(u.reshape(b, s, width), gate.reshape(b, s, width), 0, conv_w, conv_b,
               w_rg, b_rg, w_ig, b_ig, lam, ts=256)
    return _matmul_residual(y.reshape(b * s, -1), w_out.astype(BF16), x2,
                            tm=RES_TM)


def _rope_tables(s):
    half = D_ROPE // 2
    inv = ROPE_THETA ** (-jnp.arange(half, dtype=F32) / half)
    ang = jnp.arange(s, dtype=F32)[:, None] * inv[None, :]
    cos, sin = jnp.cos(ang), jnp.sin(ang)
    return (jnp.concatenate([cos, cos, cos, cos], axis=-1),
            jnp.concatenate([-sin, sin, -sin, sin], axis=-1))


def _layer_mla(x2, b, s, norm_g, w_in, q_lat_g, kv_lat_g, w_uq, w_ukv, qk_g, w_out):
    q_rank, kv_rank = q_lat_g.shape[0], kv_lat_g.shape[0]
    heads = D_HEADS
    lat = q_rank + kv_rank
    w_main = jnp.concatenate([w_in[:, :lat], w_in[:, lat + D_ROPE:]], axis=1)
    w_kpe = w_in[:, lat:lat + D_ROPE]
    w_kpe = jnp.concatenate([w_kpe, w_kpe], axis=1).astype(BF16)
    uq = w_uq.reshape(q_rank, heads, D_NOPE + D_ROPE)
    w_uq_p = jnp.concatenate([uq[:, :, :D_NOPE].reshape(q_rank, -1),
                              uq[:, :, D_NOPE:].reshape(q_rank, -1)], axis=1).astype(BF16)
    ukv = w_ukv.reshape(kv_rank, heads, D_NOPE + D_V)
    w_ukv_p = jnp.concatenate([ukv[:, :, :D_NOPE].reshape(kv_rank, -1),
                               ukv[:, :, D_NOPE:].reshape(kv_rank, -1)], axis=1).astype(BF16)
    dup = lambda v: jnp.concatenate([v, v])
    gains = jnp.stack([qk_g[0, :D_NOPE], dup(qk_g[0, D_NOPE:]),
                       qk_g[1, :D_NOPE], dup(qk_g[1, D_NOPE:])]).astype(F32)
    cos, sin_signed = _rope_tables(s)

    proj, kpe = _norm_matmul(x2, norm_g, w_main, out_dtype=BF16, tm=MM_TM, tn=MM_TN,
                             w_side=w_kpe)
    q = _norm_matmul(proj, q_lat_g, w_uq_p, out_dtype=BF16, tm=MM_TM,
                     tn=w_uq_p.shape[1], col_block=0)
    kv = _norm_matmul(proj, kv_lat_g, w_ukv_p, out_dtype=BF16, tm=MM_TM,
                      tn=w_ukv_p.shape[1], col_block=1)
    gate_block0 = lat // (2 * LANES)
    y = _mla_attention(q.reshape(b, s, -1), kv.reshape(b, s, -1),
                       kpe.reshape(b, s, LANES), proj.reshape(b, s, -1),
                       gate_block0, cos, sin_signed, gains)
    return _matmul_residual(y.reshape(b * s, -1), w_out.astype(BF16), x2,
                            tm=RES_TM)


def kernel(x, norm_g, rel_bias, a_w_in, a_qk_g, a_lambda, a_subln_g, a_w_out, b_w_in, b_w_gate, b_gate_bias, b_out_g, b_w_out, c_w_in, c_conv_w, c_conv_b, c_w_rgate, c_b_rgate, c_w_igate, c_b_igate, c_lambda, c_w_out, d_w_in, d_q_lat_g, d_kv_lat_g, d_w_uq, d_w_ukv, d_qk_g, d_w_out):
    b, s, d = x.shape
    depth = norm_g.shape[0]
    x2 = x.reshape(b * s, d)
    bias_tiles = _t5_tiles(rel_bias, ATTN_TQ)
    for i in range(depth):
        m, j = i % 4, i // 4
        if m == 0:
            x2 = _layer_diff(x2, b, s, norm_g[i], a_w_in[j], a_qk_g[j], a_lambda[j],
                             a_subln_g[j], a_w_out[j], bias_tiles, i)
        elif m == 1:
            x2 = _layer_gla(x2, b, s, norm_g[i], b_w_in[j], b_w_gate[j],
                            b_gate_bias[j], b_out_g[j], b_w_out[j])
        elif m == 2:
            x2 = _layer_rglru(x2, b, s, norm_g[i], c_w_in[j], c_conv_w[j], c_conv_b[j],
                              c_w_rgate[j], c_b_rgate[j], c_w_igate[j], c_b_igate[j],
                              c_lambda[j], c_w_out[j])
        else:
            x2 = _layer_mla(x2, b, s, norm_g[i], d_w_in[j], d_q_lat_g[j],
                            d_kv_lat_g[j], d_w_uq[j], d_w_ukv[j], d_qk_g[j], d_w_out[j])
    return x2.reshape(b, s, d)
```

```python
import functools
import math

import jax
import jax.numpy as jnp
from jax import lax
from jax.experimental import pallas as pl
from jax.experimental.pallas import tpu as pltpu

F32 = jnp.float32
BF16 = jnp.bfloat16

EPS = 1e-6
NEG_INF = -1e30
LOG2E = math.log2(math.e)
CHUNK = 64
LANES = 128
SUBLANES = 8
BF16_ROWS = 16
V7X_VMEM_BYTES = 64 * 1024 * 1024
VMEM_LIMIT = V7X_VMEM_BYTES - 8 * 1024 * 1024

A_HEADS = 16
A_HEAD_DIM = 64
T5_BUCKETS = 32
B_HEADS = 4
B_GATE_TAU = 16.0
C_BLOCKS = 8
C_CONV = 4
C_C = 8.0
D_HEADS = 16
D_NOPE = 128
D_ROPE = 64
D_V = 128
ROPE_THETA = 10000.0

ATTN_TQ = 512
V_ROWS = LANES + BF16_ROWS
FAR_UNROLL = 6


def _cparams(*semantics):
    return pltpu.CompilerParams(dimension_semantics=semantics,
                                vmem_limit_bytes=VMEM_LIMIT)


def _sigmoid(x):
    return 0.5 * jnp.tanh(0.5 * x) + 0.5


def _silu(g):
    h = 0.5 * g
    return h * jnp.tanh(h) + h


def _softplus(x):
    return jnp.maximum(x, 0.0) + jnp.log(1.0 + jnp.exp(-jnp.abs(x)))


def _one_minus_exp2(y):
    t = jnp.tanh(y)
    return (-2.0 * t) / (1.0 - t)


def _nt_dot(a, b):
    return lax.dot_general(a, b, (((1,), (1,)), ((), ())),
                           preferred_element_type=F32)


def _norm_matmul_kernel(*refs, has_side, n_first):
    refs = list(refs)
    x_ref, g_ref, w_ref = refs[:3]
    del refs[:3]
    ws_ref = refs.pop(0) if has_side else None
    o_ref = refs.pop(0)
    rest_ref = refs.pop(0) if n_first is not None else None
    side_ref = refs.pop(0) if has_side else None
    h_ref = refs.pop(0)
    j = pl.program_id(1)

    @pl.when(j == 0)
    def _():
        x = x_ref[...].astype(F32)
        ms = jnp.mean(x * x, axis=-1, keepdims=True)
        h_ref[...] = (x * lax.rsqrt(ms + EPS) * g_ref[...]).astype(BF16)
        if has_side:
            side_ref[...] = jnp.dot(h_ref[...], ws_ref[...],
                                    preferred_element_type=F32).astype(side_ref.dtype)

    w = w_ref[...]
    if w.dtype != BF16:
        w = w.astype(BF16)
    res = jnp.dot(h_ref[...], w, preferred_element_type=F32)
    if n_first is None:
        o_ref[...] = res.astype(o_ref.dtype)
    else:
        @pl.when(j < n_first)
        def _():
            o_ref[...] = res.astype(o_ref.dtype)

        @pl.when(j >= n_first)
        def _():
            rest_ref[...] = res.astype(rest_ref.dtype)


def _norm_matmul(x, gain, w, *, out_dtype, tm, tn, col_block=0, n_cols=None,
                 w_side=None, first=None):
    m = x.shape[0]
    k = w.shape[0]
    n = w.shape[1] if n_cols is None else n_cols
    n_first = None if first is None else first[0] // tn
    in_specs = [pl.BlockSpec((tm, k), lambda i, j: (i, col_block)),
                pl.BlockSpec((1, k), lambda i, j: (0, 0)),
                pl.BlockSpec((k, tn), lambda i, j: (0, j))]
    args = [x, gain.reshape(1, k).astype(F32), w]
    if w_side is not None:
        in_specs.append(pl.BlockSpec((k, LANES), lambda i, j: (0, 0)))
        args.append(w_side)
    if first is None:
        out_specs = [pl.BlockSpec((tm, tn), lambda i, j: (i, j))]
        out_shape = [jax.ShapeDtypeStruct((m, n), out_dtype)]
    else:
        out_specs = [pl.BlockSpec((tm, tn), lambda i, j: (i, jnp.minimum(j, n_first - 1))),
                     pl.BlockSpec((tm, tn), lambda i, j: (i, jnp.maximum(j - n_first, 0)))]
        out_shape = [jax.ShapeDtypeStruct((m, first[0]), first[1]),
                     jax.ShapeDtypeStruct((m, n - first[0]), out_dtype)]
    if w_side is not None:
        out_specs.append(pl.BlockSpec((tm, LANES), lambda i, j: (i, 0)))
        out_shape.append(jax.ShapeDtypeStruct((m, LANES), BF16))
    outs = pl.pallas_call(
        functools.partial(_norm_matmul_kernel, has_side=w_side is not None,
                          n_first=n_first),
        grid=(m // tm, n // tn),
        in_specs=in_specs,
        out_specs=out_specs,
        out_shape=out_shape,
        scratch_shapes=[pltpu.VMEM((tm, k), BF16)],
        compiler_params=_cparams("parallel", "arbitrary"),
        name="norm_matmul",
    )(*args)
    return outs[0] if len(outs) == 1 else tuple(outs)


def _matmul_residual_kernel(y_ref, w_ref, x_ref, o_ref):
    o_ref[...] = x_ref[...] + jnp.dot(y_ref[...], w_ref[...],
                                      preferred_element_type=F32)


def _matmul_residual(y, w, x, *, tm):
    m, k = y.shape
    n = w.shape[1]
    return pl.pallas_call(
        _matmul_residual_kernel,
        grid=(m // tm,),
        in_specs=[pl.BlockSpec((tm, k), lambda i: (i, 0)),
                  pl.BlockSpec((k, n), lambda i: (0, 0)),
                  pl.BlockSpec((tm, n), lambda i: (i, 0))],
        out_specs=pl.BlockSpec((tm, n), lambda i: (i, 0)),
        out_shape=jax.ShapeDtypeStruct((m, n), F32),
        compiler_params=_cparams("parallel"),
        name="matmul_residual",
    )(y, w, x)


def _group_rmsnorm(x, gain, group):
    r = lax.broadcasted_iota(jnp.int32, (LANES, LANES), 0)
    c = lax.broadcasted_iota(jnp.int32, (LANES, LANES), 1)
    ones = jnp.where(r // group == c // group, 1.0, 0.0).astype(BF16)
    x2 = x * x
    hi = x2.astype(BF16)
    lo = (x2 - hi.astype(F32)).astype(BF16)
    ss = (jnp.dot(hi, ones, preferred_element_type=F32)
          + jnp.dot(lo, ones, preferred_element_type=F32))
    return x * lax.rsqrt(ss * (1.0 / group) + EPS) * gain


def _half_rmsnorm(x, gain, lo_mask):
    x2 = x * x
    lo = jnp.sum(jnp.where(lo_mask, x2, 0.0), axis=-1, keepdims=True)
    hi = jnp.sum(jnp.where(lo_mask, 0.0, x2), axis=-1, keepdims=True)
    ms = jnp.where(lo_mask, lo, hi) * (1.0 / 64)
    return x * lax.rsqrt(ms + EPS) * gain


def _store_vt_block(vt_ref, lead, v_rows):
    tq = v_rows.shape[0]
    r = lax.broadcasted_iota(jnp.int32, (LANES, LANES), 0)
    c = lax.broadcasted_iota(jnp.int32, (LANES, LANES), 1)
    eye = jnp.where(r == c, 1.0, 0.0).astype(BF16)
    vt_ref[(*lead, slice(0, LANES), slice(None))] = _nt_dot(eye, v_rows).astype(BF16)
    vt_ref[(*lead, slice(LANES, V_ROWS), slice(None))] = jnp.ones((BF16_ROWS, tq), BF16)


def _flash_init(s, vt):
    m = jnp.max(s, axis=0, keepdims=True)
    p = jnp.exp2(s - m).astype(BF16)
    return m, jnp.dot(vt, p, preferred_element_type=F32)


def _flash_update(s, vt, m, acc):
    m_new = jnp.maximum(m, jnp.max(s, axis=0, keepdims=True))
    alpha = jnp.exp2(m - m_new)
    p = jnp.exp2(s - m_new).astype(BF16)
    return m_new, alpha * acc + jnp.dot(vt, p, preferred_element_type=F32)


def _chunk_causal_flash(i, streams, acc_ref, s_ref):
    first = jnp.maximum(i - 1, 0)
    n_far = first

    def issue(blk, slot, add=None):
        for si, (scores, _, near_add) in enumerate(streams):
            s = scores(blk)
            s_ref[si, slot] = s if add is None else s + near_add(add)

    def update(blk, slot, ms):
        out = []
        for si, (_, values, _) in enumerate(streams):
            m, acc_ref[si] = _flash_update(s_ref[si, slot], values(blk), ms[si],
                                           acc_ref[si])
            out.append(m)
        return tuple(out)

    def group(t0, count, prefetch_next, ms):
        for k in range(count):
            if k + 1 < count or prefetch_next:
                issue(t0 + k + 1, (k + 1) % 2)
            ms = update(t0 + k, k % 2, ms)
        return ms

    issue(first, 0, add=0)
    issue(first + 1, 1, add=1)
    ms = []
    for si, (_, values, _) in enumerate(streams):
        m, acc_ref[si] = _flash_init(s_ref[si, 0], values(first))
        ms.append(m)
    issue(0, 0)
    ms = update(first + 1, 1, tuple(ms))

    full = jnp.maximum(n_far - 1, 0) // FAR_UNROLL
    rest = n_far - FAR_UNROLL * full
    ms = lax.fori_loop(0, full,
                       lambda g, ms: group(g * FAR_UNROLL, FAR_UNROLL, True, ms), ms)
    for count in range(1, FAR_UNROLL + 1):
        ms = lax.fori_loop(0, (rest == count).astype(jnp.int32),
                           lambda _, ms, count=count: group(n_far - count, count, False, ms),
                           ms)


def _normalised(acc):
    return acc[0:LANES] * (1.0 / acc[LANES:LANES + 1])


_T5_LARGE_THRESHOLDS = (12, 16, 23, 32, 46, 64, 91)


def _t5_tiles_kernel(rb_ref, o_ref, *, tq):
    h = pl.program_id(0)
    variant = pl.program_id(1)
    j = lax.broadcasted_iota(jnp.int32, (1, tq), 1)
    rel_row = jnp.where(j < tq // 2, -j, tq - j)
    n = jnp.abs(rel_row)
    large = jnp.full_like(n, 8)
    for t in _T5_LARGE_THRESHOLDS:
        large = large + (n >= t).astype(jnp.int32)
    bucket = jnp.where(rel_row > 0, T5_BUCKETS // 2, 0) + jnp.where(n < 8, n, large)
    far = rb_ref[T5_BUCKETS // 2 - 1, h]
    row = jnp.zeros((1, tq), F32)
    for b in range(T5_BUCKETS):
        row = jnp.where(bucket == b, (rb_ref[b, h] - far) * LOG2E, row)
    tile = pltpu.roll(jnp.broadcast_to(row, (2 * tq, tq)), 0, 1, stride=1, stride_axis=0)
    c = lax.broadcasted_iota(jnp.int32, (2 * tq, tq), 0)
    r = lax.broadcasted_iota(jnp.int32, (2 * tq, tq), 1)
    key_off = c - tq * variant
    bias = jnp.where(key_off - r > -_T5_LARGE_THRESHOLDS[-1], tile, 0.0)
    visible = (lax.shift_right_arithmetic(key_off, 6)
               <= lax.shift_right_arithmetic(r, 6))
    o_ref[0, 0] = jnp.where(visible, bias, NEG_INF)


def _t5_tiles(rel_bias, tq):
    nb, nh = rel_bias.shape
    return pl.pallas_call(
        functools.partial(_t5_tiles_kernel, tq=tq),
        grid=(nh, 2),
        in_specs=[pl.BlockSpec(memory_space=pltpu.SMEM)],
        out_specs=pl.BlockSpec((1, 1, 2 * tq, tq), lambda h, v: (h, v, 0, 0)),
        out_shape=jax.ShapeDtypeStruct((nh, 2, 2 * tq, tq), F32),
        compiler_params=_cparams("parallel", "arbitrary"),
        name="t5_tiles",
    )(rel_bias.astype(F32))


def _diff_attn_kernel(lamv_ref, q_ref, k_ref, v_ref, g_ref, bias_ref, qkg_ref,
                      sub_ref, o_ref, kn_ref, vt_ref, acc_ref, s_ref, *, tq, lam_init):
    i = pl.program_id(2)
    seq = k_ref.shape[1]
    lane = lax.broadcasted_iota(jnp.int32, (1, LANES), 1)
    lo_mask = lane < A_HEAD_DIM

    @pl.when(i == 0)
    def _():
        def body(c, carry):
            rows = pl.ds(pl.multiple_of(c * tq, tq), tq)
            kn_ref[rows, :] = _group_rmsnorm(
                k_ref[0, rows, :].astype(F32), qkg_ref[1:2, :], A_HEAD_DIM).astype(BF16)
            _store_vt_block(vt_ref, (c,), v_ref[0, rows, :])
            return carry
        lax.fori_loop(0, seq // tq, body, 0, unroll=2)

    lf = lamv_ref[...].astype(F32)
    lam = (jnp.exp(jnp.sum(lf[0:1] * lf[1:2], axis=-1, keepdims=True))
           - jnp.exp(jnp.sum(lf[2:3] * lf[3:4], axis=-1, keepdims=True)) + lam_init)

    qn = (_half_rmsnorm(q_ref[0].astype(F32), qkg_ref[0:1, :], lo_mask)
          * (A_HEAD_DIM ** -0.5 * LOG2E))
    q01 = jnp.concatenate([jnp.where(lo_mask, qn, 0.0).astype(BF16),
                           jnp.where(lo_mask, 0.0, qn).astype(BF16)], axis=0)

    def scores(blk):
        return _nt_dot(kn_ref[pl.ds(pl.multiple_of(blk * tq, tq), tq), :], q01)

    def near_add(j):
        bias_t = bias_ref[0, 0, j * tq:(j + 1) * tq, :]
        return jnp.concatenate([bias_t, bias_t], axis=1)

    _chunk_causal_flash(i, [(scores, lambda blk: vt_ref[blk], near_add)], acc_ref,
                        s_ref)

    o01 = _normalised(acc_ref[0])
    o_t = o01[:, 0:tq] - lam * o01[:, tq:]
    ms_t = jnp.mean(o_t * o_t, axis=0, keepdims=True)
    o_t = o_t * lax.rsqrt(ms_t + EPS) * sub_ref[...] * (1.0 - lam_init)
    o_ref[0] = (o_t.T * _silu(g_ref[0].astype(F32))).astype(o_ref.dtype)


def _diff_attention(proj, lam_vecs, qk_g, subln_g, bias_tiles, layer_idx):
    b, s, four_w = proj.shape
    width = four_w // 4
    hblocks = width // LANES
    tq = ATTN_TQ
    lam_init = 0.8 - 0.6 * math.exp(-0.3 * layer_idx)
    qkg = jnp.concatenate([qk_g, qk_g], axis=-1).astype(F32)
    return pl.pallas_call(
        functools.partial(_diff_attn_kernel, tq=tq, lam_init=lam_init),
        grid=(b, hblocks, s // tq),
        in_specs=[
            pl.BlockSpec((4, A_HEAD_DIM), lambda bi, h, i: (0, 0)),
            pl.BlockSpec((1, tq, LANES), lambda bi, h, i: (bi, i, h)),
            pl.BlockSpec((1, s, LANES), lambda bi, h, i: (bi, 0, hblocks + h)),
            pl.BlockSpec((1, s, LANES), lambda bi, h, i: (bi, 0, 2 * hblocks + h)),
            pl.BlockSpec((1, tq, LANES), lambda bi, h, i: (bi, i, 3 * hblocks + h)),
            pl.BlockSpec((1, 1, 2 * tq, tq),
                         lambda bi, h, i: (h, jnp.minimum(i, 1), 0, 0)),
            pl.BlockSpec((2, LANES), lambda bi, h, i: (0, 0)),
            pl.BlockSpec((LANES, 1), lambda bi, h, i: (0, 0)),
        ],
        out_specs=pl.BlockSpec((1, tq, LANES), lambda bi, h, i: (bi, i, h)),
        out_shape=jax.ShapeDtypeStruct((b, s, width), BF16),
        scratch_shapes=[pltpu.VMEM((s, LANES), BF16),
                        pltpu.VMEM((s // tq, V_ROWS, tq), BF16),
                        pltpu.VMEM((1, V_ROWS, 2 * tq), F32),
                        pltpu.VMEM((1, 2, tq, 2 * tq), F32)],
        compiler_params=_cparams("parallel", "parallel", "arbitrary"),
        name="diff_attention",
    )(lam_vecs.astype(F32), proj, proj, proj, proj, bias_tiles, qkg,
      subln_g.reshape(LANES, 1).astype(F32))


def _gla_kernel(q_ref, k_ref, v_ref, g_ref, lr_ref, wgt_ref, gb_ref, og_ref,
                cm_ref, o_ref, state_ref, *, ts, dk):
    @pl.when(pl.program_id(2) == 0)
    def _():
        state_ref[...] = jnp.zeros_like(state_ref)

    heads = q_ref.shape[2] // dk
    dv = v_ref.shape[2] // heads
    span = cm_ref.shape[0]
    la_t, rest_t, kd_t = [], [], []
    for hh in range(heads):
        hk = slice(hh * dk, (hh + 1) * dk)
        z = _nt_dot(wgt_ref[hk, :], lr_ref[0]) + gb_ref[hk, :]
        la = -_softplus(-z) * (1.0 / B_GATE_TAU)
        la_hi = la.astype(BF16)
        la_lo = (la - la_hi.astype(F32)).astype(BF16)
        rest = jnp.concatenate(
            [jnp.dot(la_hi[:, s0:s0 + span], cm_ref[...], preferred_element_type=F32)
             + jnp.dot(la_lo[:, s0:s0 + span], cm_ref[...], preferred_element_type=F32)
             for s0 in range(0, ts, span)], axis=1)
        la_t.append(la)
        rest_t.append(rest)
        kd_t.append((k_ref[0, :, hk].astype(F32).T * jnp.exp(rest)).astype(BF16))

    lane = lax.broadcasted_iota(jnp.int32, (1, LANES), 1)
    first_chunk = lane < CHUNK
    states = [state_ref[hh] for hh in range(heads)]
    for c in range(ts // CHUNK):
        group = slice((c // 2) * LANES, (c // 2 + 1) * LANES)
        keep = first_chunk if c % 2 == 0 else jnp.logical_not(first_chunk)
        t0 = slice(c * CHUNK, c * CHUNK + 1)
        rows = slice(c * CHUNK, (c + 1) * CHUNK)
        for hh in range(heads):
            hk = slice(hh * dk, (hh + 1) * dk)
            hv = slice(hh * dv, (hh + 1) * dv)
            kd_c = jnp.where(keep, kd_t[hh][:, group], jnp.zeros((), BF16))
            states[hh] = (jnp.exp(rest_t[hh][:, t0] + la_t[hh][:, t0]) * states[hh]
                          + jnp.dot(kd_c, v_ref[0, group, hv], preferred_element_type=F32))
            q = (q_ref[0, rows, hk].astype(F32) * (dk ** -0.5)).astype(BF16)
            o = jnp.dot(q, states[hh].astype(BF16), preferred_element_type=F32)
            ms = jnp.mean(o * o, axis=-1, keepdims=True)
            o = o * lax.rsqrt(ms + EPS) * og_ref[...]
            o_ref[0, rows, hv] = (o * _silu(g_ref[0, rows, hv].astype(F32))).astype(o_ref.dtype)
    for hh in range(heads):
        state_ref[hh] = states[hh]


GLA_HEADS_PER_STEP = 4


def _gla(proj, lr, w_gate_t, gate_bias, out_g, *, ts):
    b, s, n = proj.shape
    heads = B_HEADS
    dk = n // (6 * heads)
    dv = 2 * dk
    hp = GLA_HEADS_PER_STEP
    steps = heads // hp
    wk, wv = hp * dk, hp * dv
    k0 = (heads * dk) // wk
    v0 = (2 * heads * dk) // wv
    g0 = v0 + steps
    span = 2 * LANES
    t_in = jnp.arange(span)[:, None]
    t_out = jnp.arange(span)[None, :]
    chunk_masks = (((t_in // CHUNK) == (t_out // CHUNK)) & (t_in > t_out)).astype(BF16)
    return pl.pallas_call(
        functools.partial(_gla_kernel, ts=ts, dk=dk),
        grid=(b, steps, s // ts),
        in_specs=[
            pl.BlockSpec((1, ts, wk), lambda bi, h, i: (bi, i, h)),
            pl.BlockSpec((1, ts, wk), lambda bi, h, i: (bi, i, k0 + h)),
            pl.BlockSpec((1, ts, wv), lambda bi, h, i: (bi, i, v0 + h)),
            pl.BlockSpec((1, ts, wv), lambda bi, h, i: (bi, i, g0 + h)),
            pl.BlockSpec((1, ts, LANES), lambda bi, h, i: (bi, i, 0)),
            pl.BlockSpec((wk, LANES), lambda bi, h, i: (h, 0)),
            pl.BlockSpec((wk, 1), lambda bi, h, i: (h, 0)),
            pl.BlockSpec((1, dv), lambda bi, h, i: (0, 0)),
            pl.BlockSpec((span, span), lambda bi, h, i: (0, 0)),
        ],
        out_specs=pl.BlockSpec((1, ts, wv), lambda bi, h, i: (bi, i, h)),
        out_shape=jax.ShapeDtypeStruct((b, s, heads * dv), BF16),
        scratch_shapes=[pltpu.VMEM((hp, dk, dv), F32)],
        compiler_params=_cparams("parallel", "parallel", "arbitrary"),
        name="gla",
    )(proj, proj, proj, proj, lr, w_gate_t, gate_bias.reshape(-1, 1).astype(F32),
      out_g.reshape(1, dv).astype(F32), chunk_masks)


def _rglru_kernel(u_ref, g_ref, cw_ref, cb_ref, wr_ref, br_ref, wi_ref, bi_ref,
                  lam_ref, o_ref, ubuf_ref, xc_ref, a_ref, b_ref, h_ref, *, ts):
    width = u_ref.shape[2]
    bd = width // C_BLOCKS

    @pl.when(pl.program_id(1) == 0)
    def _():
        ubuf_ref[0:SUBLANES, :] = jnp.zeros((SUBLANES, width), F32)
        h_ref[...] = jnp.zeros_like(h_ref)

    ubuf_ref[SUBLANES:SUBLANES + ts, :] = u_ref[0]
    ue = ubuf_ref[...]
    xc = ue * cw_ref[0:1, :]
    for t in range(1, C_CONV):
        xc = pltpu.roll(xc, 1, 0) + ue * cw_ref[t:t + 1, :]
    xc_ref[...] = xc[SUBLANES:, :] + cb_ref[...]
    ubuf_ref[0:SUBLANES, :] = ubuf_ref[ts:ts + SUBLANES, :]

    rate = -C_C * _softplus(-lam_ref[...])
    for n in range(C_BLOCKS):
        cols = slice(n * bd, (n + 1) * bd)
        xb = xc_ref[:, cols]
        xb16 = xb.astype(BF16)
        r = _sigmoid(jnp.dot(xb16, wr_ref[n], preferred_element_type=F32)
                     + br_ref[:, cols])
        gate_i = _sigmoid(jnp.dot(xb16, wi_ref[n], preferred_element_type=F32)
                          + bi_ref[:, cols])
        log_a = r * rate[:, cols]
        a_ref[:, cols] = jnp.exp(log_a)
        b_ref[:, cols] = jnp.sqrt(_one_minus_exp2(log_a)) * (gate_i * xb)

    row = lax.broadcasted_iota(jnp.int32, (SUBLANES, width), 0)

    def tile(t, h_prev):
        rows = pl.ds(pl.multiple_of(t * SUBLANES, SUBLANES), SUBLANES)
        a = a_ref[rows, :]
        b = b_ref[rows, :]
        d = 1
        while d < SUBLANES:
            a_up = pltpu.roll(a, d, 0)
            b_up = pltpu.roll(b, d, 0)
            keep = row >= d
            b = jnp.where(keep, a * b_up + b, b)
            a = jnp.where(keep, a * a_up, a)
            d *= 2
        h = b + a * h_prev
        b_ref[rows, :] = h
        return h[SUBLANES - 1:SUBLANES, :]

    h_ref[...] = lax.fori_loop(0, ts // SUBLANES, tile, h_ref[...])
    o_ref[0] = (b_ref[...] * _silu(g_ref[0].astype(F32))).astype(o_ref.dtype)


def _rglru(u, gate, gate_block, conv_w, conv_b, w_rg, b_rg, w_ig, b_ig, lam, *, ts):
    b, s, width = u.shape
    bd = width // C_BLOCKS
    row = lambda a: a.reshape(1, width).astype(F32)
    const2 = lambda bi, i: (0, 0)
    return pl.pallas_call(
        functools.partial(_rglru_kernel, ts=ts),
        grid=(b, s // ts),
        in_specs=[
            pl.BlockSpec((1, ts, width), lambda bi, i: (bi, i, 0)),
            pl.BlockSpec((1, ts, width), lambda bi, i: (bi, i, gate_block)),
            pl.BlockSpec((C_CONV, width), const2),
            pl.BlockSpec((1, width), const2),
            pl.BlockSpec((C_BLOCKS, bd, bd), lambda bi, i: (0, 0, 0)),
            pl.BlockSpec((1, width), const2),
            pl.BlockSpec((C_BLOCKS, bd, bd), lambda bi, i: (0, 0, 0)),
            pl.BlockSpec((1, width), const2),
            pl.BlockSpec((1, width), const2),
        ],
        out_specs=pl.BlockSpec((1, ts, width), lambda bi, i: (bi, i, 0)),
        out_shape=jax.ShapeDtypeStruct((b, s, width), BF16),
        scratch_shapes=[pltpu.VMEM((ts + SUBLANES, width), F32),
                        pltpu.VMEM((ts, width), F32),
                        pltpu.VMEM((ts, width), F32),
                        pltpu.VMEM((ts, width), F32),
                        pltpu.VMEM((1, width), F32)],
        compiler_params=_cparams("parallel", "arbitrary"),
        name="rglru",
    )(u, gate, conv_w.astype(F32), row(conv_b), w_rg.astype(BF16), row(b_rg),
      w_ig.astype(BF16), row(b_ig), row(lam))


def _swap_rope_halves(x):
    lane = lax.broadcasted_iota(jnp.int32, (1, LANES), 1)
    first = (lane % D_ROPE) < (D_ROPE // 2)
    return jnp.where(first, pltpu.roll(x, LANES - D_ROPE // 2, 1),
                     pltpu.roll(x, D_ROPE // 2, 1))


def _rope(x, cos, sin_signed):
    return x * cos + _swap_rope_halves(x) * sin_signed


def _mla_attn_kernel(qn_ref, qp_ref, kn_ref, kp_ref, v_ref, g_ref, cos_ref,
                     sin_ref, gains_ref, o_ref, kcat_ref, vt_ref, acc_ref, s_ref,
                     *, tq):
    i = pl.program_id(2)
    seq = kn_ref.shape[1]
    lane = lax.broadcasted_iota(jnp.int32, (1, LANES), 1)
    lo_mask = lane < D_ROPE
    g_q_nope, g_q_pe = gains_ref[0:1, :], gains_ref[1:2, :]
    g_k_nope, g_k_pe = gains_ref[2:3, :], gains_ref[3:4, :]

    @pl.when(i == 0)
    def _():
        def body(c, carry):
            rows = pl.ds(pl.multiple_of(c * tq, tq), tq)
            kp = _rope(_group_rmsnorm(kp_ref[0, rows, :].astype(F32), g_k_pe, D_ROPE),
                       cos_ref[rows, :], sin_ref[rows, :])
            kn = kn_ref[0, rows, :].astype(F32)
            kcat_ref[0, rows, 0:LANES] = _group_rmsnorm(
                kn[:, 0:LANES], g_k_nope, D_NOPE).astype(BF16)
            kcat_ref[0, rows, LANES:] = jnp.where(lo_mask, kp, 0.0).astype(BF16)
            kcat_ref[1, rows, 0:LANES] = _group_rmsnorm(
                kn[:, LANES:], g_k_nope, D_NOPE).astype(BF16)
            kcat_ref[1, rows, LANES:] = jnp.where(lo_mask, 0.0, kp).astype(BF16)
            _store_vt_block(vt_ref, (0, c), v_ref[0, rows, 0:D_V])
            _store_vt_block(vt_ref, (1, c), v_ref[0, rows, D_V:])
            return carry
        lax.fori_loop(0, seq // tq, body, 0, unroll=2)

    scale = (D_NOPE + D_ROPE) ** -0.5 * LOG2E
    qrows = pl.ds(pl.multiple_of(i * tq, tq), tq)
    qp = _rope(_group_rmsnorm(qp_ref[0].astype(F32), g_q_pe, D_ROPE),
               cos_ref[qrows, :], sin_ref[qrows, :]) * scale
    qn = qn_ref[0].astype(F32)
    qcat = (
        jnp.concatenate([(_group_rmsnorm(qn[:, 0:LANES], g_q_nope, D_NOPE)
                          * scale).astype(BF16),
                         jnp.where(lo_mask, qp, 0.0).astype(BF16)], axis=-1),
        jnp.concatenate([(_group_rmsnorm(qn[:, LANES:], g_q_nope, D_NOPE)
                          * scale).astype(BF16),
                         jnp.where(lo_mask, 0.0, qp).astype(BF16)], axis=-1),
    )

    def near_add(j):
        c = lax.broadcasted_iota(jnp.int32, (tq, tq), 0) + tq * (j - jnp.minimum(i, 1))
        r = lax.broadcasted_iota(jnp.int32, (tq, tq), 1)
        visible = (lax.shift_right_arithmetic(c, 6) <= lax.shift_right_arithmetic(r, 6))
        return jnp.where(visible, 0.0, NEG_INF).astype(F32)

    def stream(hh):
        def scores(blk):
            rows = pl.ds(pl.multiple_of(blk * tq, tq), tq)
            return _nt_dot(kcat_ref[hh, rows, :], qcat[hh])
        return scores, lambda blk: vt_ref[hh, blk], near_add

    _chunk_causal_flash(i, [stream(0), stream(1)], acc_ref, s_ref)

    g = g_ref[0].astype(F32)
    o_ref[0, :, 0:D_V] = (_normalised(acc_ref[0]).T * _silu(g[:, 0:D_V])).astype(o_ref.dtype)
    o_ref[0, :, D_V:] = (_normalised(acc_ref[1]).T * _silu(g[:, D_V:])).astype(o_ref.dtype)


def _mla_attention(q, kv, kpe, proj, gate_block0, cos, sin_signed, gains):
    b, s, _ = q.shape
    heads = D_HEADS
    pairs = heads // 2
    tq = ATTN_TQ
    pw = 2 * LANES
    return pl.pallas_call(
        functools.partial(_mla_attn_kernel, tq=tq),
        grid=(b, pairs, s // tq),
        in_specs=[
            pl.BlockSpec((1, tq, pw), lambda bi, p, i: (bi, i, p)),
            pl.BlockSpec((1, tq, LANES), lambda bi, p, i: (bi, i, 2 * pairs + p)),
            pl.BlockSpec((1, s, pw), lambda bi, p, i: (bi, 0, p)),
            pl.BlockSpec((1, s, LANES), lambda bi, p, i: (bi, 0, 0)),
            pl.BlockSpec((1, s, pw), lambda bi, p, i: (bi, 0, pairs + p)),
            pl.BlockSpec((1, tq, pw), lambda bi, p, i: (bi, i, gate_block0 + p)),
            pl.BlockSpec((s, LANES), lambda bi, p, i: (0, 0)),
            pl.BlockSpec((s, LANES), lambda bi, p, i: (0, 0)),
            pl.BlockSpec((4, LANES), lambda bi, p, i: (0, 0)),
        ],
        out_specs=pl.BlockSpec((1, tq, pw), lambda bi, p, i: (bi, i, p)),
        out_shape=jax.ShapeDtypeStruct((b, s, heads * D_V), BF16),
        scratch_shapes=[pltpu.VMEM((2, s, pw), BF16),
                        pltpu.VMEM((2, s // tq, V_ROWS, tq), BF16),
                        pltpu.VMEM((2, V_ROWS, tq), F32),
                        pltpu.VMEM((2, 2, tq, tq), F32)],
        compiler_params=_cparams("parallel", "parallel", "arbitrary"),
        name="mla_attention",
    )(q, q, kv, kpe, kv, proj, cos, sin_signed, gains)


MM_TM = 1024
MM_TN = 1024
RES_TM = 512


def _layer_diff(x2, b, s, norm_g, w_in, qk_g, lam_vecs, subln_g, w_out, bias_tiles,
                layer_idx):
    proj = _norm_matmul(x2, norm_g, w_in, out_dtype=BF16, tm=MM_TM, tn=MM_TN)
    y = _diff_attention(proj.reshape(b, s, -1), lam_vecs, qk_g, subln_g,
                        bias_tiles, layer_idx)
    return _matmul_residual(y.reshape(b * s, -1), w_out.astype(BF16), x2,
                            tm=RES_TM)


def _layer_gla(x2, b, s, norm_g, w_in, w_gate, gate_bias, out_g, w_out):
    rank, hdk = w_gate.shape
    n_main = w_in.shape[1] - rank
    w_lr = jnp.pad(w_in[:, n_main:], ((0, 0), (0, LANES - rank))).astype(BF16)
    w_gate_t = jnp.pad(w_gate, ((0, LANES - rank), (0, 0))).T.astype(BF16)
    proj, lr = _norm_matmul(x2, norm_g, w_in, out_dtype=BF16, tm=MM_TM, tn=MM_TN,
                            n_cols=n_main, w_side=w_lr)
    y = _gla(proj.reshape(b, s, -1), lr.reshape(b, s, LANES), w_gate_t, gate_bias,
             out_g, ts=1024)
    return _matmul_residual(y.reshape(b * s, -1), w_out.astype(BF16), x2,
                            tm=RES_TM)


def _layer_rglru(x2, b, s, norm_g, w_in, conv_w, conv_b, w_rg, b_rg, w_ig, b_ig,
                 lam, w_out):
    width = w_in.shape[1] // 2
    u, gate = _norm_matmul(x2, norm_g, w_in.astype(BF16), out_dtype=BF16,
                           tm=MM_TM, tn=MM_TN, first=(width, F32))
    y = _rglru(u.reshape(b, s, width), gate.reshape(b, s, width), 0, conv_w, conv_b,
               w_rg, b_rg, w_ig, b_ig, lam, ts=512)
    return _matmul_residual(y.reshape(b * s, -1), w_out.astype(BF16), x2,
                            tm=RES_TM)


def _rope_tables(s):
    half = D_ROPE // 2
    inv = ROPE_THETA ** (-jnp.arange(half, dtype=F32) / half)
    ang = jnp.arange(s, dtype=F32)[:, None] * inv[None, :]
    cos, sin = jnp.cos(ang), jnp.sin(ang)
    return (jnp.concatenate([cos, cos, cos, cos], axis=-1),
            jnp.concatenate([-sin, sin, -sin, sin], axis=-1))


def _layer_mla(x2, b, s, norm_g, w_in, q_lat_g, kv_lat_g, w_uq, w_ukv, qk_g, w_out):
    q_rank, kv_rank = q_lat_g.shape[0], kv_lat_g.shape[0]
    heads = D_HEADS
    lat = q_rank + kv_rank
    w_main = jnp.concatenate([w_in[:, :lat], w_in[:, lat + D_ROPE:]], axis=1)
    w_kpe = w_in[:, lat:lat + D_ROPE]
    w_kpe = jnp.concatenate([w_kpe, w_kpe], axis=1).astype(BF16)
    uq = w_uq.reshape(q_rank, heads, D_NOPE + D_ROPE)
    w_uq_p = jnp.concatenate([uq[:, :, :D_NOPE].reshape(q_rank, -1),
                              uq[:, :, D_NOPE:].reshape(q_rank, -1)], axis=1).astype(BF16)
    ukv = w_ukv.reshape(kv_rank, heads, D_NOPE + D_V)
    w_ukv_p = jnp.concatenate([ukv[:, :, :D_NOPE].reshape(kv_rank, -1),
                               ukv[:, :, D_NOPE:].reshape(kv_rank, -1)], axis=1).astype(BF16)
    dup = lambda v: jnp.concatenate([v, v])
    gains = jnp.stack([qk_g[0, :D_NOPE], dup(qk_g[0, D_NOPE:]),
                       qk_g[1, :D_NOPE], dup(qk_g[1, D_NOPE:])]).astype(F32)
    cos, sin_signed = _rope_tables(s)

    proj, kpe = _norm_matmul(x2, norm_g, w_main, out_dtype=BF16, tm=MM_TM, tn=MM_TN,
                             w_side=w_kpe)
    q = _norm_matmul(proj, q_lat_g, w_uq_p, out_dtype=BF16, tm=MM_TM,
                     tn=w_uq_p.shape[1], col_block=0)
    kv = _norm_matmul(proj, kv_lat_g, w_ukv_p, out_dtype=BF16, tm=MM_TM,
                      tn=w_ukv_p.shape[1], col_block=1)
    gate_block0 = lat // (2 * LANES)
    y = _mla_attention(q.reshape(b, s, -1), kv.reshape(b, s, -1),
                       kpe.reshape(b, s, LANES), proj.reshape(b, s, -1),
                       gate_block0, cos, sin_signed, gains)
    return _matmul_residual(y.reshape(b * s, -1), w_out.astype(BF16), x2,
                            tm=RES_TM)


def kernel(x, norm_g, rel_bias, a_w_in, a_qk_g, a_lambda, a_subln_g, a_w_out, b_w_in, b_w_gate, b_gate_bias, b_out_g, b_w_out, c_w_in, c_conv_w, c_conv_b, c_w_rgate, c_b_rgate, c_w_igate, c_b_igate, c_lambda, c_w_out, d_w_in, d_q_lat_g, d_kv_lat_g, d_w_uq, d_w_ukv, d_qk_g, d_w_out):
    b, s, d = x.shape
    depth = norm_g.shape[0]
    x2 = x.reshape(b * s, d)
    bias_tiles = _t5_tiles(rel_bias, ATTN_TQ)
    for i in range(depth):
        m, j = i % 4, i // 4
        if m == 0:
            x2 = _layer_diff(x2, b, s, norm_g[i], a_w_in[j], a_qk_g[j], a_lambda[j],
                             a_subln_g[j], a_w_out[j], bias_tiles, i)
        elif m == 1:
            x2 = _layer_gla(x2, b, s, norm_g[i], b_w_in[j], b_w_gate[j],
                            b_gate_bias[j], b_out_g[j], b_w_out[j])
        elif m == 2:
            x2 = _layer_rglru(x2, b, s, norm_g[i], c_w_in[j], c_conv_w[j], c_conv_b[j],
                              c_w_rgate[j], c_b_rgate[j], c_w_igate[j], c_b_igate[j],
                              c_lambda[j], c_w_out[j])
        else:
            x2 = _layer_mla(x2, b, s, norm_g[i], d_w_in[j], d_q_lat_g[j],
                            d_kv_lat_g[j], d_w_uq[j], d_w_ukv[j], d_qk_g[j], d_w_out[j])
    return x2.reshape(b, s, d)
```
